```python
import math, functools
import jax
import jax.numpy as jnp
from jax import lax
import numpy as np

D_MODEL = 1024
BATCH = 8
SEQ = 4096
DEPTH = 1

CTX_LEN = 256
GRID_W = 64
D_MIX = D_MODEL
D_RWKV = D_MIX // 2
D_S5 = D_MIX - D_RWKV
HEAD_DIM = 64
N_HEADS = D_RWKV // HEAD_DIM
LORA_W = 64
LORA_A = 64
N_DIR = 2
S5_GROUP = 16
N_GROUPS = D_S5 // S5_GROUP
S5_STATE = 64
SHIFT_W = 3 * D_RWKV + N_DIR * (LORA_W + LORA_A)
D_IN = SHIFT_W + D_RWKV + 2 * D_S5
EPS = 1e-6
LN_X_EPS = 64e-5
F32 = jnp.float32

kernel_name = 'hybrid_rwkv7_s5_dit_layer'


def rmsnorm(x, g):
    xf = x.astype(F32)
    return xf * lax.rsqrt(jnp.mean(xf * xf, axis=-1, keepdims=True) + EPS) * g


def modulation(cond, w_ada, b_ada):
    m = jax.nn.silu(cond.astype(F32)) @ w_ada + b_ada
    return jnp.split(m, 3, axis=-1)


def q_shift_grid(z, rows):
    b, l, ch = z.shape
    zg = z.reshape(b, rows, GRID_W, ch // 4, 4)
    zp = jnp.pad(zg, ((0, 0), (1, 1), (1, 1), (0, 0), (0, 0)))
    left = zp[:, 1:-1, :-2, :, 0]
    right = zp[:, 1:-1, 2:, :, 1]
    up = zp[:, :-2, 1:-1, :, 2]
    down = zp[:, 2:, 1:-1, :, 3]
    return jnp.stack([left, right, up, down], axis=-1).reshape(b, l, ch)


def shift_1d(z):
    b, l, ch = z.shape
    zp = jnp.pad(z.reshape(b, l, ch // 4, 4), ((0, 0), (1, 1), (0, 0), (0, 0)))
    prev, nxt = zp[:, :-2], zp[:, 2:]
    return jnp.stack([prev[..., 0], nxt[..., 1], prev[..., 2], nxt[..., 3]], axis=-1).reshape(b, l, ch)


def heads(t):
    return t.reshape(t.shape[:-1] + (N_HEADS, HEAD_DIM))


def rwkv7_step(S, inp):
    r, w, k, v, a, b = inp
    sa = jnp.einsum('dbhvk,dbhk->dbhv', S, a)
    S = S * w[..., None, :] + sa[..., :, None] * b[..., None, :] + v[..., :, None] * k[..., None, :]
    return S, jnp.einsum('dbhvk,dbhk->dbhv', S, r)


def rwkv7_branch(zs, g_r, p, s0, with_output):
    bsz, l, _ = zs.shape
    r, k, v, zw, za = jnp.split(zs, [D_RWKV, 2 * D_RWKV, 3 * D_RWKV, 3 * D_RWKV + N_DIR * LORA_W], axis=-1)
    zw = zw.reshape(bsz, l, N_DIR, LORA_W)
    za = za.reshape(bsz, l, N_DIR, LORA_A)
    w_log = -jax.nn.softplus(-(p['w0'] + jnp.einsum('bldr,drc->bldc', jnp.tanh(zw), p['w2']))) - 0.5
    decay = jnp.exp(-jnp.exp(w_log.astype(F32)))
    a = jax.nn.sigmoid(p['a0'] + jnp.einsum('bldr,drc->bldc', za, p['a2']))
    kk = heads((k * p['k_k']).astype(F32))
    kk = kk / jnp.maximum(jnp.sqrt(jnp.sum(kk * kk, axis=-1, keepdims=True)), 1e-12)
    kk = kk.reshape(bsz, l, D_RWKV)
    k_dir = k[:, :, None, :] * (1.0 + (a - 1.0) * p['k_a'])

    def dirs(t):
        t = jnp.stack([t[:, :, 0], jnp.flip(t[:, :, 1], axis=1)], axis=0)
        t = t.reshape(N_DIR, bsz, l, N_HEADS, HEAD_DIM).transpose(2, 0, 1, 3, 4)
        return t.astype(F32)

    def both(t):
        return dirs(jnp.stack([t, t], axis=2))

    kk_s = both(kk)
    s_fin, y = lax.scan(rwkv7_step, s0, (both(r), dirs(decay), dirs(k_dir), both(v), -kk_s, kk_s * dirs(a)))
    if not with_output:
        return None, s_fin
    y = y.transpose(1, 2, 0, 3, 4)
    y = y[0] + jnp.flip(y[1], axis=1)
    mean = jnp.mean(y, axis=-1, keepdims=True)
    var = jnp.mean(jnp.square(y - mean), axis=-1, keepdims=True)
    y = ((y - mean) * lax.rsqrt(var + LN_X_EPS)).reshape(bsz, l, D_RWKV) * p['lnx_g'] + p['lnx_b']
    rh = heads(r.astype(F32))
    kh = heads(jnp.mean(k_dir, axis=2).astype(F32))
    vh = heads(v.astype(F32))
    bonus = (jnp.sum(rh * kh * p['r_k'], axis=-1, keepdims=True) * vh).reshape(bsz, l, D_RWKV)
    return (y + bonus) * jax.nn.silu(g_r), s_fin


def s5_discretise(p):
    dt = jnp.exp(p['log_dt'].astype(F32))[..., None]
    lr = p['lam_re'].astype(F32)
    li = p['lam_im'].astype(F32)
    mag = jnp.exp(dt * lr)
    ar = mag * jnp.cos(dt * li)
    ai = mag * jnp.sin(dt * li)
    den = lr * lr + li * li
    fr = ((ar - 1.0) * lr + ai * li) / den
    fi = (ai * lr - (ar - 1.0) * li) / den
    b_re = p['b_re'].astype(F32)
    b_im = p['b_im'].astype(F32)
    bbr = fr[..., None] * b_re - fi[..., None] * b_im
    bbi = fr[..., None] * b_im + fi[..., None] * b_re
    return ar, ai, bbr, bbi


def s5_combine(e1, e2):
    a1r, a1i, b1r, b1i = e1
    a2r, a2i, b2r, b2i = e2
    return (a1r * a2r - a1i * a2i, a1r * a2i + a1i * a2r,
            a2r * b1r - a2i * b1i + b2r, a2r * b1i + a2i * b1r + b2i)


def s5_scan(ut, ar, ai, bbr, bbi, h0r, h0i):
    br = jnp.einsum('gpc,lbgc->lbgp', bbr, ut)
    bi = jnp.einsum('gpc,lbgc->lbgp', bbi, ut)
    br = br.at[0].add(ar * h0r - ai * h0i)
    bi = bi.at[0].add(ar * h0i + ai * h0r)
    l = ut.shape[0]
    a_re = jnp.broadcast_to(ar, (l, 1) + ar.shape)
    a_im = jnp.broadcast_to(ai, (l, 1) + ai.shape)
    _, _, hr, hi = lax.associative_scan(s5_combine, (a_re, a_im, br, bi), axis=0)
    return hr, hi


def s5_branch(u, g_s, p, h0_re, h0_im, with_output):
    bsz, l, _ = u.shape
    uf = u.astype(F32)
    ut = uf.reshape(bsz, l, N_GROUPS, S5_GROUP).transpose(1, 0, 2, 3)
    ar, ai, bbr, bbi = s5_discretise(p)
    y_sum = None
    fin_re = []
    fin_im = []
    for d in range(N_DIR):
        ud = ut if d == 0 else jnp.flip(ut, axis=0)
        hr, hi = s5_scan(ud, ar[d], ai[d], bbr[d], bbi[d], h0_re[d], h0_im[d])
        fin_re.append(hr[-1])
        fin_im.append(hi[-1])
        if with_output:
            yd = jnp.einsum('gcp,lbgp->lbgc', p['c_re'], hr) - jnp.einsum('gcp,lbgp->lbgc', p['c_im'], hi)
            yd = yd if d == 0 else jnp.flip(yd, axis=0)
            y_sum = yd if y_sum is None else y_sum + yd
    h_fin_re = jnp.stack(fin_re, axis=0)
    h_fin_im = jnp.stack(fin_im, axis=0)
    if not with_output:
        return None, h_fin_re, h_fin_im
    y = y_sum.transpose(1, 0, 2, 3).reshape(bsz, l, D_S5) + p['d'] * uf
    y = jax.nn.gelu(y)
    gl = y @ p['w_glu']
    y = gl[..., :D_S5] * jax.nn.sigmoid(gl[..., D_S5:])
    return y * jax.nn.silu(g_s), h_fin_re, h_fin_im


def mixer(h, shift_fn, p, s0, h0_re, h0_im, with_output):
    z = h @ p['w_in']
    zs, g_r, u, g_s = jnp.split(z, [SHIFT_W, SHIFT_W + D_RWKV, SHIFT_W + D_RWKV + D_S5], axis=-1)
    zs = zs + p['mu'] * (shift_fn(zs) - zs)
    y_r, s_fin = rwkv7_branch(zs, g_r, p, s0, with_output)
    y_s, h_fin_re, h_fin_im = s5_branch(u, g_s, p, h0_re, h0_im, with_output)
    if not with_output:
        return None, s_fin, h_fin_re, h_fin_im
    out = jnp.concatenate([y_r, y_s], axis=-1) @ p['w_out']
    return out, s_fin, h_fin_re, h_fin_im


def setup_inputs(seed: int = 0) -> dict:
    key = jax.random.key(seed)
    ks = jax.random.split(key, 32)

    def nrm(k, shape, scale):
        return jax.random.normal(k, shape, F32) * scale

    ramp = (jnp.arange(D_RWKV, dtype=F32) / (D_RWKV - 1)) ** 0.85
    return {
        'x': nrm(ks[0], (BATCH, SEQ, D_MODEL), 1.0),
        'c': nrm(ks[1], (BATCH, D_MODEL), 1.0),
        'ctx': nrm(ks[2], (BATCH, CTX_LEN, D_MODEL), 1.0),
        'c_ctx': nrm(ks[3], (D_MODEL,), 1.0),
        'norm_g': 1.0 + nrm(ks[4], (DEPTH, D_MODEL), 0.02),
        'w_ada': nrm(ks[5], (DEPTH, D_MODEL, 3 * D_MODEL), 0.5 * D_MODEL ** -0.5),
        'b_ada': nrm(ks[6], (DEPTH, 3 * D_MODEL), 0.02),
        'w_in': nrm(ks[7], (DEPTH, D_MODEL, D_IN), D_MODEL ** -0.5),
        'mu_shift': jax.random.uniform(ks[8], (DEPTH, SHIFT_W), F32, 0.2, 0.8),
        'rwkv_w0': -6.0 + 5.0 * ramp + nrm(ks[9], (DEPTH, N_DIR, D_RWKV), 0.1),
        'rwkv_w2': nrm(ks[10], (DEPTH, N_DIR, LORA_W, D_RWKV), 0.5 * LORA_W ** -0.5),
        'rwkv_a0': nrm(ks[11], (DEPTH, N_DIR, D_RWKV), 0.1),
        'rwkv_a2': nrm(ks[12], (DEPTH, N_DIR, LORA_A, D_RWKV), 0.5 * LORA_A ** -0.5),
        'rwkv_k_k': 0.85 + nrm(ks[13], (DEPTH, D_RWKV), 0.02),
        'rwkv_k_a': 1.0 + nrm(ks[14], (DEPTH, D_RWKV), 0.02),
        'rwkv_r_k': nrm(ks[15], (DEPTH, N_HEADS, HEAD_DIM), 0.1),
        'lnx_g': 1.0 + nrm(ks[16], (DEPTH, D_RWKV), 0.02),
        'lnx_b': nrm(ks[17], (DEPTH, D_RWKV), 0.02),
        's5_lam_re': -0.5 + nrm(ks[18], (DEPTH, N_DIR, N_GROUPS, S5_STATE), 0.01),
        's5_lam_im': math.pi * jnp.arange(S5_STATE, dtype=F32) + nrm(ks[19], (DEPTH, N_DIR, N_GROUPS, S5_STATE), 0.01),
        's5_log_dt': jax.random.uniform(ks[20], (DEPTH, N_DIR, N_GROUPS), F32, math.log(0.001), math.log(0.1)),
        's5_b_re': nrm(ks[21], (DEPTH, N_GROUPS, S5_STATE, S5_GROUP), (2 * S5_GROUP) ** -0.5),
        's5_b_im': nrm(ks[22], (DEPTH, N_GROUPS, S5_STATE, S5_GROUP), (2 * S5_GROUP) ** -0.5),
        's5_c_re': nrm(ks[23], (DEPTH, N_GROUPS, S5_GROUP, S5_STATE), (2 * S5_STATE) ** -0.5),
        's5_c_im': nrm(ks[24], (DEPTH, N_GROUPS, S5_GROUP, S5_STATE), (2 * S5_STATE) ** -0.5),
        's5_d': nrm(ks[25], (DEPTH, D_S5), 0.5),
        's5_w_glu': nrm(ks[26], (DEPTH, D_S5, 2 * D_S5), D_S5 ** -0.5),
        'w_out': nrm(ks[27], (DEPTH, D_MIX, D_MODEL), D_MIX ** -0.5),
        'final_g': 1.0 + nrm(ks[28], (D_MODEL,), 0.02),
    }


def reference(x, c, ctx, c_ctx, norm_g, w_ada, b_ada, w_in, mu_shift, rwkv_w0, rwkv_w2, rwkv_a0,
              rwkv_a2, rwkv_k_k, rwkv_k_a, rwkv_r_k, lnx_g, lnx_b, s5_lam_re, s5_lam_im, s5_log_dt,
              s5_b_re, s5_b_im, s5_c_re, s5_c_im, s5_d, s5_w_glu, w_out, final_g):
    out_dtype = x.dtype
    bsz = x.shape[0]
    rows = x.shape[1] // GRID_W
    latent_shift = functools.partial(q_shift_grid, rows=rows)
    for i in range(DEPTH):
        p = {'w_in': w_in[i], 'mu': mu_shift[i], 'w0': rwkv_w0[i], 'w2': rwkv_w2[i],
             'a0': rwkv_a0[i], 'a2': rwkv_a2[i], 'k_k': rwkv_k_k[i], 'k_a': rwkv_k_a[i],
             'r_k': rwkv_r_k[i], 'lnx_g': lnx_g[i], 'lnx_b': lnx_b[i],
             'lam_re': s5_lam_re[i], 'lam_im': s5_lam_im[i], 'log_dt': s5_log_dt[i],
             'b_re': s5_b_re[i], 'b_im': s5_b_im[i], 'c_re': s5_c_re[i], 'c_im': s5_c_im[i],
             'd': s5_d[i], 'w_glu': s5_w_glu[i], 'w_out': w_out[i]}
        last = i == DEPTH - 1
        shift, scale, gate = (t[:, None, :] for t in modulation(c, w_ada[i], b_ada[i]))
        cshift, cscale, cgate = modulation(c_ctx, w_ada[i], b_ada[i])
        hc = rmsnorm(ctx, norm_g[i]) * (1.0 + cscale) + cshift
        s0 = jnp.zeros((N_DIR, bsz, N_HEADS, HEAD_DIM, HEAD_DIM), F32)
        h0 = jnp.zeros((N_DIR, bsz, N_GROUPS, S5_STATE), F32)
        out_c, s_ctx, h_ctx_re, h_ctx_im = mixer(hc, shift_1d, p, s0, h0, h0, not last)
        hx = rmsnorm(x, norm_g[i]) * (1.0 + scale) + shift
        out_x, _, _, _ = mixer(hx, latent_shift, p, s_ctx, h_ctx_re, h_ctx_im, True)
        x = x + gate * out_x
        if not last:
            ctx = ctx + cgate * out_c
    return rmsnorm(x, final_g).astype(out_dtype)
```

```python
import functools
import math

import jax
import jax.numpy as jnp
from jax import lax
from jax.experimental import pallas as pl
from jax.experimental.pallas import tpu as pltpu

F32 = jnp.float32
BF16 = jnp.bfloat16

GRID_W = 64
HEAD = 64
LORA = 64
S5_GROUP = 16
S5_STATE = 64
S5_CHUNK = 16
EPS = 1e-6
LN_X_EPS = 64e-5
EXP_M05 = math.exp(-0.5)

LANES = 128
VMEM_LIMIT = 56 * 1024 * 1024


def _params(*sem):
    return pltpu.CompilerParams(dimension_semantics=sem, vmem_limit_bytes=VMEM_LIMIT)


def _sigmoid(x):
    return 1.0 / (1.0 + jnp.exp(-x))


def _dot_bf16(a, b):
    return jnp.dot(a.astype(BF16), b.astype(BF16), preferred_element_type=F32)


def _dot_f32(a, b):
    return jnp.dot(a, b, preferred_element_type=F32, precision=lax.Precision.HIGHEST)


def _segsum(x, e_ref):
    hi = x.astype(BF16)
    lo = (x - hi.astype(F32)).astype(BF16)
    e = e_ref[...]
    return (jnp.dot(hi, e, preferred_element_type=F32) + jnp.dot(lo, e, preferred_element_type=F32))


def _mod_kernel(c_ref, w_ref, b_ref, o_ref):
    c = c_ref[...]
    o_ref[...] = _dot_f32(c * _sigmoid(c), w_ref[...]) + b_ref[...]


def _modulation(cond, w_ada, b_ada):
    rows, d = cond.shape
    n = w_ada.shape[1]
    tn = 512
    return pl.pallas_call(
        _mod_kernel,
        grid=(n // tn,),
        in_specs=[pl.BlockSpec((rows, d), lambda j: (0, 0)),
                  pl.BlockSpec((d, tn), lambda j: (0, j)),
                  pl.BlockSpec((1, tn), lambda j: (0, j))],
        out_specs=pl.BlockSpec((rows, tn), lambda j: (0, j)),
        out_shape=jax.ShapeDtypeStruct((rows, n), F32),
        compiler_params=_params("arbitrary"),
        name="modulation",
    )(cond, w_ada, b_ada.reshape(1, n))


def _inproj_kernel(x_ref, g_ref, sc_ref, sh_ref, w1_ref, w2_ref, z1_ref, z2_ref):
    x = x_ref[0]
    ms = jnp.mean(x * x, axis=-1, keepdims=True)
    h = x * lax.rsqrt(ms + EPS) * g_ref[...]
    h = (h * (1.0 + sc_ref[0]) + sh_ref[0]).astype(BF16)
    z1_ref[0] = jnp.dot(h, w1_ref[...], preferred_element_type=F32)
    z2_ref[0] = jnp.dot(h, w2_ref[...], preferred_element_type=F32)


def _inproj(x, norm_g, scale, shift, w1, w2):
    b, l, d = x.shape
    n1, n2 = w1.shape[1], w2.shape[1]
    tm = min(l, 512)
    return pl.pallas_call(
        _inproj_kernel,
        grid=(b, l // tm),
        in_specs=[pl.BlockSpec((1, tm, d), lambda i, j: (i, j, 0)),
                  pl.BlockSpec((1, d), lambda i, j: (0, 0)),
                  pl.BlockSpec((1, 1, d), lambda i, j: (i, 0, 0)),
                  pl.BlockSpec((1, 1, d), lambda i, j: (i, 0, 0)),
                  pl.BlockSpec((d, n1), lambda i, j: (0, 0)),
                  pl.BlockSpec((d, n2), lambda i, j: (0, 0))],
        out_specs=[pl.BlockSpec((1, tm, n1), lambda i, j: (i, j, 0)),
                   pl.BlockSpec((1, tm, n2), lambda i, j: (i, j, 0))],
        out_shape=[jax.ShapeDtypeStruct((b, l, n1), F32),
                   jax.ShapeDtypeStruct((b, l, n2), F32)],
        compiler_params=_params("arbitrary", "arbitrary"),
        name="inproj",
    )(x, norm_g.reshape(1, d), scale, shift, w1, w2)


def _prep_body(z, up, down, mu_ref, w0_ref, w2_ref, a0_ref, a2_ref, kk_ref, ka_ref, rk_ref, e_ref, outs,
               grid2d):
    t, sw = z.shape
    dr = (sw - 4 * LORA) // 3
    row = lax.broadcasted_iota(jnp.int32, (t, 1), 0)
    slot = lax.broadcasted_iota(jnp.int32, (1, sw), 1) & 3
    prev = pltpu.roll(z, 1, 0)
    nxt = pltpu.roll(z, t - 1, 0)
    if grid2d:
        col = row & (GRID_W - 1)
        prev = jnp.where(col == 0, 0.0, prev)
        nxt = jnp.where(col == GRID_W - 1, 0.0, nxt)
        sh = jnp.where(slot == 0, prev, jnp.where(slot == 1, nxt, jnp.where(slot == 2, up, down)))
    else:
        prev = jnp.where(row == 0, 0.0, prev)
        nxt = jnp.where(row == t - 1, 0.0, nxt)
        sh = jnp.where((slot & 1) == 0, prev, nxt)
    zs = z + mu_ref[...] * (sh - z)

    r = zs[:, 0:dr]
    k = zs[:, dr:2 * dr]
    v = zs[:, 2 * dr:3 * dr]
    kk = k * kk_ref[...]
    ss = _segsum(kk * kk, e_ref)
    kk = kk / jnp.maximum(jnp.sqrt(ss), 1e-12)
    r_o, v_o, aa_o, w_o, kd_o, bb_o, bonus_o = outs
    r_o[0] = r
    v_o[0] = v
    aa_o[0] = -kk
    ksum = None
    for d in range(2):
        zw = zs[:, 3 * dr + LORA * d:3 * dr + LORA * (d + 1)]
        za = zs[:, 3 * dr + 2 * LORA + LORA * d:3 * dr + 2 * LORA + LORA * (d + 1)]
        wl = w0_ref[d:d + 1, :] + _dot_f32(jnp.tanh(zw), w2_ref[d])
        w_o[d, 0] = jnp.exp(-EXP_M05 * _sigmoid(wl))
        asig = _sigmoid(a0_ref[d:d + 1, :] + _dot_f32(za, a2_ref[d]))
        kd = k * (1.0 + (asig - 1.0) * ka_ref[...])
        kd_o[d, 0] = kd
        bb_o[d, 0] = kk * asig
        ksum = kd if ksum is None else ksum + kd
    bonus_o[0] = _segsum(r * (0.5 * ksum) * rk_ref[...], e_ref) * v


def _prep2d_kernel(zc_ref, zu_ref, zd_ref, *rest):
    params, outs = rest[:9], rest[9:]
    j = pl.program_id(1)
    nj = pl.num_programs(1)
    z = zc_ref[0]
    t = z.shape[0]
    row = lax.broadcasted_iota(jnp.int32, (t, 1), 0)
    up = jnp.concatenate([zu_ref[0], z[:t - GRID_W]], axis=0)
    up = jnp.where(jnp.logical_and(j == 0, row < GRID_W), 0.0, up)
    down = jnp.concatenate([z[GRID_W:], zd_ref[0]], axis=0)
    down = jnp.where(jnp.logical_and(j == nj - 1, row >= t - GRID_W), 0.0, down)
    _prep_body(z, up, down, *params, outs, grid2d=True)


def _prep1d_kernel(zc_ref, *rest):
    params, outs = rest[:9], rest[9:]
    _prep_body(zc_ref[0], None, None, *params, outs, grid2d=False)


def _prep(z1, pp, grid2d):
    b, l, sw = z1.shape
    dr = (sw - 4 * LORA) // 3
    tt = 256 if grid2d else l
    nj = l // tt
    rb = tt // GRID_W
    const = lambda *shape: pl.BlockSpec(shape, lambda i, j: (0,) * len(shape))
    p_specs = [const(1, sw), const(2, dr), const(2, LORA, dr), const(2, dr), const(2, LORA, dr),
               const(1, dr), const(1, dr), const(1, dr), const(dr, dr)]
    cur = pl.BlockSpec((1, tt, sw), lambda i, j: (i, j, 0))
    if grid2d:
        nrow = l // GRID_W
        in_specs = [cur,
                    pl.BlockSpec((1, GRID_W, sw), lambda i, j: (i, jnp.maximum(j * rb - 1, 0), 0)),
                    pl.BlockSpec((1, GRID_W, sw), lambda i, j: (i, jnp.minimum((j + 1) * rb, nrow - 1), 0))]
        args = (z1, z1, z1)
        body = _prep2d_kernel
    else:
        in_specs = [cur]
        args = (z1,)
        body = _prep1d_kernel
    o1 = pl.BlockSpec((1, tt, dr), lambda i, j: (i, j, 0))
    o2 = pl.BlockSpec((2, 1, tt, dr), lambda i, j: (0, i, j, 0))
    s1 = jax.ShapeDtypeStruct((b, l, dr), F32)
    s2 = jax.ShapeDtypeStruct((2, b, l, dr), F32)
    return pl.pallas_call(
        body,
        grid=(b, nj),
        in_specs=in_specs + p_specs,
        out_specs=[o1, o1, o1, o2, o2, o2, o1],
        out_shape=[s1, s1, s1, s2, s2, s2, s1],
        compiler_params=_params("arbitrary", "arbitrary"),
        name="prep2d" if grid2d else "prep1d",
    )(*args, *pp)


def _scan_kernel(w_ref, kd_ref, bb_ref, aa_ref, r_ref, v_ref, s0_ref, y_ref, sout_ref, s_scr):
    tb, n, _ = w_ref.shape

    @pl.when(pl.program_id(0) == 0)
    def _():
        s_scr[...] = s0_ref[...]

    def step(t, carry):
        sa = jnp.zeros((n, LANES), F32)
        for k in range(n):
            sa = sa + s_scr[k] * aa_ref[t, k:k + 1, :]
        vv = v_ref[t]
        y = jnp.zeros((n, LANES), F32)
        for k in range(n):
            new = s_scr[k] * w_ref[t, k:k + 1, :] + sa * bb_ref[t, k:k + 1, :] + vv * kd_ref[t, k:k + 1, :]
            s_scr[k] = new
            y = y + new * r_ref[t, k:k + 1, :]
        y_ref[t] = y
        return carry

    lax.fori_loop(0, tb, step, 0)

    @pl.when(pl.program_id(0) == pl.num_programs(0) - 1)
    def _():
        sout_ref[...] = s_scr[...]


def _rwkv_scan(w, kd, bb, aa, r, v, s0):
    l, n, c = w.shape
    tb = 32
    blk = pl.BlockSpec((tb, n, c), lambda i: (i, 0, 0))
    st = pl.BlockSpec((n, n, c), lambda i: (0, 0, 0))
    return pl.pallas_call(
        _scan_kernel,
        grid=(l // tb,),
        in_specs=[blk] * 6 + [st],
        out_specs=[blk, st],
        out_shape=[jax.ShapeDtypeStruct((l, n, c), F32), jax.ShapeDtypeStruct((n, n, c), F32)],
        scratch_shapes=[pltpu.VMEM((n, n, c), F32)],
        compiler_params=_params("arbitrary"),
        name="rwkv_scan",
    )(w, kd, bb, aa, r, v, s0)


def _to_chains(xf, xb):
    b, l, c = xf.shape
    h = c // HEAD
    f = xf.reshape(b, l, h, HEAD).transpose(1, 3, 0, 2)
    g = jnp.flip(xb, axis=1).reshape(b, l, h, HEAD).transpose(1, 3, 0, 2)
    return jnp.concatenate([f, g], axis=2).reshape(l, HEAD, 2 * b * h)


def _from_chains(y, b):
    l, n, c = y.shape
    h = c // (2 * b)
    y = y.reshape(l, n, 2, b, h)
    yf = y[:, :, 0].transpose(2, 0, 3, 1)
    yb = jnp.flip(y[:, :, 1], axis=0).transpose(2, 0, 3, 1)
    return (yf + yb).reshape(b, l, h * n)


def _cmul_add(h, acat, bcat, x):
    p = h.shape[-1] // 2
    return h * acat + pltpu.roll(h, p, 1) * bcat + x


def _s5_kernel(uc_ref, u_ref, w1_ref, cm_ref, a_ref, y_ref, r_scr, h_scr):
    sp = 2 * S5_STATE
    tw = 2 * S5_CHUNK * S5_GROUP
    w1 = w1_ref[0]
    af, bf, ab, bbk = a_ref[0, 0:1, :], a_ref[0, 1:2, :], a_ref[0, 2:3, :], a_ref[0, 3:4, :]
    nb = 8
    hin = jnp.dot(uc_ref[0], w1[:, tw:], preferred_element_type=F32)
    ncc = hin.shape[0] // nb
    hf = jnp.zeros((nb, sp), F32)
    hb = jnp.zeros((nb, sp), F32)
    for c in range(ncc):
        hf = _cmul_add(hf, af, bf, hin[c * nb:(c + 1) * nb, 0:sp])
        cb = ncc - 1 - c
        hb = _cmul_add(hb, ab, bbk, hin[cb * nb:(cb + 1) * nb, sp:2 * sp])
    r_scr[...] = jnp.dot(u_ref[0], w1, preferred_element_type=F32)
    nc = r_scr.shape[0] // nb

    def fwd(c, h):
        rows = pl.ds(pl.multiple_of(c * nb, nb), nb)
        h_scr[rows, 0:sp] = h
        return _cmul_add(h, af, bf, r_scr[rows, tw:tw + sp])

    def bwd(i, h):
        c = nc - 1 - i
        rows = pl.ds(pl.multiple_of(c * nb, nb), nb)
        h_scr[rows, sp:2 * sp] = h
        return _cmul_add(h, ab, bbk, r_scr[rows, tw + sp:tw + 2 * sp])

    lax.fori_loop(0, nc, fwd, hf)
    lax.fori_loop(0, nc, bwd, hb)
    half = tw // 2
    y_ref[0] = (r_scr[:, 0:half] + r_scr[:, half:tw]
                + jnp.dot(h_scr[...].astype(BF16), cm_ref[0], preferred_element_type=F32))


def _s5_scan(u3c, u3, w1, cm, acat):
    g, rows, kw = u3.shape
    rows_c = u3c.shape[1]
    nw = w1.shape[2]
    return pl.pallas_call(
        _s5_kernel,
        grid=(g,),
        in_specs=[pl.BlockSpec((1, rows_c, kw), lambda i: (i, 0, 0)),
                  pl.BlockSpec((1, rows, kw), lambda i: (i, 0, 0)),
                  pl.BlockSpec((1, kw, nw), lambda i: (i, 0, 0)),
                  pl.BlockSpec((1, 4 * S5_STATE, kw), lambda i: (i, 0, 0)),
                  pl.BlockSpec((1, 4, 2 * S5_STATE), lambda i: (i, 0, 0))],
        out_specs=pl.BlockSpec((1, rows, kw), lambda i: (i, 0, 0)),
        out_shape=jax.ShapeDtypeStruct((g, rows, kw), F32),
        scratch_shapes=[pltpu.VMEM((rows, nw), F32), pltpu.VMEM((rows, 4 * S5_STATE), F32)],
        compiler_params=_params("arbitrary"),
        name="s5_scan",
    )(u3c, u3, w1, cm, acat)


def _s5_matrices(lam_re, lam_im, log_dt, b_re, b_im, c_re, c_im):
    hp = lax.Precision.HIGHEST
    t = S5_CHUNK
    dt = jnp.exp(log_dt.astype(F32))[..., None]
    lr, li = lam_re.astype(F32), lam_im.astype(F32)
    j = jnp.arange(t + 1, dtype=F32)[:, None, None, None]
    mag = jnp.exp(j * dt * lr)
    pr, pi = mag * jnp.cos(j * dt * li), mag * jnp.sin(j * dt * li)
    ar, ai = pr[1], pi[1]
    den = lr * lr + li * li
    fr = ((ar - 1.0) * lr + ai * li) / den
    fi = (ai * lr - (ar - 1.0) * li) / den
    bbr = fr[..., None] * b_re - fi[..., None] * b_im
    bbi = fr[..., None] * b_im + fi[..., None] * b_re
    car = c_re[None, None] * pr[:, :, :, None, :] - c_im[None, None] * pi[:, :, :, None, :]
    cai = c_re[None, None] * pi[:, :, :, None, :] + c_im[None, None] * pr[:, :, :, None, :]
    kj = (jnp.einsum('jdgcp,dgpe->jdgce', car, bbr, precision=hp)
          - jnp.einsum('jdgcp,dgpe->jdgce', cai, bbi, precision=hp))
    sig = jnp.arange(t)[:, None]
    tau = jnp.arange(t)[None, :]
    g = lr.shape[1]

    def toeplitz(d, lag, ok):
        m = kj[jnp.clip(lag, 0, t), d]
        m = jnp.where(ok[:, :, None, None, None], m, 0.0)
        return m.transpose(2, 0, 4, 1, 3).reshape(g, t * S5_GROUP, t * S5_GROUP)

    m_f = toeplitz(0, tau - sig, tau >= sig)
    m_b = toeplitz(1, sig - tau, sig >= tau)

    def state_in(d, pw):
        qr = pr[pw, d][..., None] * bbr[d][None] - pi[pw, d][..., None] * bbi[d][None]
        qi = pr[pw, d][..., None] * bbi[d][None] + pi[pw, d][..., None] * bbr[d][None]
        q = jnp.stack([qr, qi], axis=2)
        return q.transpose(1, 0, 4, 2, 3).reshape(g, t * S5_GROUP, 2 * S5_STATE)

    b_f = state_in(0, t - 1 - jnp.arange(t))
    b_b = state_in(1, jnp.arange(t))

    def state_out(d, pw):
        q = jnp.stack([car[pw, d], -cai[pw, d]], axis=2)
        return q.transpose(1, 2, 4, 0, 3).reshape(g, 2 * S5_STATE, t * S5_GROUP)

    c_f = state_out(0, jnp.arange(t) + 1)
    c_b = state_out(1, t - jnp.arange(t))
    w1 = jnp.concatenate([m_f, m_b, b_f, b_b], axis=2).astype(BF16)
    cm = jnp.concatenate([c_f, c_b], axis=1).astype(BF16)
    at, ait = pr[t], pi[t]
    acat = jnp.stack([jnp.concatenate([at[0], at[0]], -1), jnp.concatenate([-ait[0], ait[0]], -1),
                      jnp.concatenate([at[1], at[1]], -1), jnp.concatenate([-ait[1], ait[1]], -1)], axis=1)
    return w1, cm, acat


def _to_s5_rows(u):
    b, l, c = u.shape
    g = c // S5_GROUP
    u = u.reshape(b, l // S5_CHUNK, S5_CHUNK, g, S5_GROUP).transpose(3, 1, 0, 2, 4)
    return u.reshape(g, (l // S5_CHUNK) * b, S5_CHUNK * S5_GROUP)


def _from_s5_rows(y, b):
    g, rows, kw = y.shape
    nc = rows // b
    y = y.reshape(g, nc, b, S5_CHUNK, S5_GROUP).transpose(2, 1, 3, 0, 4)
    return y.reshape(b, nc * S5_CHUNK, g * S5_GROUP)


def _out_kernel(x_ref, gate_ref, yr_ref, bonus_ref, gr_ref, u_ref, gs_ref, ys_ref, lg_ref, lb_ref, d_ref,
                wg_ref, wo_ref, fg_ref, e_ref, o_ref):
    dr = yr_ref.shape[2]
    y = yr_ref[0]
    mean = _segsum(y, e_ref) * (1.0 / HEAD)
    dev = y - mean
    var = _segsum(dev * dev, e_ref) * (1.0 / HEAD)
    yn = dev * lax.rsqrt(var + LN_X_EPS) * lg_ref[...] + lb_ref[...]
    gr = gr_ref[0]
    y_r = (yn + bonus_ref[0]) * (gr * _sigmoid(gr))

    s = ys_ref[0] + d_ref[...] * u_ref[0]
    s = 0.5 * s * (1.0 + jnp.tanh(math.sqrt(2.0 / math.pi) * (s + 0.044715 * (s * s * s))))
    gl = jnp.dot(s.astype(BF16), wg_ref[...], preferred_element_type=F32)
    ds = gl.shape[1] // 2
    gs = gs_ref[0]
    y_s = gl[:, :ds] * _sigmoid(gl[:, ds:]) * (gs * _sigmoid(gs))

    out = (jnp.dot(y_r.astype(BF16), wo_ref[0:dr, :], preferred_element_type=F32)
           + jnp.dot(y_s.astype(BF16), wo_ref[dr:, :], preferred_element_type=F32))
    xo = x_ref[0] + gate_ref[0] * out
    ms = jnp.mean(xo * xo, axis=-1, keepdims=True)
    o_ref[0] = (xo * lax.rsqrt(ms + EPS) * fg_ref[...]).astype(o_ref.dtype)


def _out_stage(x, gate, yr, bonus, z2, ys, lnx_g, lnx_b, s5_d, w_glu, w_out, final_g, e):
    b, l, d = x.shape
    dr = yr.shape[2]
    ds = ys.shape[2]
    tm = 512
    tok = lambda w, cb: pl.BlockSpec((1, tm, w), lambda i, j: (i, j, cb))
    const = lambda *shape: pl.BlockSpec(shape, lambda i, j: (0,) * len(shape))
    return pl.pallas_call(
        _out_kernel,
        grid=(b, l // tm),
        in_specs=[tok(d, 0), pl.BlockSpec((1, 1, d), lambda i, j: (i, 0, 0)),
                  tok(dr, 0), tok(dr, 0), tok(dr, 0), tok(ds, 1), tok(ds, 2), tok(ds, 0),
                  const(1, dr), const(1, dr), const(1, ds), const(ds, 2 * ds), const(dr + ds, d),
                  const(1, d), const(dr, dr)],
        out_specs=tok(d, 0),
        out_shape=jax.ShapeDtypeStruct((b, l, d), x.dtype),
        compiler_params=_params("arbitrary", "arbitrary"),
        name="out_stage",
    )(x, gate, yr, bonus, z2, z2, z2, ys, lnx_g.reshape(1, dr), lnx_b.reshape(1, dr), s5_d.reshape(1, ds),
      w_glu.astype(BF16), w_out.astype(BF16), final_g.reshape(1, d), e)


def kernel(x, c, ctx, c_ctx, norm_g, w_ada, b_ada, w_in, mu_shift, rwkv_w0, rwkv_w2, rwkv_a0, rwkv_a2, rwkv_k_k, rwkv_k_a, rwkv_r_k, lnx_g, lnx_b, s5_lam_re, s5_lam_im, s5_log_dt, s5_b_re, s5_b_im, s5_c_re, s5_c_im, s5_d, s5_w_glu, w_out, final_g):
    assert norm_g.shape[0] == 1, "one layer"
    b, l, d = x.shape
    lc = ctx.shape[1]
    dr = rwkv_k_k.shape[1]
    ds = s5_d.shape[1]
    sw = mu_shift.shape[1]
    assert sw == 3 * dr + 4 * LORA and l % (4 * GRID_W) == 0 and lc % S5_CHUNK == 0
    assert 2 * b * (dr // HEAD) == LANES, "the recurrence kernel maps (direction, batch, head) onto the lanes"

    cond = jnp.zeros((2 * b, d), F32).at[:b].set(c).at[b].set(c_ctx)
    m = _modulation(cond, w_ada[0], b_ada[0])
    shift, scale, gate = m[:b, :d], m[:b, d:2 * d], m[:b, 2 * d:]
    cshift = jnp.broadcast_to(m[b, :d], (b, d))
    cscale = jnp.broadcast_to(m[b, d:2 * d], (b, d))

    w1 = w_in[0][:, :sw].astype(BF16)
    w2 = w_in[0][:, sw:].astype(BF16)
    z1c, z2c = _inproj(ctx, norm_g[0], cscale[:, None], cshift[:, None], w1, w2)
    z1, z2 = _inproj(x, norm_g[0], scale[:, None], shift[:, None], w1, w2)

    hid = jnp.arange(dr) // HEAD
    e = (hid[:, None] == hid[None, :]).astype(BF16)
    pp = (mu_shift, rwkv_w0[0], rwkv_w2[0], rwkv_a0[0], rwkv_a2[0], rwkv_k_k, rwkv_k_a,
          rwkv_r_k[0].reshape(1, dr), e)
    rc, vc, aac, wc, kdc, bbc, _ = _prep(z1c, pp, grid2d=False)
    rx, vx, aax, wx, kdx, bbx, bonus = _prep(z1, pp, grid2d=True)

    s0 = jnp.zeros((HEAD, HEAD, LANES), F32)
    _, s_ctx = _rwkv_scan(_to_chains(wc[0], wc[1]), _to_chains(kdc[0], kdc[1]), _to_chains(bbc[0], bbc[1]),
                          _to_chains(aac, aac), _to_chains(rc, rc), _to_chains(vc, vc), s0)
    y_ch, _ = _rwkv_scan(_to_chains(wx[0], wx[1]), _to_chains(kdx[0], kdx[1]), _to_chains(bbx[0], bbx[1]),
                         _to_chains(aax, aax), _to_chains(rx, rx), _to_chains(vx, vx), s_ctx)
    yr = _from_chains(y_ch, b)

    sm_w1, sm_cm, sm_a = _s5_matrices(s5_lam_re[0], s5_lam_im[0], s5_log_dt[0], s5_b_re[0], s5_b_im[0],
                                      s5_c_re[0], s5_c_im[0])
    u3c = _to_s5_rows(z2c[:, :, dr:dr + ds]).astype(BF16)
    u3 = _to_s5_rows(z2[:, :, dr:dr + ds]).astype(BF16)
    ys = _from_s5_rows(_s5_scan(u3c, u3, sm_w1, sm_cm, sm_a), b)

    return _out_stage(x, gate[:, None], yr, bonus, z2, ys, lnx_g[0], lnx_b[0], s5_d[0], s5_w_glu[0], w_out[0],
                      final_g, e)
```

```python
import functools
import math

import jax
import jax.numpy as jnp
from jax import lax
from jax.experimental import pallas as pl
from jax.experimental.pallas import tpu as pltpu

F32 = jnp.float32
BF16 = jnp.bfloat16

GRID_W = 64
HEAD = 64
LORA = 64
S5_GROUP = 16
S5_STATE = 64
S5_CHUNK = 16
WKV_CHUNK = 64
EPS = 1e-6
LN_X_EPS = 64e-5
EXP_M05 = math.exp(-0.5)

LANES = 128
VMEM_LIMIT = 56 * 1024 * 1024


def _params(*sem):
    return pltpu.CompilerParams(dimension_semantics=sem, vmem_limit_bytes=VMEM_LIMIT)


def _sigmoid(x):
    return 1.0 / (1.0 + jnp.exp(-x))


def _dot_bf16(a, b):
    return jnp.dot(a.astype(BF16), b.astype(BF16), preferred_element_type=F32)


def _dot_f32(a, b):
    return jnp.dot(a, b, preferred_element_type=F32, precision=lax.Precision.HIGHEST)


def _segsum(x, e_ref):
    hi = x.astype(BF16)
    lo = (x - hi.astype(F32)).astype(BF16)
    e = e_ref[...]
    return (jnp.dot(hi, e, preferred_element_type=F32) + jnp.dot(lo, e, preferred_element_type=F32))


def _mod_kernel(c_ref, w_ref, b_ref, o_ref):
    c = c_ref[...]
    o_ref[...] = _dot_f32(c * _sigmoid(c), w_ref[...]) + b_ref[...]


def _modulation(cond, w_ada, b_ada):
    rows, d = cond.shape
    n = w_ada.shape[1]
    tn = 512
    return pl.pallas_call(
        _mod_kernel,
        grid=(n // tn,),
        in_specs=[pl.BlockSpec((rows, d), lambda j: (0, 0)),
                  pl.BlockSpec((d, tn), lambda j: (0, j)),
                  pl.BlockSpec((1, tn), lambda j: (0, j))],
        out_specs=pl.BlockSpec((rows, tn), lambda j: (0, j)),
        out_shape=jax.ShapeDtypeStruct((rows, n), F32),
        compiler_params=_params("arbitrary"),
        name="modulation",
    )(cond, w_ada, b_ada.reshape(1, n))


def _inproj_kernel(x_ref, g_ref, sc_ref, sh_ref, w1_ref, w2_ref, z1_ref, z2_ref):
    x = x_ref[0]
    ms = jnp.mean(x * x, axis=-1, keepdims=True)
    h = x * lax.rsqrt(ms + EPS) * g_ref[...]
    h = (h * (1.0 + sc_ref[0]) + sh_ref[0]).astype(BF16)
    z1_ref[0] = jnp.dot(h, w1_ref[...], preferred_element_type=F32)
    z2_ref[0] = jnp.dot(h, w2_ref[...], preferred_element_type=F32)


def _inproj(x, norm_g, scale, shift, w1, w2):
    b, l, d = x.shape
    n1, n2 = w1.shape[1], w2.shape[1]
    tm = min(l, 512)
    return pl.pallas_call(
        _inproj_kernel,
        grid=(b, l // tm),
        in_specs=[pl.BlockSpec((1, tm, d), lambda i, j: (i, j, 0)),
                  pl.BlockSpec((1, d), lambda i, j: (0, 0)),
                  pl.BlockSpec((1, 1, d), lambda i, j: (i, 0, 0)),
                  pl.BlockSpec((1, 1, d), lambda i, j: (i, 0, 0)),
                  pl.BlockSpec((d, n1), lambda i, j: (0, 0)),
                  pl.BlockSpec((d, n2), lambda i, j: (0, 0))],
        out_specs=[pl.BlockSpec((1, tm, n1), lambda i, j: (i, j, 0)),
                   pl.BlockSpec((1, tm, n2), lambda i, j: (i, j, 0))],
        out_shape=[jax.ShapeDtypeStruct((b, l, n1), F32),
                   jax.ShapeDtypeStruct((b, l, n2), F32)],
        compiler_params=_params("arbitrary", "arbitrary"),
        name="inproj",
    )(x, norm_g.reshape(1, d), scale, shift, w1, w2)


def _prep_body(z, up, down, mu_ref, w0_ref, w2_ref, a0_ref, a2_ref, kk_ref, ka_ref, rk_ref, e_ref, outs,
               grid2d):
    t, sw = z.shape
    dr = (sw - 4 * LORA) // 3
    row = lax.broadcasted_iota(jnp.int32, (t, 1), 0)
    slot = lax.broadcasted_iota(jnp.int32, (1, sw), 1) & 3
    prev = pltpu.roll(z, 1, 0)
    nxt = pltpu.roll(z, t - 1, 0)
    if grid2d:
        col = row & (GRID_W - 1)
        prev = jnp.where(col == 0, 0.0, prev)
        nxt = jnp.where(col == GRID_W - 1, 0.0, nxt)
        sh = jnp.where(slot == 0, prev, jnp.where(slot == 1, nxt, jnp.where(slot == 2, up, down)))
    else:
        prev = jnp.where(row == 0, 0.0, prev)
        nxt = jnp.where(row == t - 1, 0.0, nxt)
        sh = jnp.where((slot & 1) == 0, prev, nxt)
    zs = z + mu_ref[...] * (sh - z)

    r = zs[:, 0:dr]
    k = zs[:, dr:2 * dr]
    v = zs[:, 2 * dr:3 * dr]
    kk = k * kk_ref[...]
    ss = _segsum(kk * kk, e_ref)
    kk = kk / jnp.maximum(jnp.sqrt(ss), 1e-12)
    r_o, v_o, aa_o, w_o, kd_o, bb_o, bonus_o = outs
    r_o[0] = r
    v_o[0] = v
    aa_o[0] = -kk
    ksum = None
    for d in range(2):
        zw = zs[:, 3 * dr + LORA * d:3 * dr + LORA * (d + 1)]
        za = zs[:, 3 * dr + 2 * LORA + LORA * d:3 * dr + 2 * LORA + LORA * (d + 1)]
        wl = w0_ref[d:d + 1, :] + _dot_f32(jnp.tanh(zw), w2_ref[d])
        w_o[d, 0] = -EXP_M05 * _sigmoid(wl)
        asig = _sigmoid(a0_ref[d:d + 1, :] + _dot_f32(za, a2_ref[d]))
        kd = k * (1.0 + (asig - 1.0) * ka_ref[...])
        kd_o[d, 0] = kd
        bb_o[d, 0] = kk * asig
        ksum = kd if ksum is None else ksum + kd
    bonus_o[0] = _segsum(r * (0.5 * ksum) * rk_ref[...], e_ref) * v


def _prep2d_kernel(zc_ref, zu_ref, zd_ref, *rest):
    params, outs = rest[:9], rest[9:]
    j = pl.program_id(1)
    nj = pl.num_programs(1)
    z = zc_ref[0]
    t = z.shape[0]
    row = lax.broadcasted_iota(jnp.int32, (t, 1), 0)
    up = jnp.concatenate([zu_ref[0], z[:t - GRID_W]], axis=0)
    up = jnp.where(jnp.logical_and(j == 0, row < GRID_W), 0.0, up)
    down = jnp.concatenate([z[GRID_W:], zd_ref[0]], axis=0)
    down = jnp.where(jnp.logical_and(j == nj - 1, row >= t - GRID_W), 0.0, down)
    _prep_body(z, up, down, *params, outs, grid2d=True)


def _prep1d_kernel(zc_ref, *rest):
    params, outs = rest[:9], rest[9:]
    _prep_body(zc_ref[0], None, None, *params, outs, grid2d=False)


def _prep(z1, pp, grid2d):
    b, l, sw = z1.shape
    dr = (sw - 4 * LORA) // 3
    tt = 256 if grid2d else l
    nj = l // tt
    rb = tt // GRID_W
    const = lambda *shape: pl.BlockSpec(shape, lambda i, j: (0,) * len(shape))
    p_specs = [const(1, sw), const(2, dr), const(2, LORA, dr), const(2, dr), const(2, LORA, dr),
               const(1, dr), const(1, dr), const(1, dr), const(dr, dr)]
    cur = pl.BlockSpec((1, tt, sw), lambda i, j: (i, j, 0))
    if grid2d:
        nrow = l // GRID_W
        in_specs = [cur,
                    pl.BlockSpec((1, GRID_W, sw), lambda i, j: (i, jnp.maximum(j * rb - 1, 0), 0)),
                    pl.BlockSpec((1, GRID_W, sw), lambda i, j: (i, jnp.minimum((j + 1) * rb, nrow - 1), 0))]
        args = (z1, z1, z1)
        body = _prep2d_kernel
    else:
        in_specs = [cur]
        args = (z1,)
        body = _prep1d_kernel
    o1 = pl.BlockSpec((1, tt, dr), lambda i, j: (i, j, 0))
    o2 = pl.BlockSpec((2, 1, tt, dr), lambda i, j: (0, i, j, 0))
    s1 = jax.ShapeDtypeStruct((b, l, dr), F32)
    s2 = jax.ShapeDtypeStruct((2, b, l, dr), F32)
    return pl.pallas_call(
        body,
        grid=(b, nj),
        in_specs=in_specs + p_specs,
        out_specs=[o1, o1, o1, o2, o2, o2, o1],
        out_shape=[s1, s1, s1, s2, s2, s2, s1],
        compiler_params=_params("arbitrary", "arbitrary"),
        name="prep2d" if grid2d else "prep1d",
    )(*args, *pp)


def _dot_nt(a, b):
    return lax.dot_general(a, b, (((1,), (1,)), ((), ())), preferred_element_type=F32)


def _dot_tn(a, b):
    return lax.dot_general(a, b, (((0,), (0,)), ((), ())), preferred_element_type=F32)


def _wkv_operands(lw_ref, kd_ref, bb_ref, aa_ref, r_ref, v_ref, backward):
    c = aa_ref.shape[1]
    row = lax.broadcasted_iota(jnp.int32, (c, c), 0)
    col = lax.broadcasted_iota(jnp.int32, (c, c), 1)
    strict, incl = (col > row, col >= row) if backward else (col < row, col <= row)
    lw = lw_ref[0, 0]
    lw_hi = lw.astype(BF16)
    lw_lo = (lw - lw_hi.astype(F32)).astype(BF16)
    lc = incl.astype(BF16)
    cum = jnp.dot(lc, lw_hi, preferred_element_type=F32) + jnp.dot(lc, lw_lo, preferred_element_type=F32)
    tot = cum[0:1, :] if backward else cum[c - 1:c, :]
    ones = jnp.ones((c, HEAD), BF16)
    pinv = jnp.exp(-cum)
    pend = jnp.exp(tot - cum)
    bb, kd = bb_ref[0, 0], kd_ref[0, 0]
    rt = r_ref[0] * jnp.exp(cum)
    return dict(
        strict=strict, incl=incl,
        pc=jnp.exp(_dot_tn(lw_hi, ones) + _dot_tn(lw_lo, ones)),
        at=(aa_ref[0] * jnp.exp(cum - lw)).astype(BF16), rt=rt, rt_b=rt.astype(BF16),
        bt=(bb * pinv).astype(BF16), kt=(kd * pinv).astype(BF16),
        be=(bb * pend).astype(BF16), ke=(kd * pend).astype(BF16), v=v_ref[0].astype(BF16))


def _wkv_kernel(lwf_ref, kdf_ref, bbf_ref, aaf_ref, rf_ref, vf_ref, lwb_ref, kdb_ref, bbb_ref, aab_ref, rb_ref,
                vb_ref, s0_ref, yf_ref, yb_ref, sout_ref, s_scr):
    ci = pl.program_id(1)
    c, width = aaf_ref.shape[1], aaf_ref.shape[2]
    nh = width // HEAD

    @pl.when(ci == 0)
    def _():
        s_scr[...] = s0_ref[:, 0]

    ops = (_wkv_operands(lwf_ref, kdf_ref, bbf_ref, aaf_ref, rf_ref, vf_ref, False),
           _wkv_operands(lwb_ref, kdb_ref, bbb_ref, aab_ref, rb_ref, vb_ref, True))
    chains = [(d, h) for h in range(nh) for d in range(2)]
    cut = lambda name: [ops[d][name][:, h * HEAD:(h + 1) * HEAD] for d, h in chains]
    at, rt, rt_b, bt, kt, be, ke, v = (cut(n) for n in ("at", "rt", "rt_b", "bt", "kt", "be", "ke", "v"))
    strict = [ops[d]["strict"] for d, _ in chains]
    incl = [ops[d]["incl"] for d, _ in chains]
    n = range(len(chains))
    mm = lambda a, b: jnp.dot(a, b, preferred_element_type=F32)
    row = lax.broadcasted_iota(jnp.int32, (c, c), 0)
    col = lax.broadcasted_iota(jnp.int32, (c, c), 1)
    eye = (row == col).astype(F32)

    g = [_dot_nt(jnp.concatenate([at[i], rt_b[i]], axis=0), jnp.concatenate([bt[i], kt[i]], axis=0)) for i in n]
    x = [jnp.where(strict[i], g[i][:c, :c], 0.0) for i in n]
    a_ak = [jnp.where(strict[i], g[i][:c, c:], 0.0).astype(BF16) for i in n]
    m_rb = [jnp.where(incl[i], g[i][c:, :c], 0.0).astype(BF16) for i in n]
    m_rk = [jnp.where(incl[i], g[i][c:, c:], 0.0).astype(BF16) for i in n]
    t = [eye + x[i] for i in n]
    pw = [x[i].astype(BF16) for i in n]
    for _ in range((c - 1).bit_length() - 1):
        pw = [mm(pw[i], pw[i]).astype(BF16) for i in n]
        t = [t[i] + mm(t[i].astype(BF16), pw[i]) for i in n]
    z1 = [mm(a_ak[i], v[i]).astype(BF16) for i in n]
    za = [mm(t[i].astype(BF16), jnp.concatenate([at[i], z1[i]], axis=1)).astype(BF16) for i in n]
    w2 = [mm(m_rb[i], za[i]) for i in n]
    yp = [w2[i][:, HEAD:] + mm(m_rk[i], v[i]) for i in n]
    gh = [_dot_tn(be[i], za[i]) for i in n]
    hh = [gh[i][:, HEAD:] + _dot_tn(ke[i], v[i]) for i in n]
    s = [s_scr[d, h] for d, h in chains]
    s_b = [s[i].astype(BF16) for i in n]
    y = [mm((rt[i] + w2[i][:, :HEAD]).astype(BF16), s_b[i]) + yp[i] for i in n]
    for i, (d, h) in enumerate(chains):
        s_scr[d, h] = (ops[d]["pc"][h * HEAD:(h + 1) * HEAD, :] * s[i]
                       + mm(gh[i][:, :HEAD].astype(BF16), s_b[i]) + hh[i])
    yf_ref[0] = jnp.concatenate([y[i] for i, (d, _) in enumerate(chains) if d == 0], axis=1)
    yb_ref[0] = jnp.concatenate([y[i] for i, (d, _) in enumerate(chains) if d == 1], axis=1)

    @pl.when(ci == pl.num_programs(1) - 1)
    def _():
        sout_ref[:, 0] = s_scr[...]


def _wkv(lw, kd, bb, aa, r, v, s0):
    _, b, l, width = lw.shape
    nh = width // HEAD
    c = WKV_CHUNK
    nc = l // c
    dirs = lambda d: pl.BlockSpec((1, 1, c, width), lambda i, j: (d, i, (nc - 1 - j) if d else j, 0))
    both = lambda d: pl.BlockSpec((1, c, width), lambda i, j: (i, (nc - 1 - j) if d else j, 0))
    stspec = pl.BlockSpec((2, 1, nh, HEAD, HEAD), lambda i, j: (0, i, 0, 0, 0))
    return pl.pallas_call(
        _wkv_kernel,
        grid=(b, nc),
        in_specs=[dirs(0), dirs(0), dirs(0), both(0), both(0), both(0),
                  dirs(1), dirs(1), dirs(1), both(1), both(1), both(1), stspec],
        out_specs=[both(0), both(1), stspec],
        out_shape=[jax.ShapeDtypeStruct((b, l, width), F32), jax.ShapeDtypeStruct((b, l, width), F32),
                   jax.ShapeDtypeStruct((2, b, nh, HEAD, HEAD), F32)],
        scratch_shapes=[pltpu.VMEM((2, nh, HEAD, HEAD), F32)],
        compiler_params=_params("arbitrary", "arbitrary"),
        name="wkv",
    )(lw, kd, bb, aa, r, v, lw, kd, bb, aa, r, v, s0)


def _s5_kernel(uc_ref, u_ref, w1_ref, cm_ref, a_ref, y_ref, r_scr, h_scr, *, nb):
    p = S5_STATE
    tw = 2 * S5_CHUNK * S5_GROUP
    w1 = w1_ref[0]
    ar, ai = a_ref[0, 0:1, :], a_ref[0, 1:2, :]
    is_f = lax.broadcasted_iota(jnp.int32, (nb, 2 * p), 1) < p

    def advance(hr, hi, re_f, re_b, im_f, im_b):
        return (hr * ar - hi * ai + jnp.where(is_f, re_f, re_b),
                hr * ai + hi * ar + jnp.where(is_f, im_f, im_b))

    hin = jnp.dot(uc_ref[0], w1[:, tw:], preferred_element_type=F32)
    ncc = hin.shape[0] // nb
    hr = jnp.zeros((nb, 2 * p), F32)
    hi = jnp.zeros((nb, 2 * p), F32)
    for c in range(ncc):
        f = slice(c * nb, (c + 1) * nb)
        b = slice((ncc - 1 - c) * nb, (ncc - c) * nb)
        hr, hi = advance(hr, hi, hin[f, 0:2 * p], hin[b, 0:2 * p], hin[f, 2 * p:4 * p], hin[b, 2 * p:4 * p])
    r_scr[...] = jnp.dot(u_ref[0], w1, preferred_element_type=F32)
    nc = r_scr.shape[0] // nb

    def body(i, carry):
        hr, hi = carry
        rf = pl.ds(pl.multiple_of(i * nb, nb), nb)
        rb = pl.ds(pl.multiple_of((nc - 1 - i) * nb, nb), nb)
        h_scr[rf, 0:p] = hr[:, 0:p]
        h_scr[rb, p:2 * p] = hr[:, p:2 * p]
        h_scr[rf, 2 * p:3 * p] = hi[:, 0:p]
        h_scr[rb, 3 * p:4 * p] = hi[:, p:2 * p]
        return advance(hr, hi, r_scr[rf, tw:tw + 2 * p], r_scr[rb, tw:tw + 2 * p],
                       r_scr[rf, tw + 2 * p:tw + 4 * p], r_scr[rb, tw + 2 * p:tw + 4 * p])

    lax.fori_loop(0, nc, body, (hr, hi))
    half = tw // 2
    y_ref[0] = (r_scr[:, 0:half] + r_scr[:, half:tw]
                + jnp.dot(h_scr[...].astype(BF16), cm_ref[0], preferred_element_type=F32))


def _s5_scan(u3c, u3, w1, cm, acat, nb):
    g, rows, kw = u3.shape
    rows_c = u3c.shape[1]
    nw = w1.shape[2]
    return pl.pallas_call(
        functools.partial(_s5_kernel, nb=nb),
        grid=(g,),
        in_specs=[pl.BlockSpec((1, rows_c, kw), lambda i: (i, 0, 0)),
                  pl.BlockSpec((1, rows, kw), lambda i: (i, 0, 0)),
                  pl.BlockSpec((1, kw, nw), lambda i: (i, 0, 0)),
                  pl.BlockSpec((1, 4 * S5_STATE, kw), lambda i: (i, 0, 0)),
                  pl.BlockSpec((1, 2, 2 * S5_STATE), lambda i: (i, 0, 0))],
        out_specs=pl.BlockSpec((1, rows, kw), lambda i: (i, 0, 0)),
        out_shape=jax.ShapeDtypeStruct((g, rows, kw), F32),
        scratch_shapes=[pltpu.VMEM((rows, nw), F32), pltpu.VMEM((rows, 4 * S5_STATE), F32)],
        compiler_params=_params("arbitrary"),
        name="s5_scan",
    )(u3c, u3, w1, cm, acat)


def _s5_matrices(lam_re, lam_im, log_dt, b_re, b_im, c_re, c_im):
    hp = lax.Precision.HIGHEST
    t = S5_CHUNK
    dt = jnp.exp(log_dt.astype(F32))[..., None]
    lr, li = lam_re.astype(F32), lam_im.astype(F32)
    j = jnp.arange(t + 1, dtype=F32)[:, None, None, None]
    mag = jnp.exp(j * dt * lr)
    pr, pi = mag * jnp.cos(j * dt * li), mag * jnp.sin(j * dt * li)
    ar, ai = pr[1], pi[1]
    den = lr * lr + li * li
    fr = ((ar - 1.0) * lr + ai * li) / den
    fi = (ai * lr - (ar - 1.0) * li) / den
    bbr = fr[..., None] * b_re - fi[..., None] * b_im
    bbi = fr[..., None] * b_im + fi[..., None] * b_re
    car = c_re[None, None] * pr[:, :, :, None, :] - c_im[None, None] * pi[:, :, :, None, :]
    cai = c_re[None, None] * pi[:, :, :, None, :] + c_im[None, None] * pr[:, :, :, None, :]
    kj = (jnp.einsum('jdgcp,dgpe->jdgce', car, bbr, precision=hp)
          - jnp.einsum('jdgcp,dgpe->jdgce', cai, bbi, precision=hp))
    sig = jnp.arange(t)[:, None]
    tau = jnp.arange(t)[None, :]
    g = lr.shape[1]

    def toeplitz(d, lag, ok):
        m = kj[jnp.clip(lag, 0, t), d]
        m = jnp.where(ok[:, :, None, None, None], m, 0.0)
        return m.transpose(2, 0, 4, 1, 3).reshape(g, t * S5_GROUP, t * S5_GROUP)

    m_f = toeplitz(0, tau - sig, tau >= sig)
    m_b = toeplitz(1, sig - tau, sig >= tau)

    def state_in(d, pw):
        qr = pr[pw, d][..., None] * bbr[d][None] - pi[pw, d][..., None] * bbi[d][None]
        qi = pr[pw, d][..., None] * bbi[d][None] + pi[pw, d][..., None] * bbr[d][None]
        lay = lambda q: q.transpose(1, 0, 3, 2).reshape(g, t * S5_GROUP, S5_STATE)
        return lay(qr), lay(qi)

    bre_f, bim_f = state_in(0, t - 1 - jnp.arange(t))
    bre_b, bim_b = state_in(1, jnp.arange(t))

    def state_out(d, pw):
        lay = lambda q: q.transpose(1, 3, 0, 2).reshape(g, S5_STATE, t * S5_GROUP)
        return lay(car[pw, d]), lay(-cai[pw, d])

    cre_f, cim_f = state_out(0, jnp.arange(t) + 1)
    cre_b, cim_b = state_out(1, t - jnp.arange(t))
    w1 = jnp.concatenate([m_f, m_b, bre_f, bre_b, bim_f, bim_b], axis=2).astype(BF16)
    cm = jnp.concatenate([cre_f, cre_b, cim_f, cim_b], axis=1).astype(BF16)
    at, ait = pr[t], pi[t]
    acat = jnp.stack([jnp.concatenate([at[0], at[1]], -1), jnp.concatenate([ait[0], ait[1]], -1)], axis=1)
    return w1, cm, acat


def _to_s5_rows(u):
    b, l, c = u.shape
    g = c // S5_GROUP
    u = u.reshape(b, l // S5_CHUNK, S5_CHUNK, g, S5_GROUP).transpose(3, 1, 0, 2, 4)
    return u.reshape(g, (l // S5_CHUNK) * b, S5_CHUNK * S5_GROUP)


def _from_s5_rows(y, b):
    g, rows, kw = y.shape
    nc = rows // b
    y = y.reshape(g, nc, b, S5_CHUNK, S5_GROUP).transpose(2, 1, 3, 0, 4)
    return y.reshape(b, nc * S5_CHUNK, g * S5_GROUP)


def _out_kernel(x_ref, gate_ref, yf_ref, yb_ref, bonus_ref, gr_ref, u_ref, gs_ref, ys_ref, lg_ref, lb_ref, d_ref,
                wg_ref, wo_ref, fg_ref, e_ref, o_ref):
    dr = bonus_ref.shape[2]
    y = yf_ref[0] + yb_ref[0]
    mean = _segsum(y, e_ref) * (1.0 / HEAD)
    dev = y - mean
    var = _segsum(dev * dev, e_ref) * (1.0 / HEAD)
    yn = dev * lax.rsqrt(var + LN_X_EPS) * lg_ref[...] + lb_ref[...]
    gr = gr_ref[0]
    y_r = (yn + bonus_ref[0]) * (gr * _sigmoid(gr))

    s = ys_ref[0] + d_ref[...] * u_ref[0]
    s = 0.5 * s * (1.0 + jnp.tanh(math.sqrt(2.0 / math.pi) * (s + 0.044715 * (s * s * s))))
    gl = jnp.dot(s.astype(BF16), wg_ref[...], preferred_element_type=F32)
    ds = gl.shape[1] // 2
    gs = gs_ref[0]
    y_s = gl[:, :ds] * _sigmoid(gl[:, ds:]) * (gs * _sigmoid(gs))

    out = (jnp.dot(y_r.astype(BF16), wo_ref[0:dr, :], preferred_element_type=F32)
           + jnp.dot(y_s.astype(BF16), wo_ref[dr:, :], preferred_element_type=F32))
    xo = x_ref[0] + gate_ref[0] * out
    ms = jnp.mean(xo * xo, axis=-1, keepdims=True)
    o_ref[0] = (xo * lax.rsqrt(ms + EPS) * fg_ref[...]).astype(o_ref.dtype)


def _out_stage(x, gate, yf, yb, bonus, z2, ys, lnx_g, lnx_b, s5_d, w_glu, w_out, final_g, e):
    b, l, d = x.shape
    dr = bonus.shape[2]
    ds = ys.shape[2]
    assert dr == ds
    tm = 512
    tok = lambda w, cb: pl.BlockSpec((1, tm, w), lambda i, j: (i, j, cb))
    const = lambda *shape: pl.BlockSpec(shape, lambda i, j: (0,) * len(shape))
    return pl.pallas_call(
        _out_kernel,
        grid=(b, l // tm),
        in_specs=[tok(d, 0), pl.BlockSpec((1, 1, d), lambda i, j: (i, 0, 0)),
                  tok(dr, 0), tok(dr, 0), tok(dr, 0), tok(dr, 0), tok(ds, 1), tok(ds, 2), tok(ds, 0),
                  const(1, dr), const(1, dr), const(1, ds), const(ds, 2 * ds), const(dr + ds, d),
                  const(1, d), const(dr, dr)],
        out_specs=tok(d, 0),
        out_shape=jax.ShapeDtypeStruct((b, l, d), x.dtype),
        compiler_params=_params("arbitrary", "arbitrary"),
        name="out_stage",
    )(x, gate, yf, yb, bonus, z2, z2, z2, ys, lnx_g.reshape(1, dr), lnx_b.reshape(1, dr), s5_d.reshape(1, ds),
      w_glu.astype(BF16), w_out.astype(BF16), final_g.reshape(1, d), e)


def kernel(x, c, ctx, c_ctx, norm_g, w_ada, b_ada, w_in, mu_shift, rwkv_w0, rwkv_w2, rwkv_a0, rwkv_a2, rwkv_k_k, rwkv_k_a, rwkv_r_k, lnx_g, lnx_b, s5_lam_re, s5_lam_im, s5_log_dt, s5_b_re, s5_b_im, s5_c_re, s5_c_im, s5_d, s5_w_glu, w_out, final_g):
    assert norm_g.shape[0] == 1, "one layer"
    b, l, d = x.shape
    lc = ctx.shape[1]
    dr = rwkv_k_k.shape[1]
    ds = s5_d.shape[1]
    sw = mu_shift.shape[1]
    assert sw == 3 * dr + 4 * LORA and l % (4 * GRID_W) == 0 and lc % WKV_CHUNK == 0 and b % 8 == 0

    cond = jnp.zeros((2 * b, d), F32).at[:b].set(c).at[b].set(c_ctx)
    m = _modulation(cond, w_ada[0], b_ada[0])
    shift, scale, gate = m[:b, :d], m[:b, d:2 * d], m[:b, 2 * d:]
    cshift = jnp.broadcast_to(m[b, :d], (b, d))
    cscale = jnp.broadcast_to(m[b, d:2 * d], (b, d))

    w1 = w_in[0][:, :sw].astype(BF16)
    w2 = w_in[0][:, sw:].astype(BF16)
    z1c, z2c = _inproj(ctx, norm_g[0], cscale[:, None], cshift[:, None], w1, w2)
    z1, z2 = _inproj(x, norm_g[0], scale[:, None], shift[:, None], w1, w2)

    hid = jnp.arange(dr) // HEAD
    e = (hid[:, None] == hid[None, :]).astype(BF16)
    pp = (mu_shift, rwkv_w0[0], rwkv_w2[0], rwkv_a0[0], rwkv_a2[0], rwkv_k_k, rwkv_k_a,
          rwkv_r_k[0].reshape(1, dr), e)
    rc, vc, aac, wc, kdc, bbc, _ = _prep(z1c, pp, grid2d=False)
    rx, vx, aax, wx, kdx, bbx, bonus = _prep(z1, pp, grid2d=True)

    s0 = jnp.zeros((2, b, dr // HEAD, HEAD, HEAD), F32)
    _, _, s_ctx = _wkv(wc, kdc, bbc, aac, rc, vc, s0)
    yf, yb, _ = _wkv(wx, kdx, bbx, aax, rx, vx, s_ctx)

    sm_w1, sm_cm, sm_a = _s5_matrices(s5_lam_re[0], s5_lam_im[0], s5_log_dt[0], s5_b_re[0], s5_b_im[0],
                                      s5_c_re[0], s5_c_im[0])
    u3c = _to_s5_rows(z2c[:, :, dr:dr + ds]).astype(BF16)
    u3 = _to_s5_rows(z2[:, :, dr:dr + ds]).astype(BF16)
    ys = _from_s5_rows(_s5_scan(u3c, u3, sm_w1, sm_cm, sm_a, b), b)

    return _out_stage(x, gate[:, None], yf, yb, bonus, z2, ys, lnx_g[0], lnx_b[0], s5_d[0], s5_w_glu[0],
                      w_out[0], final_g, e)
```

```python
import functools
import math

import jax
import jax.numpy as jnp
from jax import lax
from jax.experimental import pallas as pl
from jax.experimental.pallas import tpu as pltpu

F32 = jnp.float32
BF16 = jnp.bfloat16

GRID_W = 64
HEAD = 64
LORA = 64
S5_GROUP = 16
S5_STATE = 64
S5_CHUNK = 16
S5_ROWS = 256
WKV_CHUNK = 64
EPS = 1e-6
LN_X_EPS = 64e-5
EXP_M05 = math.exp(-0.5)

LANES = 128
VMEM_LIMIT = 56 * 1024 * 1024


def _params(*sem):
    return pltpu.CompilerParams(dimension_semantics=sem, vmem_limit_bytes=VMEM_LIMIT)


def _sigmoid(x):
    return 1.0 / (1.0 + jnp.exp(-x))


def _dot_bf16(a, b):
    return jnp.dot(a.astype(BF16), b.astype(BF16), preferred_element_type=F32)


def _dot_f32(a, b):
    return jnp.dot(a, b, preferred_element_type=F32, precision=lax.Precision.HIGHEST)


def _segsum(x, e_ref):
    hi = x.astype(BF16)
    lo = (x - hi.astype(F32)).astype(BF16)
    e = e_ref[...]
    return (jnp.dot(hi, e, preferred_element_type=F32) + jnp.dot(lo, e, preferred_element_type=F32))


def _mod_kernel(c_ref, w_ref, b_ref, o_ref):
    c = c_ref[...]
    o_ref[...] = _dot_f32(c * _sigmoid(c), w_ref[...]) + b_ref[...]


def _modulation(cond, w_ada, b_ada):
    rows, d = cond.shape
    n = w_ada.shape[1]
    tn = 512
    return pl.pallas_call(
        _mod_kernel,
        grid=(n // tn,),
        in_specs=[pl.BlockSpec((rows, d), lambda j: (0, 0)),
                  pl.BlockSpec((d, tn), lambda j: (0, j)),
                  pl.BlockSpec((1, tn), lambda j: (0, j))],
        out_specs=pl.BlockSpec((rows, tn), lambda j: (0, j)),
        out_shape=jax.ShapeDtypeStruct((rows, n), F32),
        compiler_params=_params("arbitrary"),
        name="modulation",
    )(cond, w_ada, b_ada.reshape(1, n))


def _inproj_kernel(x_ref, g_ref, sc_ref, sh_ref, w1_ref, w2_ref, z1_ref, z2_ref, u_ref):
    x = x_ref[0]
    ms = jnp.mean(x * x, axis=-1, keepdims=True)
    h = x * lax.rsqrt(ms + EPS) * g_ref[...]
    h = (h * (1.0 + sc_ref[0]) + sh_ref[0]).astype(BF16)
    z1_ref[0] = jnp.dot(h, w1_ref[...], preferred_element_type=F32)
    z2 = jnp.dot(h, w2_ref[...], preferred_element_type=F32)
    z2_ref[0] = z2
    nblk, _, nch = u_ref.shape[0], u_ref.shape[1], u_ref.shape[2]
    w = z2.shape[1] // 3
    for j in range(nblk):
        u_ref[j, 0] = z2[:, w + LANES * j:w + LANES * (j + 1)].reshape(nch, S5_CHUNK, LANES)


def _inproj(x, norm_g, scale, shift, w1, w2):
    b, l, d = x.shape
    n1, n2 = w1.shape[1], w2.shape[1]
    tm = min(l, 512)
    nblk = n2 // 3 // LANES
    return pl.pallas_call(
        _inproj_kernel,
        grid=(b, l // tm),
        in_specs=[pl.BlockSpec((1, tm, d), lambda i, j: (i, j, 0)),
                  pl.BlockSpec((1, d), lambda i, j: (0, 0)),
                  pl.BlockSpec((1, 1, d), lambda i, j: (i, 0, 0)),
                  pl.BlockSpec((1, 1, d), lambda i, j: (i, 0, 0)),
                  pl.BlockSpec((d, n1), lambda i, j: (0, 0)),
                  pl.BlockSpec((d, n2), lambda i, j: (0, 0))],
        out_specs=[pl.BlockSpec((1, tm, n1), lambda i, j: (i, j, 0)),
                   pl.BlockSpec((1, tm, n2), lambda i, j: (i, j, 0)),
                   pl.BlockSpec((nblk, 1, tm // S5_CHUNK, S5_CHUNK, LANES), lambda i, j: (0, i, j, 0, 0))],
        out_shape=[jax.ShapeDtypeStruct((b, l, n1), F32),
                   jax.ShapeDtypeStruct((b, l, n2), F32),
                   jax.ShapeDtypeStruct((nblk, b, l // S5_CHUNK, S5_CHUNK, LANES), F32)],
        compiler_params=_params("arbitrary", "arbitrary"),
        name="inproj",
    )(x, norm_g.reshape(1, d), scale, shift, w1, w2)


def _prep_body(z, up, down, mu_ref, w0_ref, w2_ref, a0_ref, a2_ref, kk_ref, ka_ref, rk_ref, e_ref, outs,
               grid2d):
    t, sw = z.shape
    dr = (sw - 4 * LORA) // 3
    row = lax.broadcasted_iota(jnp.int32, (t, 1), 0)
    slot = lax.broadcasted_iota(jnp.int32, (1, sw), 1) & 3
    prev = pltpu.roll(z, 1, 0)
    nxt = pltpu.roll(z, t - 1, 0)
    if grid2d:
        col = row & (GRID_W - 1)
        prev = jnp.where(col == 0, 0.0, prev)
        nxt = jnp.where(col == GRID_W - 1, 0.0, nxt)
        sh = jnp.where(slot == 0, prev, jnp.where(slot == 1, nxt, jnp.where(slot == 2, up, down)))
    else:
        prev = jnp.where(row == 0, 0.0, prev)
        nxt = jnp.where(row == t - 1, 0.0, nxt)
        sh = jnp.where((slot & 1) == 0, prev, nxt)
    zs = z + mu_ref[...] * (sh - z)

    r = zs[:, 0:dr]
    k = zs[:, dr:2 * dr]
    v = zs[:, 2 * dr:3 * dr]
    kk = k * kk_ref[...]
    ss = _segsum(kk * kk, e_ref)
    kk = kk / jnp.maximum(jnp.sqrt(ss), 1e-12)
    r_o, v_o, aa_o, w_o, kd_o, bb_o, bonus_o = outs
    r_o[0] = r
    v_o[0] = v
    aa_o[0] = -kk
    ksum = None
    for d in range(2):
        zw = zs[:, 3 * dr + LORA * d:3 * dr + LORA * (d + 1)]
        za = zs[:, 3 * dr + 2 * LORA + LORA * d:3 * dr + 2 * LORA + LORA * (d + 1)]
        wl = w0_ref[d:d + 1, :] + _dot_f32(jnp.tanh(zw), w2_ref[d])
        w_o[d, 0] = -EXP_M05 * _sigmoid(wl)
        asig = _sigmoid(a0_ref[d:d + 1, :] + _dot_f32(za, a2_ref[d]))
        kd = k * (1.0 + (asig - 1.0) * ka_ref[...])
        kd_o[d, 0] = kd
        bb_o[d, 0] = kk * asig
        ksum = kd if ksum is None else ksum + kd
    bonus_o[0] = _segsum(r * (0.5 * ksum) * rk_ref[...], e_ref) * v


def _prep2d_kernel(zc_ref, zu_ref, zd_ref, *rest):
    params, outs = rest[:9], rest[9:]
    j = pl.program_id(1)
    nj = pl.num_programs(1)
    z = zc_ref[0]
    t = z.shape[0]
    row = lax.broadcasted_iota(jnp.int32, (t, 1), 0)
    up = jnp.concatenate([zu_ref[0], z[:t - GRID_W]], axis=0)
    up = jnp.where(jnp.logical_and(j == 0, row < GRID_W), 0.0, up)
    down = jnp.concatenate([z[GRID_W:], zd_ref[0]], axis=0)
    down = jnp.where(jnp.logical_and(j == nj - 1, row >= t - GRID_W), 0.0, down)
    _prep_body(z, up, down, *params, outs, grid2d=True)


def _prep1d_kernel(zc_ref, *rest):
    params, outs = rest[:9], rest[9:]
    _prep_body(zc_ref[0], None, None, *params, outs, grid2d=False)


def _prep(z1, pp, grid2d):
    b, l, sw = z1.shape
    dr = (sw - 4 * LORA) // 3
    tt = 256 if grid2d else l
    nj = l // tt
    rb = tt // GRID_W
    const = lambda *shape: pl.BlockSpec(shape, lambda i, j: (0,) * len(shape))
    p_specs = [const(1, sw), const(2, dr), const(2, LORA, dr), const(2, dr), const(2, LORA, dr),
               const(1, dr), const(1, dr), const(1, dr), const(dr, dr)]
    cur = pl.BlockSpec((1, tt, sw), lambda i, j: (i, j, 0))
    if grid2d:
        nrow = l // GRID_W
        in_specs = [cur,
                    pl.BlockSpec((1, GRID_W, sw), lambda i, j: (i, jnp.maximum(j * rb - 1, 0), 0)),
                    pl.BlockSpec((1, GRID_W, sw), lambda i, j: (i, jnp.minimum((j + 1) * rb, nrow - 1), 0))]
        args = (z1, z1, z1)
        body = _prep2d_kernel
    else:
        in_specs = [cur]
        args = (z1,)
        body = _prep1d_kernel
    o1 = pl.BlockSpec((1, tt, dr), lambda i, j: (i, j, 0))
    o2 = pl.BlockSpec((2, 1, tt, dr), lambda i, j: (0, i, j, 0))
    s1 = jax.ShapeDtypeStruct((b, l, dr), F32)
    s2 = jax.ShapeDtypeStruct((2, b, l, dr), F32)
    return pl.pallas_call(
        body,
        grid=(b, nj),
        in_specs=in_specs + p_specs,
        out_specs=[o1, o1, o1, o2, o2, o2, o1],
        out_shape=[s1, s1, s1, s2, s2, s2, s1],
        compiler_params=_params("arbitrary", "arbitrary"),
        name="prep2d" if grid2d else "prep1d",
    )(*args, *pp)


def _dot_nt(a, b):
    return lax.dot_general(a, b, (((1,), (1,)), ((), ())), preferred_element_type=F32)


def _dot_tn(a, b):
    return lax.dot_general(a, b, (((0,), (0,)), ((), ())), preferred_element_type=F32)


def _wkv_operands(lw_ref, kd_ref, bb_ref, aa_ref, r_ref, v_ref, backward):
    c = aa_ref.shape[1]
    row = lax.broadcasted_iota(jnp.int32, (c, c), 0)
    col = lax.broadcasted_iota(jnp.int32, (c, c), 1)
    strict, incl = (col > row, col >= row) if backward else (col < row, col <= row)
    lw = lw_ref[0, 0]
    lw_hi = lw.astype(BF16)
    lw_lo = (lw - lw_hi.astype(F32)).astype(BF16)
    lc = incl.astype(BF16)
    cum = jnp.dot(lc, lw_hi, preferred_element_type=F32) + jnp.dot(lc, lw_lo, preferred_element_type=F32)
    tot = cum[0:1, :] if backward else cum[c - 1:c, :]
    ones = jnp.ones((c, HEAD), BF16)
    pinv = jnp.exp(-cum)
    pend = jnp.exp(tot - cum)
    bb, kd = bb_ref[0, 0], kd_ref[0, 0]
    rt = r_ref[0] * jnp.exp(cum)
    return dict(
        strict=strict, incl=incl,
        pc=jnp.exp(_dot_tn(lw_hi, ones) + _dot_tn(lw_lo, ones)),
        at=(aa_ref[0] * jnp.exp(cum - lw)).astype(BF16), rt=rt, rt_b=rt.astype(BF16),
        bt=(bb * pinv).astype(BF16), kt=(kd * pinv).astype(BF16),
        be=(bb * pend).astype(BF16), ke=(kd * pend).astype(BF16), v=v_ref[0].astype(BF16))


def _wkv_kernel(lwf_ref, kdf_ref, bbf_ref, aaf_ref, rf_ref, vf_ref, lwb_ref, kdb_ref, bbb_ref, aab_ref, rb_ref,
                vb_ref, s0_ref, yf_ref, yb_ref, sout_ref, s_scr):
    ci = pl.program_id(1)
    c, width = aaf_ref.shape[1], aaf_ref.shape[2]
    nh = width // HEAD

    @pl.when(ci == 0)
    def _():
        s_scr[...] = s0_ref[:, 0]

    ops = (_wkv_operands(lwf_ref, kdf_ref, bbf_ref, aaf_ref, rf_ref, vf_ref, False),
           _wkv_operands(lwb_ref, kdb_ref, bbb_ref, aab_ref, rb_ref, vb_ref, True))
    chains = [(d, h) for h in range(nh) for d in range(2)]
    cut = lambda name: [ops[d][name][:, h * HEAD:(h + 1) * HEAD] for d, h in chains]
    at, rt, rt_b, bt, kt, be, ke, v = (cut(n) for n in ("at", "rt", "rt_b", "bt", "kt", "be", "ke", "v"))
    strict = [ops[d]["strict"] for d, _ in chains]
    incl = [ops[d]["incl"] for d, _ in chains]
    n = range(len(chains))
    mm = lambda a, b: jnp.dot(a, b, preferred_element_type=F32)
    row = lax.broadcasted_iota(jnp.int32, (c, c), 0)
    col = lax.broadcasted_iota(jnp.int32, (c, c), 1)
    eye = (row == col).astype(F32)

    g = [_dot_nt(jnp.concatenate([at[i], rt_b[i]], axis=0), jnp.concatenate([bt[i], kt[i]], axis=0)) for i in n]
    x = [jnp.where(strict[i], g[i][:c, :c], 0.0) for i in n]
    a_ak = [jnp.where(strict[i], g[i][:c, c:], 0.0).astype(BF16) for i in n]
    m_rb = [jnp.where(incl[i], g[i][c:, :c], 0.0).astype(BF16) for i in n]
    m_rk = [jnp.where(incl[i], g[i][c:, c:], 0.0).astype(BF16) for i in n]
    t = [eye + x[i] for i in n]
    pw = [x[i].astype(BF16) for i in n]
    for _ in range((c - 1).bit_length() - 1):
        pw = [mm(pw[i], pw[i]).astype(BF16) for i in n]
        t = [t[i] + mm(t[i].astype(BF16), pw[i]) for i in n]
    z1 = [mm(a_ak[i], v[i]).astype(BF16) for i in n]
    za = [mm(t[i].astype(BF16), jnp.concatenate([at[i], z1[i]], axis=1)).astype(BF16) for i in n]
    w2 = [mm(m_rb[i], za[i]) for i in n]
    yp = [w2[i][:, HEAD:] + mm(m_rk[i], v[i]) for i in n]
    gh = [_dot_tn(be[i], za[i]) for i in n]
    hh = [gh[i][:, HEAD:] + _dot_tn(ke[i], v[i]) for i in n]
    s = [s_scr[d, h] for d, h in chains]
    s_b = [s[i].astype(BF16) for i in n]
    y = [mm((rt[i] + w2[i][:, :HEAD]).astype(BF16), s_b[i]) + yp[i] for i in n]
    for i, (d, h) in enumerate(chains):
        s_scr[d, h] = (ops[d]["pc"][h * HEAD:(h + 1) * HEAD, :] * s[i]
                       + mm(gh[i][:, :HEAD].astype(BF16), s_b[i]) + hh[i])
    yf_ref[0] = jnp.concatenate([y[i] for i, (d, _) in enumerate(chains) if d == 0], axis=1)
    yb_ref[0] = jnp.concatenate([y[i] for i, (d, _) in enumerate(chains) if d == 1], axis=1)

    @pl.when(ci == pl.num_programs(1) - 1)
    def _():
        sout_ref[:, 0] = s_scr[...]


def _wkv(lw, kd, bb, aa, r, v, s0):
    _, b, l, width = lw.shape
    nh = width // HEAD
    c = WKV_CHUNK
    nc = l // c
    dirs = lambda d: pl.BlockSpec((1, 1, c, width), lambda i, j: (d, i, (nc - 1 - j) if d else j, 0))
    both = lambda d: pl.BlockSpec((1, c, width), lambda i, j: (i, (nc - 1 - j) if d else j, 0))
    stspec = pl.BlockSpec((2, 1, nh, HEAD, HEAD), lambda i, j: (0, i, 0, 0, 0))
    return pl.pallas_call(
        _wkv_kernel,
        grid=(b, nc),
        in_specs=[dirs(0), dirs(0), dirs(0), both(0), both(0), both(0),
                  dirs(1), dirs(1), dirs(1), both(1), both(1), both(1), stspec],
        out_specs=[both(0), both(1), stspec],
        out_shape=[jax.ShapeDtypeStruct((b, l, width), F32), jax.ShapeDtypeStruct((b, l, width), F32),
                   jax.ShapeDtypeStruct((2, b, nh, HEAD, HEAD), F32)],
        scratch_shapes=[pltpu.VMEM((2, nh, HEAD, HEAD), F32)],
        compiler_params=_params("arbitrary", "arbitrary"),
        name="wkv",
    )(lw, kd, bb, aa, r, v, lw, kd, bb, aa, r, v, s0)


def _s5_kernel(*refs, nb, reverse, toeplitz):
    if toeplitz:
        u_ref, wt_ref, win_ref, wout_ref, a_ref, h0_ref, y_ref, hout_ref, x_scr, hh_scr, hre_scr, him_scr = refs
    else:
        u_ref, win_ref, wout_ref, a_ref, h0_ref, y_ref, hout_ref, x_scr, hh_scr, hre_scr, him_scr = refs
    i = pl.program_id(1)
    ns = hre_scr.shape[1]

    @pl.when(i == 0)
    def _():
        hre_scr[...] = h0_ref[0, 0]
        him_scr[...] = h0_ref[0, 1]

    u = u_ref[0].astype(BF16)
    x_scr[...] = jnp.dot(u, win_ref[0], preferred_element_type=F32)
    ar, ai = a_ref[0, 0:1, :], a_ref[0, 1:2, :]
    nch = x_scr.shape[0] // nb

    def body(k, carry):
        hr, hi = carry
        c = nch - 1 - k if reverse else k
        rows = pl.ds(pl.multiple_of(c * nb, nb), nb)
        hh_scr[rows, 0:ns] = hr
        hh_scr[rows, ns:2 * ns] = hi
        return (hr * ar - hi * ai + x_scr[rows, 0:ns], hr * ai + hi * ar + x_scr[rows, ns:2 * ns])

    hr, hi = lax.fori_loop(0, nch, body, (hre_scr[...], him_scr[...]))
    hre_scr[...] = hr
    him_scr[...] = hi
    y = jnp.dot(hh_scr[...].astype(BF16), wout_ref[0], preferred_element_type=F32)
    if toeplitz:
        y = y + jnp.dot(u, wt_ref[0], preferred_element_type=F32)
    y_ref[0] = y

    @pl.when(i == pl.num_programs(1) - 1)
    def _():
        hout_ref[0, 0] = hr
        hout_ref[0, 1] = hi


def _s5_scan(u, wt, win, wout, acoef, h0, nb, reverse):
    nblk, rows, kw = u.shape
    ns = win.shape[2] // 2
    rt = min(rows, S5_ROWS)
    nt = rows // rt
    tile = pl.BlockSpec((1, rt, kw), lambda g, i: (g, (nt - 1 - i) if reverse else i, 0))
    per_blk = lambda a: pl.BlockSpec((1,) + a.shape[1:], lambda g, i: (g,) + (0,) * (a.ndim - 1))
    weights = ([wt] if wt is not None else []) + [win, wout, acoef, h0]
    return pl.pallas_call(
        functools.partial(_s5_kernel, nb=nb, reverse=reverse, toeplitz=wt is not None),
        grid=(nblk, nt),
        in_specs=[tile] + [per_blk(a) for a in weights],
        out_specs=[tile, per_blk(h0)],
        out_shape=[jax.ShapeDtypeStruct((nblk, rows, kw), F32), jax.ShapeDtypeStruct(h0.shape, F32)],
        scratch_shapes=[pltpu.VMEM((rt, 2 * ns), F32), pltpu.VMEM((rt, 2 * ns), F32),
                        pltpu.VMEM((nb, ns), F32), pltpu.VMEM((nb, ns), F32)],
        compiler_params=_params("arbitrary", "arbitrary"),
        name="s5_bwd" if reverse else "s5_fwd",
    )(u, *weights)


def _s5_matrices(lam_re, lam_im, log_dt, b_re, b_im, c_re, c_im):
    hp = lax.Precision.HIGHEST
    t = S5_CHUNK
    dt = jnp.exp(log_dt.astype(F32))[..., None]
    lr, li = lam_re.astype(F32), lam_im.astype(F32)
    j = jnp.arange(t + 1, dtype=F32)[:, None, None, None]
    mag = jnp.exp(j * dt * lr)
    pr, pi = mag * jnp.cos(j * dt * li), mag * jnp.sin(j * dt * li)
    ar, ai = pr[1], pi[1]
    den = lr * lr + li * li
    fr = ((ar - 1.0) * lr + ai * li) / den
    fi = (ai * lr - (ar - 1.0) * li) / den
    bbr = fr[..., None] * b_re - fi[..., None] * b_im
    bbi = fr[..., None] * b_im + fi[..., None] * b_re
    car = c_re[None, None] * pr[:, :, :, None, :] - c_im[None, None] * pi[:, :, :, None, :]
    cai = c_re[None, None] * pi[:, :, :, None, :] + c_im[None, None] * pr[:, :, :, None, :]
    kj = (jnp.einsum('jdgcp,dgpe->jdgce', car, bbr, precision=hp)
          - jnp.einsum('jdgcp,dgpe->jdgce', cai, bbi, precision=hp))
    sig = jnp.arange(t)[:, None]
    tau = jnp.arange(t)[None, :]
    g = lr.shape[1]

    def toeplitz(d, lag, ok):
        m = kj[jnp.clip(lag, 0, t), d]
        m = jnp.where(ok[:, :, None, None, None], m, 0.0)
        return m.transpose(2, 0, 4, 1, 3).reshape(g, t * S5_GROUP, t * S5_GROUP)

    m_f = toeplitz(0, tau - sig, tau >= sig)
    m_b = toeplitz(1, sig - tau, sig >= tau)

    def state_in(d, pw):
        qr = pr[pw, d][..., None] * bbr[d][None] - pi[pw, d][..., None] * bbi[d][None]
        qi = pr[pw, d][..., None] * bbi[d][None] + pi[pw, d][..., None] * bbr[d][None]
        lay = lambda q: q.transpose(1, 0, 3, 2).reshape(g, t * S5_GROUP, S5_STATE)
        return lay(qr), lay(qi)

    bre_f, bim_f = state_in(0, t - 1 - jnp.arange(t))
    bre_b, bim_b = state_in(1, jnp.arange(t))

    def state_out(d, pw):
        lay = lambda q: q.transpose(1, 3, 0, 2).reshape(g, S5_STATE, t * S5_GROUP)
        return lay(car[pw, d]), lay(-cai[pw, d])

    cre_f, cim_f = state_out(0, jnp.arange(t) + 1)
    cre_b, cim_b = state_out(1, t - jnp.arange(t))

    gpb = LANES // S5_GROUP
    nblk = g // gpb
    eye = jnp.eye(gpb, dtype=F32)
    kw = t * LANES
    m6 = (m_f + m_b).reshape(nblk, gpb, t, S5_GROUP, t, S5_GROUP)
    wt = jnp.einsum('bgscte,gh->bsgcthe', m6, eye).reshape(nblk, kw, kw).astype(BF16)

    def widen_in(re, im):
        q = jnp.stack([re, im], axis=2).reshape(nblk, gpb, t, S5_GROUP, 2, S5_STATE)
        return jnp.einsum('bgscrp,gh->bsgcrhp', q, eye).reshape(nblk, kw, 2 * gpb * S5_STATE).astype(BF16)

    def widen_out(re, im):
        q = jnp.stack([re, im], axis=1).reshape(nblk, gpb, 2, S5_STATE, t, S5_GROUP)
        return jnp.einsum('bgrpte,gh->brgpthe', q, eye).reshape(nblk, 2 * gpb * S5_STATE, kw).astype(BF16)

    coef = lambda d: jnp.stack([pr[t, d].reshape(nblk, gpb * S5_STATE), pi[t, d].reshape(nblk, gpb * S5_STATE)],
                               axis=1)
    fwd = (wt, widen_in(bre_f, bim_f), widen_out(cre_f, cim_f), coef(0))
    bwd = (None, widen_in(bre_b, bim_b), widen_out(cre_b, cim_b), coef(1))
    return fwd, bwd


def _s5_rows(u5):
    nblk, b, nc, t, w = u5.shape
    return u5.transpose(0, 2, 1, 3, 4).reshape(nblk, nc * b, t * w)


def _s5_unrows(y, b):
    nblk, rows, kw = y.shape
    return y.reshape(nblk, rows // b, b, S5_CHUNK, kw // S5_CHUNK).transpose(0, 2, 1, 3, 4)


def _out_kernel(x_ref, gate_ref, yf_ref, yb_ref, bonus_ref, gr_ref, u_ref, gs_ref, ys_ref, lg_ref, lb_ref, d_ref,
                wg_ref, wo_ref, fg_ref, e_ref, o_ref):
    dr = bonus_ref.shape[2]
    y = yf_ref[0] + yb_ref[0]
    mean = _segsum(y, e_ref) * (1.0 / HEAD)
    dev = y - mean
    var = _segsum(dev * dev, e_ref) * (1.0 / HEAD)
    yn = dev * lax.rsqrt(var + LN_X_EPS) * lg_ref[...] + lb_ref[...]
    gr = gr_ref[0]
    y_r = (yn + bonus_ref[0]) * (gr * _sigmoid(gr))

    tm = u_ref.shape[1]
    ys = jnp.concatenate([ys_ref[j, 0].reshape(tm, LANES) for j in range(ys_ref.shape[0])], axis=1)
    s = ys + d_ref[...] * u_ref[0]
    s = 0.5 * s * (1.0 + jnp.tanh(math.sqrt(2.0 / math.pi) * (s + 0.044715 * (s * s * s))))
    gl = jnp.dot(s.astype(BF16), wg_ref[...], preferred_element_type=F32)
    ds = gl.shape[1] // 2
    gs = gs_ref[0]
    y_s = gl[:, :ds] * _sigmoid(gl[:, ds:]) * (gs * _sigmoid(gs))

    out = (jnp.dot(y_r.astype(BF16), wo_ref[0:dr, :], preferred_element_type=F32)
           + jnp.dot(y_s.astype(BF16), wo_ref[dr:, :], preferred_element_type=F32))
    xo = x_ref[0] + gate_ref[0] * out
    ms = jnp.mean(xo * xo, axis=-1, keepdims=True)
    o_ref[0] = (xo * lax.rsqrt(ms + EPS) * fg_ref[...]).astype(o_ref.dtype)


def _out_stage(x, gate, yf, yb, bonus, z2, ys, lnx_g, lnx_b, s5_d, w_glu, w_out, final_g, e):
    b, l, d = x.shape
    dr = bonus.shape[2]
    ds = ys.shape[0] * LANES
    assert dr == ds
    tm = 512
    tok = lambda w, cb: pl.BlockSpec((1, tm, w), lambda i, j: (i, j, cb))
    const = lambda *shape: pl.BlockSpec(shape, lambda i, j: (0,) * len(shape))
    return pl.pallas_call(
        _out_kernel,
        grid=(b, l // tm),
        in_specs=[tok(d, 0), pl.BlockSpec((1, 1, d), lambda i, j: (i, 0, 0)),
                  tok(dr, 0), tok(dr, 0), tok(dr, 0), tok(dr, 0), tok(ds, 1), tok(ds, 2),
                  pl.BlockSpec((ys.shape[0], 1, tm // S5_CHUNK, S5_CHUNK, LANES), lambda i, j: (0, i, j, 0, 0)),
                  const(1, dr), const(1, dr), const(1, ds), const(ds, 2 * ds), const(dr + ds, d),
                  const(1, d), const(dr, dr)],
        out_specs=tok(d, 0),
        out_shape=jax.ShapeDtypeStruct((b, l, d), x.dtype),
        compiler_params=_params("arbitrary", "arbitrary"),
        name="out_stage",
    )(x, gate, yf, yb, bonus, z2, z2, z2, ys, lnx_g.reshape(1, dr), lnx_b.reshape(1, dr), s5_d.reshape(1, ds),
      w_glu.astype(BF16), w_out.astype(BF16), final_g.reshape(1, d), e)


def kernel(x, c, ctx, c_ctx, norm_g, w_ada, b_ada, w_in, mu_shift, rwkv_w0, rwkv_w2, rwkv_a0, rwkv_a2, rwkv_k_k, rwkv_k_a, rwkv_r_k, lnx_g, lnx_b, s5_lam_re, s5_lam_im, s5_log_dt, s5_b_re, s5_b_im, s5_c_re, s5_c_im, s5_d, s5_w_glu, w_out, final_g):
    assert norm_g.shape[0] == 1, "one layer"
    b, l, d = x.shape
    lc = ctx.shape[1]
    dr = rwkv_k_k.shape[1]
    ds = s5_d.shape[1]
    sw = mu_shift.shape[1]
    assert sw == 3 * dr + 4 * LORA and l % (4 * GRID_W) == 0 and lc % WKV_CHUNK == 0 and b % 8 == 0

    cond = jnp.zeros((2 * b, d), F32).at[:b].set(c).at[b].set(c_ctx)
    m = _modulation(cond, w_ada[0], b_ada[0])
    shift, scale, gate = m[:b, :d], m[:b, d:2 * d], m[:b, 2 * d:]
    cshift = jnp.broadcast_to(m[b, :d], (b, d))
    cscale = jnp.broadcast_to(m[b, d:2 * d], (b, d))

    w1 = w_in[0][:, :sw].astype(BF16)
    w2 = w_in[0][:, sw:].astype(BF16)
    z1c, _, u5c = _inproj(ctx, norm_g[0], cscale[:, None], cshift[:, None], w1, w2)
    z1, z2, u5 = _inproj(x, norm_g[0], scale[:, None], shift[:, None], w1, w2)

    hid = jnp.arange(dr) // HEAD
    e = (hid[:, None] == hid[None, :]).astype(BF16)
    pp = (mu_shift, rwkv_w0[0], rwkv_w2[0], rwkv_a0[0], rwkv_a2[0], rwkv_k_k, rwkv_k_a,
          rwkv_r_k[0].reshape(1, dr), e)
    rc, vc, aac, wc, kdc, bbc, _ = _prep(z1c, pp, grid2d=False)
    rx, vx, aax, wx, kdx, bbx, bonus = _prep(z1, pp, grid2d=True)

    s0 = jnp.zeros((2, b, dr // HEAD, HEAD, HEAD), F32)
    _, _, s_ctx = _wkv(wc, kdc, bbc, aac, rc, vc, s0)
    yf, yb, _ = _wkv(wx, kdx, bbx, aax, rx, vx, s_ctx)

    s5f, s5b = _s5_matrices(s5_lam_re[0], s5_lam_im[0], s5_log_dt[0], s5_b_re[0], s5_b_im[0],
                            s5_c_re[0], s5_c_im[0])
    uc, ux = _s5_rows(u5c), _s5_rows(u5)
    h0 = jnp.zeros((uc.shape[0], 2, b, s5f[3].shape[2]), F32)
    _, hcf = _s5_scan(uc, *s5f, h0, b, False)
    _, hcb = _s5_scan(uc, *s5b, h0, b, True)
    ysf, _ = _s5_scan(ux, *s5f, hcf, b, False)
    ysb, _ = _s5_scan(ux, *s5b, hcb, b, True)
    ys = _s5_unrows(ysf + ysb, b)

    return _out_stage(x, gate[:, None], yf, yb, bonus, z2, ys, lnx_g[0], lnx_b[0], s5_d[0], s5_w_glu[0],
                      w_out[0], final_g, e)
```

```python
import functools
import math

import jax
import jax.numpy as jnp
from jax import lax
from jax.experimental import pallas as pl
from jax.experimental.pallas import tpu as pltpu

F32 = jnp.float32
BF16 = jnp.bfloat16

GRID_W = 64
HEAD = 64
LORA = 64
S5_GROUP = 16
S5_STATE = 64
S5_CHUNK = 16
S5_ROWS = 256
WKV_CHUNK = 64
EPS = 1e-6
LN_X_EPS = 64e-5
EXP_M05 = math.exp(-0.5)

LANES = 128
VMEM_LIMIT = 56 * 1024 * 1024


def _params(*sem):
    return pltpu.CompilerParams(dimension_semantics=sem, vmem_limit_bytes=VMEM_LIMIT)


def _sigmoid(x):
    return 1.0 / (1.0 + jnp.exp(-x))


def _dot_bf16(a, b):
    return jnp.dot(a.astype(BF16), b.astype(BF16), preferred_element_type=F32)


def _dot_f32(a, b):
    return jnp.dot(a, b, preferred_element_type=F32, precision=lax.Precision.HIGHEST)


def _segsum(x, e_ref):
    hi = x.astype(BF16)
    lo = (x - hi.astype(F32)).astype(BF16)
    e = e_ref[...]
    return (jnp.dot(hi, e, preferred_element_type=F32) + jnp.dot(lo, e, preferred_element_type=F32))


def _mod_kernel(c_ref, w_ref, b_ref, o_ref):
    c = c_ref[...]
    o_ref[...] = _dot_f32(c * _sigmoid(c), w_ref[...]) + b_ref[...]


def _modulation(cond, w_ada, b_ada):
    rows, d = cond.shape
    n = w_ada.shape[1]
    tn = 512
    return pl.pallas_call(
        _mod_kernel,
        grid=(n // tn,),
        in_specs=[pl.BlockSpec((rows, d), lambda j: (0, 0)),
                  pl.BlockSpec((d, tn), lambda j: (0, j)),
                  pl.BlockSpec((1, tn), lambda j: (0, j))],
        out_specs=pl.BlockSpec((rows, tn), lambda j: (0, j)),
        out_shape=jax.ShapeDtypeStruct((rows, n), F32),
        compiler_params=_params("arbitrary"),
        name="modulation",
    )(cond, w_ada, b_ada.reshape(1, n))


def _inproj_kernel(x_ref, g_ref, sc_ref, sh_ref, w1_ref, w2_ref, z1_ref, z2_ref, u_ref):
    x = x_ref[0]
    ms = jnp.mean(x * x, axis=-1, keepdims=True)
    h = x * lax.rsqrt(ms + EPS) * g_ref[...]
    h = (h * (1.0 + sc_ref[0]) + sh_ref[0]).astype(BF16)
    z1_ref[0] = jnp.dot(h, w1_ref[...], preferred_element_type=F32)
    z2 = jnp.dot(h, w2_ref[...], preferred_element_type=F32)
    z2_ref[0] = z2
    nblk, _, nch = u_ref.shape[0], u_ref.shape[1], u_ref.shape[2]
    w = z2.shape[1] // 3
    for j in range(nblk):
        u_ref[j, 0] = z2[:, w + LANES * j:w + LANES * (j + 1)].reshape(nch, S5_CHUNK, LANES)


def _inproj(x, norm_g, scale, shift, w1, w2):
    b, l, d = x.shape
    n1, n2 = w1.shape[1], w2.shape[1]
    tm = min(l, 512)
    nblk = n2 // 3 // LANES
    return pl.pallas_call(
        _inproj_kernel,
        grid=(b, l // tm),
        in_specs=[pl.BlockSpec((1, tm, d), lambda i, j: (i, j, 0)),
                  pl.BlockSpec((1, d), lambda i, j: (0, 0)),
                  pl.BlockSpec((1, 1, d), lambda i, j: (i, 0, 0)),
                  pl.BlockSpec((1, 1, d), lambda i, j: (i, 0, 0)),
                  pl.BlockSpec((d, n1), lambda i, j: (0, 0)),
                  pl.BlockSpec((d, n2), lambda i, j: (0, 0))],
        out_specs=[pl.BlockSpec((1, tm, n1), lambda i, j: (i, j, 0)),
                   pl.BlockSpec((1, tm, n2), lambda i, j: (i, j, 0)),
                   pl.BlockSpec((nblk, 1, tm // S5_CHUNK, S5_CHUNK, LANES), lambda i, j: (0, i, j, 0, 0))],
        out_shape=[jax.ShapeDtypeStruct((b, l, n1), F32),
                   jax.ShapeDtypeStruct((b, l, n2), F32),
                   jax.ShapeDtypeStruct((nblk, b, l // S5_CHUNK, S5_CHUNK, LANES), F32)],
        compiler_params=_params("arbitrary", "arbitrary"),
        name="inproj",
    )(x, norm_g.reshape(1, d), scale, shift, w1, w2)


def _prep_body(z, up, down, mu_ref, w0_ref, w2_ref, a0_ref, a2_ref, kk_ref, ka_ref, rk_ref, e_ref, outs,
               grid2d):
    t, sw = z.shape
    dr = (sw - 4 * LORA) // 3
    row = lax.broadcasted_iota(jnp.int32, (t, 1), 0)
    slot = lax.broadcasted_iota(jnp.int32, (1, sw), 1) & 3
    prev = pltpu.roll(z, 1, 0)
    nxt = pltpu.roll(z, t - 1, 0)
    if grid2d:
        col = row & (GRID_W - 1)
        prev = jnp.where(col == 0, 0.0, prev)
        nxt = jnp.where(col == GRID_W - 1, 0.0, nxt)
        sh = jnp.where(slot == 0, prev, jnp.where(slot == 1, nxt, jnp.where(slot == 2, up, down)))
    else:
        prev = jnp.where(row == 0, 0.0, prev)
        nxt = jnp.where(row == t - 1, 0.0, nxt)
        sh = jnp.where((slot & 1) == 0, prev, nxt)
    zs = z + mu_ref[...] * (sh - z)

    r = zs[:, 0:dr]
    k = zs[:, dr:2 * dr]
    v = zs[:, 2 * dr:3 * dr]
    kk = k * kk_ref[...]
    ss = _segsum(kk * kk, e_ref)
    kk = kk / jnp.maximum(jnp.sqrt(ss), 1e-12)
    r_o, v_o, aa_o, w_o, kd_o, bb_o, bonus_o = outs
    r_o[0] = r
    v_o[0] = v
    aa_o[0] = -kk
    ksum = None
    for d in range(2):
        zw = zs[:, 3 * dr + LORA * d:3 * dr + LORA * (d + 1)]
        za = zs[:, 3 * dr + 2 * LORA + LORA * d:3 * dr + 2 * LORA + LORA * (d + 1)]
        wl = w0_ref[d:d + 1, :] + _dot_f32(jnp.tanh(zw), w2_ref[d])
        w_o[d, 0] = -EXP_M05 * _sigmoid(wl)
        asig = _sigmoid(a0_ref[d:d + 1, :] + _dot_f32(za, a2_ref[d]))
        kd = k * (1.0 + (asig - 1.0) * ka_ref[...])
        kd_o[d, 0] = kd
        bb_o[d, 0] = kk * asig
        ksum = kd if ksum is None else ksum + kd
    bonus_o[0] = _segsum(r * (0.5 * ksum) * rk_ref[...], e_ref) * v


def _prep2d_kernel(zc_ref, zu_ref, zd_ref, *rest):
    params, outs = rest[:9], rest[9:]
    j = pl.program_id(1)
    nj = pl.num_programs(1)
    z = zc_ref[0]
    t = z.shape[0]
    row = lax.broadcasted_iota(jnp.int32, (t, 1), 0)
    up = jnp.concatenate([zu_ref[0], z[:t - GRID_W]], axis=0)
    up = jnp.where(jnp.logical_and(j == 0, row < GRID_W), 0.0, up)
    down = jnp.concatenate([z[GRID_W:], zd_ref[0]], axis=0)
    down = jnp.where(jnp.logical_and(j == nj - 1, row >= t - GRID_W), 0.0, down)
    _prep_body(z, up, down, *params, outs, grid2d=True)


def _prep1d_kernel(zc_ref, *rest):
    params, outs = rest[:9], rest[9:]
    _prep_body(zc_ref[0], None, None, *params, outs, grid2d=False)


def _prep(z1, pp, grid2d):
    b, l, sw = z1.shape
    dr = (sw - 4 * LORA) // 3
    tt = 256 if grid2d else l
    nj = l // tt
    rb = tt // GRID_W
    const = lambda *shape: pl.BlockSpec(shape, lambda i, j: (0,) * len(shape))
    p_specs = [const(1, sw), const(2, dr), const(2, LORA, dr), const(2, dr), const(2, LORA, dr),
               const(1, dr), const(1, dr), const(1, dr), const(dr, dr)]
    cur = pl.BlockSpec((1, tt, sw), lambda i, j: (i, j, 0))
    if grid2d:
        nrow = l // GRID_W
        in_specs = [cur,
                    pl.BlockSpec((1, GRID_W, sw), lambda i, j: (i, jnp.maximum(j * rb - 1, 0), 0)),
                    pl.BlockSpec((1, GRID_W, sw), lambda i, j: (i, jnp.minimum((j + 1) * rb, nrow - 1), 0))]
        args = (z1, z1, z1)
        body = _prep2d_kernel
    else:
        in_specs = [cur]
        args = (z1,)
        body = _prep1d_kernel
    o1 = pl.BlockSpec((1, tt, dr), lambda i, j: (i, j, 0))
    o2 = pl.BlockSpec((2, 1, tt, dr), lambda i, j: (0, i, j, 0))
    s1 = jax.ShapeDtypeStruct((b, l, dr), F32)
    s2 = jax.ShapeDtypeStruct((2, b, l, dr), F32)
    return pl.pallas_call(
        body,
        grid=(b, nj),
        in_specs=in_specs + p_specs,
        out_specs=[o1, o1, o1, o2, o2, o2, o1],
        out_shape=[s1, s1, s1, s2, s2, s2, s1],
        compiler_params=_params("arbitrary", "arbitrary"),
        name="prep2d" if grid2d else "prep1d",
    )(*args, *pp)


def _dot_nt(a, b):
    return lax.dot_general(a, b, (((1,), (1,)), ((), ())), preferred_element_type=F32)


def _dot_tn(a, b):
    return lax.dot_general(a, b, (((0,), (0,)), ((), ())), preferred_element_type=F32)


def _wkv_operands(lw_ref, kd_ref, bb_ref, aa_ref, r_ref, v_ref, backward):
    c = aa_ref.shape[1]
    row = lax.broadcasted_iota(jnp.int32, (c, c), 0)
    col = lax.broadcasted_iota(jnp.int32, (c, c), 1)
    strict, incl = (col > row, col >= row) if backward else (col < row, col <= row)
    lw = lw_ref[0, 0]
    lw_hi = lw.astype(BF16)
    lw_lo = (lw - lw_hi.astype(F32)).astype(BF16)
    lc = incl.astype(BF16)
    cum = jnp.dot(lc, lw_hi, preferred_element_type=F32) + jnp.dot(lc, lw_lo, preferred_element_type=F32)
    tot = cum[0:1, :] if backward else cum[c - 1:c, :]
    ones = jnp.ones((c, HEAD), BF16)
    pinv = jnp.exp(-cum)
    pend = jnp.exp(tot - cum)
    bb, kd = bb_ref[0, 0], kd_ref[0, 0]
    rt = r_ref[0] * jnp.exp(cum)
    return dict(
        strict=strict, incl=incl,
        pc=jnp.exp(_dot_tn(lw_hi, ones) + _dot_tn(lw_lo, ones)),
        at=(aa_ref[0] * jnp.exp(cum - lw)).astype(BF16), rt=rt, rt_b=rt.astype(BF16),
        bt=(bb * pinv).astype(BF16), kt=(kd * pinv).astype(BF16),
        be=(bb * pend).astype(BF16), ke=(kd * pend).astype(BF16), v=v_ref[0].astype(BF16))


def _wkv_kernel(lwf_ref, kdf_ref, bbf_ref, aaf_ref, rf_ref, vf_ref, lwb_ref, kdb_ref, bbb_ref, aab_ref, rb_ref,
                vb_ref, s0_ref, yf_ref, yb_ref, sout_ref, s_scr):
    ci = pl.program_id(1)
    c, width = aaf_ref.shape[1], aaf_ref.shape[2]
    nh = width // HEAD

    @pl.when(ci == 0)
    def _():
        s_scr[...] = s0_ref[:, 0]

    ops = (_wkv_operands(lwf_ref, kdf_ref, bbf_ref, aaf_ref, rf_ref, vf_ref, False),
           _wkv_operands(lwb_ref, kdb_ref, bbb_ref, aab_ref, rb_ref, vb_ref, True))
    chains = [(d, h) for h in range(nh) for d in range(2)]
    cut = lambda name: [ops[d][name][:, h * HEAD:(h + 1) * HEAD] for d, h in chains]
    at, rt, rt_b, bt, kt, be, ke, v = (cut(n) for n in ("at", "rt", "rt_b", "bt", "kt", "be", "ke", "v"))
    strict = [ops[d]["strict"] for d, _ in chains]
    incl = [ops[d]["incl"] for d, _ in chains]
    n = range(len(chains))
    mm = lambda a, b: jnp.dot(a, b, preferred_element_type=F32)
    row = lax.broadcasted_iota(jnp.int32, (c, c), 0)
    col = lax.broadcasted_iota(jnp.int32, (c, c), 1)
    eye = (row == col).astype(F32)

    g = [_dot_nt(jnp.concatenate([at[i], rt_b[i]], axis=0), jnp.concatenate([bt[i], kt[i]], axis=0)) for i in n]
    x = [jnp.where(strict[i], g[i][:c, :c], 0.0) for i in n]
    a_ak = [jnp.where(strict[i], g[i][:c, c:], 0.0).astype(BF16) for i in n]
    m_rb = [jnp.where(incl[i], g[i][c:, :c], 0.0).astype(BF16) for i in n]
    m_rk = [jnp.where(incl[i], g[i][c:, c:], 0.0).astype(BF16) for i in n]
    t = [eye + x[i] for i in n]
    pw = [x[i].astype(BF16) for i in n]
    for _ in range((c - 1).bit_length() - 1):
        pw = [mm(pw[i], pw[i]).astype(BF16) for i in n]
        t = [t[i] + mm(t[i].astype(BF16), pw[i]) for i in n]
    z1 = [mm(a_ak[i], v[i]).astype(BF16) for i in n]
    za = [mm(t[i].astype(BF16), jnp.concatenate([at[i], z1[i]], axis=1)).astype(BF16) for i in n]
    w2 = [mm(m_rb[i], za[i]) for i in n]
    yp = [w2[i][:, HEAD:] + mm(m_rk[i], v[i]) for i in n]
    gh = [_dot_tn(be[i], za[i]) for i in n]
    hh = [gh[i][:, HEAD:] + _dot_tn(ke[i], v[i]) for i in n]
    s = [s_scr[d, h] for d, h in chains]
    s_b = [s[i].astype(BF16) for i in n]
    y = [mm((rt[i] + w2[i][:, :HEAD]).astype(BF16), s_b[i]) + yp[i] for i in n]
    for i, (d, h) in enumerate(chains):
        s_scr[d, h] = (ops[d]["pc"][h * HEAD:(h + 1) * HEAD, :] * s[i]
                       + mm(gh[i][:, :HEAD].astype(BF16), s_b[i]) + hh[i])
    yf_ref[0] = jnp.concatenate([y[i] for i, (d, _) in enumerate(chains) if d == 0], axis=1)
    yb_ref[0] = jnp.concatenate([y[i] for i, (d, _) in enumerate(chains) if d == 1], axis=1)

    @pl.when(ci == pl.num_programs(1) - 1)
    def _():
        sout_ref[:, 0] = s_scr[...]


def _wkv(lw, kd, bb, aa, r, v, s0):
    _, b, l, width = lw.shape
    nh = width // HEAD
    c = WKV_CHUNK
    nc = l // c
    dirs = lambda d: pl.BlockSpec((1, 1, c, width), lambda i, j: (d, i, (nc - 1 - j) if d else j, 0))
    both = lambda d: pl.BlockSpec((1, c, width), lambda i, j: (i, (nc - 1 - j) if d else j, 0))
    stspec = pl.BlockSpec((2, 1, nh, HEAD, HEAD), lambda i, j: (0, i, 0, 0, 0))
    return pl.pallas_call(
        _wkv_kernel,
        grid=(b, nc),
        in_specs=[dirs(0), dirs(0), dirs(0), both(0), both(0), both(0),
                  dirs(1), dirs(1), dirs(1), both(1), both(1), both(1), stspec],
        out_specs=[both(0), both(1), stspec],
        out_shape=[jax.ShapeDtypeStruct((b, l, width), F32), jax.ShapeDtypeStruct((b, l, width), F32),
                   jax.ShapeDtypeStruct((2, b, nh, HEAD, HEAD), F32)],
        scratch_shapes=[pltpu.VMEM((2, nh, HEAD, HEAD), F32)],
        compiler_params=_params("arbitrary", "arbitrary"),
        name="wkv",
    )(lw, kd, bb, aa, r, v, lw, kd, bb, aa, r, v, s0)


def _s5_kernel(*refs, nb, reverse, toeplitz):
    if toeplitz:
        (u_ref, lag_ref, win_ref, wout_ref, a_ref, h0_ref, y_ref, hout_ref,
         x_scr, hh_scr, hre_scr, him_scr, wt_scr) = refs
    else:
        u_ref, win_ref, wout_ref, a_ref, h0_ref, y_ref, hout_ref, x_scr, hh_scr, hre_scr, him_scr = refs
    i = pl.program_id(1)
    ns = hre_scr.shape[1]

    @pl.when(i == 0)
    def _():
        hre_scr[...] = h0_ref[0, 0]
        him_scr[...] = h0_ref[0, 1]
        if toeplitz:
            t = (lag_ref.shape[1] + 1) // 2
            for sig in range(t):
                for tau in range(t):
                    wt_scr[sig * LANES:(sig + 1) * LANES, tau * LANES:(tau + 1) * LANES] = lag_ref[0, tau - sig + t - 1]

    u = u_ref[0].astype(BF16)
    x_scr[...] = jnp.dot(u, win_ref[0], preferred_element_type=F32)
    ar, ai = a_ref[0, 0:1, :], a_ref[0, 1:2, :]
    nch = x_scr.shape[0] // nb

    def body(k, carry):
        hr, hi = carry
        c = nch - 1 - k if reverse else k
        rows = pl.ds(pl.multiple_of(c * nb, nb), nb)
        hh_scr[rows, 0:ns] = hr
        hh_scr[rows, ns:2 * ns] = hi
        return (hr * ar - hi * ai + x_scr[rows, 0:ns], hr * ai + hi * ar + x_scr[rows, ns:2 * ns])

    hr, hi = lax.fori_loop(0, nch, body, (hre_scr[...], him_scr[...]))
    hre_scr[...] = hr
    him_scr[...] = hi
    y = _dot_nt(hh_scr[...].astype(BF16), wout_ref[0])
    if toeplitz:
        y = y + jnp.dot(u, wt_scr[...], preferred_element_type=F32)
    y_ref[0] = y

    @pl.when(i == pl.num_programs(1) - 1)
    def _():
        hout_ref[0, 0] = hr
        hout_ref[0, 1] = hi


def _s5_scan(u, lagblk, win, wout, acoef, h0, nb, reverse):
    nblk, rows, kw = u.shape
    ns = win.shape[2] // 2
    rt = min(rows, S5_ROWS)
    nt = rows // rt
    tile = pl.BlockSpec((1, rt, kw), lambda g, i: (g, (nt - 1 - i) if reverse else i, 0))
    per_blk = lambda a: pl.BlockSpec((1,) + a.shape[1:], lambda g, i: (g,) + (0,) * (a.ndim - 1))
    weights = ([lagblk] if lagblk is not None else []) + [win, wout, acoef, h0]
    return pl.pallas_call(
        functools.partial(_s5_kernel, nb=nb, reverse=reverse, toeplitz=lagblk is not None),
        grid=(nblk, nt),
        in_specs=[tile] + [per_blk(a) for a in weights],
        out_specs=[tile, per_blk(h0)],
        out_shape=[jax.ShapeDtypeStruct((nblk, rows, kw), F32), jax.ShapeDtypeStruct(h0.shape, F32)],
        scratch_shapes=[pltpu.VMEM((rt, 2 * ns), F32), pltpu.VMEM((rt, 2 * ns), F32),
                        pltpu.VMEM((nb, ns), F32), pltpu.VMEM((nb, ns), F32)]
        + ([pltpu.VMEM((kw, kw), BF16)] if lagblk is not None else []),
        compiler_params=_params("arbitrary", "arbitrary"),
        name="s5_bwd" if reverse else "s5_fwd",
    )(u, *weights)


def _s5_matrices(lam_re, lam_im, log_dt, b_re, b_im, c_re, c_im):
    hp = lax.Precision.HIGHEST
    t = S5_CHUNK
    dt = jnp.exp(log_dt.astype(F32))[..., None]
    lr, li = lam_re.astype(F32), lam_im.astype(F32)
    j = jnp.arange(t + 1, dtype=F32)[:, None, None, None]
    mag = jnp.exp(j * dt * lr)
    pr, pi = mag * jnp.cos(j * dt * li), mag * jnp.sin(j * dt * li)
    ar, ai = pr[1], pi[1]
    den = lr * lr + li * li
    fr = ((ar - 1.0) * lr + ai * li) / den
    fi = (ai * lr - (ar - 1.0) * li) / den
    bbr = fr[..., None] * b_re - fi[..., None] * b_im
    bbi = fr[..., None] * b_im + fi[..., None] * b_re
    car = c_re[None, None] * pr[:, :, :, None, :] - c_im[None, None] * pi[:, :, :, None, :]
    cai = c_re[None, None] * pi[:, :, :, None, :] + c_im[None, None] * pr[:, :, :, None, :]
    kj = (jnp.einsum('jdgcp,dgpe->jdgce', car, bbr, precision=hp)
          - jnp.einsum('jdgcp,dgpe->jdgce', cai, bbi, precision=hp))
    g = lr.shape[1]
    gpb = LANES // S5_GROUP
    nblk = g // gpb
    ns = gpb * S5_STATE
    own_chan = (jnp.arange(gpb)[:, None, None] == (jnp.arange(LANES) // S5_GROUP)[None, None, :])
    own_state = (jnp.arange(gpb)[:, None, None] == (jnp.arange(ns) // S5_STATE)[None, None, :])

    kl = jnp.concatenate([kj[t - 1:0:-1, 1], (kj[0, 0] + kj[0, 1])[None], kj[1:t, 0]], axis=0)
    kl = kl.reshape(2 * t - 1, nblk, gpb, S5_GROUP, S5_GROUP).transpose(0, 1, 4, 2, 3)
    kl = kl.reshape(2 * t - 1, nblk, 1, S5_GROUP, LANES)
    lagblk = jnp.where(own_chan, kl, 0.0).transpose(1, 0, 2, 3, 4).reshape(nblk, 2 * t - 1, LANES, LANES)

    def blockdiag_rows(q):
        return jnp.where(own_state, q[:, :, None], 0.0).reshape(nblk, t * LANES, ns)

    def state_in(d, pw):
        prs = pr[pw, d].reshape(t, nblk, 1, ns).transpose(1, 0, 2, 3)
        pis = pi[pw, d].reshape(t, nblk, 1, ns).transpose(1, 0, 2, 3)
        lay = lambda q: q.reshape(nblk, gpb, S5_STATE, S5_GROUP).transpose(0, 3, 1, 2).reshape(nblk, 1, S5_GROUP, ns)
        br, bi = lay(bbr[d]), lay(bbi[d])
        return jnp.concatenate([blockdiag_rows(prs * br - pis * bi), blockdiag_rows(prs * bi + pis * br)],
                               axis=-1).astype(BF16)

    def state_out(d, pw):
        lay = lambda q: q.reshape(t, nblk, gpb, S5_GROUP, S5_STATE).transpose(1, 0, 3, 2, 4).reshape(
            nblk, t, S5_GROUP, ns)
        return jnp.concatenate([blockdiag_rows(lay(car[pw, d])), blockdiag_rows(lay(-cai[pw, d]))],
                               axis=-1).astype(BF16)

    coef = lambda d: jnp.stack([pr[t, d].reshape(nblk, ns), pi[t, d].reshape(nblk, ns)], axis=1)
    fwd = (lagblk.astype(BF16), state_in(0, t - 1 - jnp.arange(t)), state_out(0, jnp.arange(t) + 1), coef(0))
    bwd = (None, state_in(1, jnp.arange(t)), state_out(1, t - jnp.arange(t)), coef(1))
    return fwd, bwd


def _s5_rows(u5):
    nblk, b, nc, t, w = u5.shape
    return u5.transpose(0, 2, 1, 3, 4).reshape(nblk, nc * b, t * w)


def _s5_unrows(y, b):
    nblk, rows, kw = y.shape
    return y.reshape(nblk, rows // b, b, S5_CHUNK, kw // S5_CHUNK).transpose(0, 2, 1, 3, 4)


def _out_kernel(x_ref, gate_ref, yf_ref, yb_ref, bonus_ref, gr_ref, u_ref, gs_ref, ys_ref, lg_ref, lb_ref, d_ref,
                wg_ref, wo_ref, fg_ref, e_ref, o_ref):
    dr = bonus_ref.shape[2]
    y = yf_ref[0] + yb_ref[0]
    mean = _segsum(y, e_ref) * (1.0 / HEAD)
    dev = y - mean
    var = _segsum(dev * dev, e_ref) * (1.0 / HEAD)
    yn = dev * lax.rsqrt(var + LN_X_EPS) * lg_ref[...] + lb_ref[...]
    gr = gr_ref[0]
    y_r = (yn + bonus_ref[0]) * (gr * _sigmoid(gr))

    tm = u_ref.shape[1]
    ys = jnp.concatenate([ys_ref[j, 0].reshape(tm, LANES) for j in range(ys_ref.shape[0])], axis=1)
    s = ys + d_ref[...] * u_ref[0]
    s = 0.5 * s * (1.0 + jnp.tanh(math.sqrt(2.0 / math.pi) * (s + 0.044715 * (s * s * s))))
    gl = jnp.dot(s.astype(BF16), wg_ref[...], preferred_element_type=F32)
    ds = gl.shape[1] // 2
    gs = gs_ref[0]
    y_s = gl[:, :ds] * _sigmoid(gl[:, ds:]) * (gs * _sigmoid(gs))

    out = (jnp.dot(y_r.astype(BF16), wo_ref[0:dr, :], preferred_element_type=F32)
           + jnp.dot(y_s.astype(BF16), wo_ref[dr:, :], preferred_element_type=F32))
    xo = x_ref[0] + gate_ref[0] * out
    ms = jnp.mean(xo * xo, axis=-1, keepdims=True)
    o_ref[0] = (xo * lax.rsqrt(ms + EPS) * fg_ref[...]).astype(o_ref.dtype)


def _out_stage(x, gate, yf, yb, bonus, z2, ys, lnx_g, lnx_b, s5_d, w_glu, w_out, final_g, e):
    b, l, d = x.shape
    dr = bonus.shape[2]
    ds = ys.shape[0] * LANES
    assert dr == ds
    tm = 512
    tok = lambda w, cb: pl.BlockSpec((1, tm, w), lambda i, j: (i, j, cb))
    const = lambda *shape: pl.BlockSpec(shape, lambda i, j: (0,) * len(shape))
    return pl.pallas_call(
        _out_kernel,
        grid=(b, l // tm),
        in_specs=[tok(d, 0), pl.BlockSpec((1, 1, d), lambda i, j: (i, 0, 0)),
                  tok(dr, 0), tok(dr, 0), tok(dr, 0), tok(dr, 0), tok(ds, 1), tok(ds, 2),
                  pl.BlockSpec((ys.shape[0], 1, tm // S5_CHUNK, S5_CHUNK, LANES), lambda i, j: (0, i, j, 0, 0)),
                  const(1, dr), const(1, dr), const(1, ds), const(ds, 2 * ds), const(dr + ds, d),
                  const(1, d), const(dr, dr)],
        out_specs=tok(d, 0),
        out_shape=jax.ShapeDtypeStruct((b, l, d), x.dtype),
        compiler_params=_params("arbitrary", "arbitrary"),
        name="out_stage",
    )(x, gate, yf, yb, bonus, z2, z2, z2, ys, lnx_g.reshape(1, dr), lnx_b.reshape(1, dr), s5_d.reshape(1, ds),
      w_glu.astype(BF16), w_out.astype(BF16), final_g.reshape(1, d), e)


def kernel(x, c, ctx, c_ctx, norm_g, w_ada, b_ada, w_in, mu_shift, rwkv_w0, rwkv_w2, rwkv_a0, rwkv_a2, rwkv_k_k, rwkv_k_a, rwkv_r_k, lnx_g, lnx_b, s5_lam_re, s5_lam_im, s5_log_dt, s5_b_re, s5_b_im, s5_c_re, s5_c_im, s5_d, s5_w_glu, w_out, final_g):
    assert norm_g.shape[0] == 1, "one layer"
    b, l, d = x.shape
    lc = ctx.shape[1]
    dr = rwkv_k_k.shape[1]
    ds = s5_d.shape[1]
    sw = mu_shift.shape[1]
    assert sw == 3 * dr + 4 * LORA and l % (4 * GRID_W) == 0 and lc % WKV_CHUNK == 0 and b % 8 == 0

    cond = jnp.zeros((2 * b, d), F32).at[:b].set(c).at[b].set(c_ctx)
    m = _modulation(cond, w_ada[0], b_ada[0])
    shift, scale, gate = m[:b, :d], m[:b, d:2 * d], m[:b, 2 * d:]
    cshift = jnp.broadcast_to(m[b, :d], (b, d))
    cscale = jnp.broadcast_to(m[b, d:2 * d], (b, d))

    w1 = w_in[0][:, :sw].astype(BF16)
    w2 = w_in[0][:, sw:].astype(BF16)
    z1c, _, u5c = _inproj(ctx, norm_g[0], cscale[:, None], cshift[:, None], w1, w2)
    z1, z2, u5 = _inproj(x, norm_g[0], scale[:, None], shift[:, None], w1, w2)

    hid = jnp.arange(dr) // HEAD
    e = (hid[:, None] == hid[None, :]).astype(BF16)
    pp = (mu_shift, rwkv_w0[0], rwkv_w2[0], rwkv_a0[0], rwkv_a2[0], rwkv_k_k, rwkv_k_a,
          rwkv_r_k[0].reshape(1, dr), e)
    rc, vc, aac, wc, kdc, bbc, _ = _prep(z1c, pp, grid2d=False)
    rx, vx, aax, wx, kdx, bbx, bonus = _prep(z1, pp, grid2d=True)

    s0 = jnp.zeros((2, b, dr // HEAD, HEAD, HEAD), F32)
    _, _, s_ctx = _wkv(wc, kdc, bbc, aac, rc, vc, s0)
    yf, yb, _ = _wkv(wx, kdx, bbx, aax, rx, vx, s_ctx)

    s5f, s5b = _s5_matrices(s5_lam_re[0], s5_lam_im[0], s5_log_dt[0], s5_b_re[0], s5_b_im[0],
                            s5_c_re[0], s5_c_im[0])
    uc, ux = _s5_rows(u5c), _s5_rows(u5)
    h0 = jnp.zeros((uc.shape[0], 2, b, s5f[3].shape[2]), F32)
    _, hcf = _s5_scan(uc, *s5f, h0, b, False)
    _, hcb = _s5_scan(uc, *s5b, h0, b, True)
    ysf, _ = _s5_scan(ux, *s5f, hcf, b, False)
    ysb, _ = _s5_scan(ux, *s5b, hcb, b, True)
    ys = _s5_unrows(ysf + ysb, b)

    return _out_stage(x, gate[:, None], yf, yb, bonus, z2, ys, lnx_g[0], lnx_b[0], s5_d[0], s5_w_glu[0],
                      w_out[0], final_g, e)
```

```python
import functools
import math

import jax
import jax.numpy as jnp
from jax import lax
from jax.experimental import pallas as pl
from jax.experimental.pallas import tpu as pltpu

F32 = jnp.float32
BF16 = jnp.bfloat16

GRID_W = 64
HEAD = 64
LORA = 64
S5_GROUP = 16
S5_STATE = 64
S5_CHUNK = 16
S5_ROWS = 256
WKV_CHUNK = 64
WKV_ROWS = 2
WKV_PACK = 2
EPS = 1e-6
LN_X_EPS = 64e-5
EXP_M05 = math.exp(-0.5)

LANES = 128
VMEM_LIMIT = 56 * 1024 * 1024


def _params(*sem):
    return pltpu.CompilerParams(dimension_semantics=sem, vmem_limit_bytes=VMEM_LIMIT)


def _sigmoid(x):
    return 1.0 / (1.0 + jnp.exp(-x))


def _dot_bf16(a, b):
    return jnp.dot(a.astype(BF16), b.astype(BF16), preferred_element_type=F32)


def _dot_f32(a, b):
    return jnp.dot(a, b, preferred_element_type=F32, precision=lax.Precision.HIGHEST)


def _segsum(x, e_ref):
    hi = x.astype(BF16)
    lo = (x - hi.astype(F32)).astype(BF16)
    e = e_ref[...]
    return (jnp.dot(hi, e, preferred_element_type=F32) + jnp.dot(lo, e, preferred_element_type=F32))


def _mod_kernel(c_ref, w_ref, b_ref, o_ref):
    c = c_ref[...]
    o_ref[...] = _dot_f32(c * _sigmoid(c), w_ref[...]) + b_ref[...]


def _modulation(cond, w_ada, b_ada):
    rows, d = cond.shape
    n = w_ada.shape[1]
    tn = 512
    return pl.pallas_call(
        _mod_kernel,
        grid=(n // tn,),
        in_specs=[pl.BlockSpec((rows, d), lambda j: (0, 0)),
                  pl.BlockSpec((d, tn), lambda j: (0, j)),
                  pl.BlockSpec((1, tn), lambda j: (0, j))],
        out_specs=pl.BlockSpec((rows, tn), lambda j: (0, j)),
        out_shape=jax.ShapeDtypeStruct((rows, n), F32),
        compiler_params=_params("arbitrary"),
        name="modulation",
    )(cond, w_ada, b_ada.reshape(1, n))


def _inproj_kernel(x_ref, g_ref, sc_ref, sh_ref, w1_ref, w2_ref, z1_ref, z2_ref, u_ref):
    x = x_ref[0]
    ms = jnp.mean(x * x, axis=-1, keepdims=True)
    h = x * lax.rsqrt(ms + EPS) * g_ref[...]
    h = (h * (1.0 + sc_ref[0]) + sh_ref[0]).astype(BF16)
    z1_ref[0] = jnp.dot(h, w1_ref[...], preferred_element_type=F32)
    z2 = jnp.dot(h, w2_ref[...], preferred_element_type=F32)
    z2_ref[0] = z2
    nblk, _, nch = u_ref.shape[0], u_ref.shape[1], u_ref.shape[2]
    w = z2.shape[1] // 3
    for j in range(nblk):
        u_ref[j, 0] = z2[:, w + LANES * j:w + LANES * (j + 1)].reshape(nch, S5_CHUNK, LANES)


def _inproj(x, norm_g, scale, shift, w1, w2):
    b, l, d = x.shape
    n1, n2 = w1.shape[1], w2.shape[1]
    tm = min(l, 512)
    nblk = n2 // 3 // LANES
    return pl.pallas_call(
        _inproj_kernel,
        grid=(b, l // tm),
        in_specs=[pl.BlockSpec((1, tm, d), lambda i, j: (i, j, 0)),
                  pl.BlockSpec((1, d), lambda i, j: (0, 0)),
                  pl.BlockSpec((1, 1, d), lambda i, j: (i, 0, 0)),
                  pl.BlockSpec((1, 1, d), lambda i, j: (i, 0, 0)),
                  pl.BlockSpec((d, n1), lambda i, j: (0, 0)),
                  pl.BlockSpec((d, n2), lambda i, j: (0, 0))],
        out_specs=[pl.BlockSpec((1, tm, n1), lambda i, j: (i, j, 0)),
                   pl.BlockSpec((1, tm, n2), lambda i, j: (i, j, 0)),
                   pl.BlockSpec((nblk, 1, tm // S5_CHUNK, S5_CHUNK, LANES), lambda i, j: (0, i, j, 0, 0))],
        out_shape=[jax.ShapeDtypeStruct((b, l, n1), F32),
                   jax.ShapeDtypeStruct((b, l, n2), F32),
                   jax.ShapeDtypeStruct((nblk, b, l // S5_CHUNK, S5_CHUNK, LANES), F32)],
        compiler_params=_params("arbitrary", "arbitrary"),
        name="inproj",
    )(x, norm_g.reshape(1, d), scale, shift, w1, w2)


def _prep_body(z, up, down, mu_ref, w0_ref, w2_ref, a0_ref, a2_ref, kk_ref, ka_ref, rk_ref, e_ref, outs,
               grid2d):
    t, sw = z.shape
    dr = (sw - 4 * LORA) // 3
    row = lax.broadcasted_iota(jnp.int32, (t, 1), 0)
    slot = lax.broadcasted_iota(jnp.int32, (1, sw), 1) & 3
    prev = pltpu.roll(z, 1, 0)
    nxt = pltpu.roll(z, t - 1, 0)
    if grid2d:
        col = row & (GRID_W - 1)
        prev = jnp.where(col == 0, 0.0, prev)
        nxt = jnp.where(col == GRID_W - 1, 0.0, nxt)
        sh = jnp.where(slot == 0, prev, jnp.where(slot == 1, nxt, jnp.where(slot == 2, up, down)))
    else:
        prev = jnp.where(row == 0, 0.0, prev)
        nxt = jnp.where(row == t - 1, 0.0, nxt)
        sh = jnp.where((slot & 1) == 0, prev, nxt)
    zs = z + mu_ref[...] * (sh - z)

    r = zs[:, 0:dr]
    k = zs[:, dr:2 * dr]
    v = zs[:, 2 * dr:3 * dr]
    kk = k * kk_ref[...]
    ss = _segsum(kk * kk, e_ref)
    kk = kk / jnp.maximum(jnp.sqrt(ss), 1e-12)
    r_o, v_o, aa_o, w_o, kd_o, bb_o, bonus_o = outs
    r_o[0] = r
    v_o[0] = v
    aa_o[0] = -kk
    ksum = None
    for d in range(2):
        zw = zs[:, 3 * dr + LORA * d:3 * dr + LORA * (d + 1)]
        za = zs[:, 3 * dr + 2 * LORA + LORA * d:3 * dr + 2 * LORA + LORA * (d + 1)]
        wl = w0_ref[d:d + 1, :] + _dot_f32(jnp.tanh(zw), w2_ref[d])
        w_o[d, 0] = -EXP_M05 * _sigmoid(wl)
        asig = _sigmoid(a0_ref[d:d + 1, :] + _dot_f32(za, a2_ref[d]))
        kd = k * (1.0 + (asig - 1.0) * ka_ref[...])
        kd_o[d, 0] = kd
        bb_o[d, 0] = kk * asig
        ksum = kd if ksum is None else ksum + kd
    bonus_o[0] = _segsum(r * (0.5 * ksum) * rk_ref[...], e_ref) * v


def _prep2d_kernel(zc_ref, zu_ref, zd_ref, *rest):
    params, outs = rest[:9], rest[9:]
    j = pl.program_id(1)
    nj = pl.num_programs(1)
    z = zc_ref[0]
    t = z.shape[0]
    row = lax.broadcasted_iota(jnp.int32, (t, 1), 0)
    up = jnp.concatenate([zu_ref[0], z[:t - GRID_W]], axis=0)
    up = jnp.where(jnp.logical_and(j == 0, row < GRID_W), 0.0, up)
    down = jnp.concatenate([z[GRID_W:], zd_ref[0]], axis=0)
    down = jnp.where(jnp.logical_and(j == nj - 1, row >= t - GRID_W), 0.0, down)
    _prep_body(z, up, down, *params, outs, grid2d=True)


def _prep1d_kernel(zc_ref, *rest):
    params, outs = rest[:9], rest[9:]
    _prep_body(zc_ref[0], None, None, *params, outs, grid2d=False)


def _prep(z1, pp, grid2d):
    b, l, sw = z1.shape
    dr = (sw - 4 * LORA) // 3
    tt = 256 if grid2d else l
    nj = l // tt
    rb = tt // GRID_W
    const = lambda *shape: pl.BlockSpec(shape, lambda i, j: (0,) * len(shape))
    p_specs = [const(1, sw), const(2, dr), const(2, LORA, dr), const(2, dr), const(2, LORA, dr),
               const(1, dr), const(1, dr), const(1, dr), const(dr, dr)]
    cur = pl.BlockSpec((1, tt, sw), lambda i, j: (i, j, 0))
    if grid2d:
        nrow = l // GRID_W
        in_specs = [cur,
                    pl.BlockSpec((1, GRID_W, sw), lambda i, j: (i, jnp.maximum(j * rb - 1, 0), 0)),
                    pl.BlockSpec((1, GRID_W, sw), lambda i, j: (i, jnp.minimum((j + 1) * rb, nrow - 1), 0))]
        args = (z1, z1, z1)
        body = _prep2d_kernel
    else:
        in_specs = [cur]
        args = (z1,)
        body = _prep1d_kernel
    o1 = pl.BlockSpec((1, tt, dr), lambda i, j: (i, j, 0))
    o2 = pl.BlockSpec((2, 1, tt, dr), lambda i, j: (0, i, j, 0))
    s1 = jax.ShapeDtypeStruct((b, l, dr), F32)
    s2 = jax.ShapeDtypeStruct((2, b, l, dr), F32)
    return pl.pallas_call(
        body,
        grid=(b, nj),
        in_specs=in_specs + p_specs,
        out_specs=[o1, o1, o1, o2, o2, o2, o1],
        out_shape=[s1, s1, s1, s2, s2, s2, s1],
        compiler_params=_params("arbitrary", "arbitrary"),
        name="prep2d" if grid2d else "prep1d",
    )(*args, *pp)


def _dot_nt(a, b):
    return lax.dot_general(a, b, (((1,), (1,)), ((), ())), preferred_element_type=F32)


def _dot_tn(a, b):
    return lax.dot_general(a, b, (((0,), (0,)), ((), ())), preferred_element_type=F32)


def _wkv_operands(lw_ref, kd_ref, bb_ref, aa_ref, r_ref, v_ref, bi, backward):
    c = aa_ref.shape[1]
    row = lax.broadcasted_iota(jnp.int32, (c, c), 0)
    col = lax.broadcasted_iota(jnp.int32, (c, c), 1)
    lw = lw_ref[0, bi]
    lw_hi = lw.astype(BF16)
    lw_lo = (lw - lw_hi.astype(F32)).astype(BF16)
    lc = (col >= row if backward else col <= row).astype(BF16)
    cum = jnp.dot(lc, lw_hi, preferred_element_type=F32) + jnp.dot(lc, lw_lo, preferred_element_type=F32)
    tot = cum[0:1, :] if backward else cum[c - 1:c, :]
    pinv = jnp.exp(-cum)
    pend = jnp.exp(tot - cum)
    bb, kd = bb_ref[0, bi], kd_ref[0, bi]
    rt = r_ref[bi] * jnp.exp(cum)
    return dict(
        lw_hi=lw_hi, lw_lo=lw_lo, at=(aa_ref[bi] * jnp.exp(cum - lw)).astype(BF16), rt=rt, rt_b=rt.astype(BF16),
        bt=(bb * pinv).astype(BF16), kt=(kd * pinv).astype(BF16),
        be=(bb * pend).astype(BF16), ke=(kd * pend).astype(BF16), v=v_ref[bi].astype(BF16))


def _wkv_kernel(lwf_ref, kdf_ref, bbf_ref, aaf_ref, rf_ref, vf_ref, lwb_ref, kdb_ref, bbb_ref, aab_ref, rb_ref,
                vb_ref, s0_ref, yf_ref, yb_ref, sout_ref, s_scr):
    ci = pl.program_id(1)
    nb, c, width = aaf_ref.shape
    gw = WKV_PACK * HEAD
    ng = width // gw
    assert c == HEAD and width % gw == 0

    @pl.when(ci == 0)
    def _():
        s_scr[...] = s0_ref[...]

    ops = {(0, bi): _wkv_operands(lwf_ref, kdf_ref, bbf_ref, aaf_ref, rf_ref, vf_ref, bi, False) for bi in range(nb)}
    ops.update({(1, bi): _wkv_operands(lwb_ref, kdb_ref, bbb_ref, aab_ref, rb_ref, vb_ref, bi, True)
                for bi in range(nb)})
    chains = [(d, bi, gi) for gi in range(ng) for bi in range(nb) for d in range(2)]
    n = range(len(chains))
    cut = lambda name: [ops[d, bi][name][:, gi * gw:(gi + 1) * gw] for d, bi, gi in chains]
    at, rt, rt_b, bt, kt, be, ke, v, lw_hi, lw_lo = (
        cut(k) for k in ("at", "rt", "rt_b", "bt", "kt", "be", "ke", "v", "lw_hi", "lw_lo"))
    mm = lambda a, b: jnp.dot(a, b, preferred_element_type=F32)

    r64 = lax.broadcasted_iota(jnp.int32, (c, gw), 0)
    c64 = lax.broadcasted_iota(jnp.int32, (c, gw), 1) & (HEAD - 1)
    eye_f = (r64 == c64).astype(F32)
    causal = {0: (c64 < r64, c64 <= r64), 1: (c64 > r64, c64 >= r64)}
    strict = [causal[d][0] for d, _, _ in chains]
    incl = [causal[d][1] for d, _, _ in chains]
    same_head = (lax.broadcasted_iota(jnp.int32, (gw, gw), 0) // HEAD
                 == lax.broadcasted_iota(jnp.int32, (gw, gw), 1) // HEAD)

    def bd(xp):
        return jnp.where(same_head, jnp.concatenate([xp] * (gw // c), axis=0), jnp.zeros((), xp.dtype))

    def diag(o):
        head = lax.broadcasted_iota(jnp.int32, (c, gw), 1) // HEAD
        out = o[:c]
        for h in range(1, gw // c):
            out = jnp.where(head == h, o[h * c:(h + 1) * c], out)
        return out

    rows = lambda a, b: jnp.concatenate([a, b], axis=0)
    cols = lambda a, b: jnp.concatenate([a, b], axis=1)
    g = [_dot_nt(rows(at[i], rt_b[i]), rows(bd(bt[i]), bd(kt[i]))) for i in n]
    x = [jnp.where(strict[i], g[i][:c, :gw], 0.0) for i in n]
    a_ak = [jnp.where(strict[i], g[i][:c, gw:], 0.0).astype(BF16) for i in n]
    m_rb = [jnp.where(incl[i], g[i][c:, :gw], 0.0).astype(BF16) for i in n]
    m_rk = [jnp.where(incl[i], g[i][c:, gw:], 0.0).astype(BF16) for i in n]
    t = [eye_f + x[i] for i in n]
    p = [x[i].astype(BF16) for i in n]
    p = [mm(p[i], bd(p[i])).astype(BF16) for i in n]
    for _ in range((c - 1).bit_length() - 2):
        o = [mm(rows(p[i], t[i].astype(BF16)), bd(p[i])) for i in n]
        p = [o[i][:c].astype(BF16) for i in n]
        t = [t[i] + o[i][c:] for i in n]
    t = [t[i] + mm(t[i].astype(BF16), bd(p[i])) for i in n]
    ov = [mm(rows(a_ak[i], m_rk[i]), bd(v[i])) for i in n]
    za = [mm(t[i].astype(BF16), cols(bd(at[i]), bd(ov[i][:c].astype(BF16)))).astype(BF16) for i in n]
    w2 = [mm(m_rb[i], cols(bd(za[i][:, :gw]), bd(za[i][:, gw:]))) for i in n]
    yp = [w2[i][:, gw:] + ov[i][c:] for i in n]
    tn = [_dot_tn(rows(be[i], ke[i]), rows(za[i], cols(jnp.zeros((c, gw), BF16), v[i]))) for i in n]
    gd = [diag(tn[i][:, :gw]) for i in n]
    hv = [diag(tn[i][:, gw:]) for i in n]
    ones = jnp.ones((c, gw), BF16)
    pc = [jnp.exp(diag(_dot_tn(lw_hi[i], ones) + _dot_tn(lw_lo[i], ones))) for i in n]
    s = [s_scr[d, bi, gi] for d, bi, gi in chains]
    os_ = [mm(rows((rt[i] + w2[i][:, :gw]).astype(BF16), gd[i].astype(BF16)), bd(s[i].astype(BF16))) for i in n]
    y = [os_[i][:c] + yp[i] for i in n]
    for i, (d, bi, gi) in enumerate(chains):
        s_scr[d, bi, gi] = pc[i] * s[i] + os_[i][c:] + hv[i]
    for d, y_ref in enumerate((yf_ref, yb_ref)):
        for bi in range(nb):
            y_ref[bi] = jnp.concatenate([y[chains.index((d, bi, gi))] for gi in range(ng)], axis=1)

    @pl.when(ci == pl.num_programs(1) - 1)
    def _():
        sout_ref[...] = s_scr[...]


def _wkv(lw, kd, bb, aa, r, v, s0):
    _, b, l, width = lw.shape
    c = WKV_CHUNK
    nc = l // c
    nb = WKV_ROWS
    dirs = lambda d: pl.BlockSpec((1, nb, c, width), lambda i, j: (d, i, (nc - 1 - j) if d else j, 0))
    both = lambda d: pl.BlockSpec((nb, c, width), lambda i, j: (i, (nc - 1 - j) if d else j, 0))
    stspec = pl.BlockSpec((2, nb) + s0.shape[2:], lambda i, j: (0, i, 0, 0, 0))
    return pl.pallas_call(
        _wkv_kernel,
        grid=(b // nb, nc),
        in_specs=[dirs(0), dirs(0), dirs(0), both(0), both(0), both(0),
                  dirs(1), dirs(1), dirs(1), both(1), both(1), both(1), stspec],
        out_specs=[both(0), both(1), stspec],
        out_shape=[jax.ShapeDtypeStruct((b, l, width), F32), jax.ShapeDtypeStruct((b, l, width), F32),
                   jax.ShapeDtypeStruct(s0.shape, F32)],
        scratch_shapes=[pltpu.VMEM((2, nb) + s0.shape[2:], F32)],
        compiler_params=_params("arbitrary", "arbitrary"),
        name="wkv",
    )(lw, kd, bb, aa, r, v, lw, kd, bb, aa, r, v, s0)


def _s5_kernel(*refs, nb, reverse, toeplitz):
    if toeplitz:
        (u_ref, lag_ref, win_ref, wout_ref, a_ref, h0_ref, y_ref, hout_ref,
         x_scr, hh_scr, hre_scr, him_scr, wt_scr) = refs
    else:
        u_ref, win_ref, wout_ref, a_ref, h0_ref, y_ref, hout_ref, x_scr, hh_scr, hre_scr, him_scr = refs
    i = pl.program_id(1)
    ns = hre_scr.shape[1]

    @pl.when(i == 0)
    def _():
        hre_scr[...] = h0_ref[0, 0]
        him_scr[...] = h0_ref[0, 1]
        if toeplitz:
            t = (lag_ref.shape[1] + 1) // 2
            for sig in range(t):
                for tau in range(t):
                    wt_scr[sig * LANES:(sig + 1) * LANES, tau * LANES:(tau + 1) * LANES] = lag_ref[0, tau - sig + t - 1]

    u = u_ref[0].astype(BF16)
    x_scr[...] = jnp.dot(u, win_ref[0], preferred_element_type=F32)
    ar, ai = a_ref[0, 0:1, :], a_ref[0, 1:2, :]
    nch = x_scr.shape[0] // nb

    def body(k, carry):
        hr, hi = carry
        c = nch - 1 - k if reverse else k
        rows = pl.ds(pl.multiple_of(c * nb, nb), nb)
        hh_scr[rows, 0:ns] = hr
        hh_scr[rows, ns:2 * ns] = hi
        return (hr * ar - hi * ai + x_scr[rows, 0:ns], hr * ai + hi * ar + x_scr[rows, ns:2 * ns])

    hr, hi = lax.fori_loop(0, nch, body, (hre_scr[...], him_scr[...]))
    hre_scr[...] = hr
    him_scr[...] = hi
    y = _dot_nt(hh_scr[...].astype(BF16), wout_ref[0])
    if toeplitz:
        y = y + jnp.dot(u, wt_scr[...], preferred_element_type=F32)
    y_ref[0] = y

    @pl.when(i == pl.num_programs(1) - 1)
    def _():
        hout_ref[0, 0] = hr
        hout_ref[0, 1] = hi


def _s5_scan(u, lagblk, win, wout, acoef, h0, nb, reverse):
    nblk, rows, kw = u.shape
    ns = win.shape[2] // 2
    rt = min(rows, S5_ROWS)
    nt = rows // rt
    tile = pl.BlockSpec((1, rt, kw), lambda g, i: (g, (nt - 1 - i) if reverse else i, 0))
    per_blk = lambda a: pl.BlockSpec((1,) + a.shape[1:], lambda g, i: (g,) + (0,) * (a.ndim - 1))
    weights = ([lagblk] if lagblk is not None else []) + [win, wout, acoef, h0]
    return pl.pallas_call(
        functools.partial(_s5_kernel, nb=nb, reverse=reverse, toeplitz=lagblk is not None),
        grid=(nblk, nt),
        in_specs=[tile] + [per_blk(a) for a in weights],
        out_specs=[tile, per_blk(h0)],
        out_shape=[jax.ShapeDtypeStruct((nblk, rows, kw), F32), jax.ShapeDtypeStruct(h0.shape, F32)],
        scratch_shapes=[pltpu.VMEM((rt, 2 * ns), F32), pltpu.VMEM((rt, 2 * ns), F32),
                        pltpu.VMEM((nb, ns), F32), pltpu.VMEM((nb, ns), F32)]
        + ([pltpu.VMEM((kw, kw), BF16)] if lagblk is not None else []),
        compiler_params=_params("arbitrary", "arbitrary"),
        name="s5_bwd" if reverse else "s5_fwd",
    )(u, *weights)


def _s5_matrices(lam_re, lam_im, log_dt, b_re, b_im, c_re, c_im):
    hp = lax.Precision.HIGHEST
    t = S5_CHUNK
    dt = jnp.exp(log_dt.astype(F32))[..., None]
    lr, li = lam_re.astype(F32), lam_im.astype(F32)
    j = jnp.arange(t + 1, dtype=F32)[:, None, None, None]
    mag = jnp.exp(j * dt * lr)
    pr, pi = mag * jnp.cos(j * dt * li), mag * jnp.sin(j * dt * li)
    ar, ai = pr[1], pi[1]
    den = lr * lr + li * li
    fr = ((ar - 1.0) * lr + ai * li) / den
    fi = (ai * lr - (ar - 1.0) * li) / den
    bbr = fr[..., None] * b_re - fi[..., None] * b_im
    bbi = fr[..., None] * b_im + fi[..., None] * b_re
    car = c_re[None, None] * pr[:, :, :, None, :] - c_im[None, None] * pi[:, :, :, None, :]
    cai = c_re[None, None] * pi[:, :, :, None, :] + c_im[None, None] * pr[:, :, :, None, :]
    kj = (jnp.einsum('jdgcp,dgpe->jdgce', car, bbr, precision=hp)
          - jnp.einsum('jdgcp,dgpe->jdgce', cai, bbi, precision=hp))
    g = lr.shape[1]
    gpb = LANES // S5_GROUP
    nblk = g // gpb
    ns = gpb * S5_STATE
    own_chan = (jnp.arange(gpb)[:, None, None] == (jnp.arange(LANES) // S5_GROUP)[None, None, :])
    own_state = (jnp.arange(gpb)[:, None, None] == (jnp.arange(ns) // S5_STATE)[None, None, :])

    kl = jnp.concatenate([kj[t - 1:0:-1, 1], (kj[0, 0] + kj[0, 1])[None], kj[1:t, 0]], axis=0)
    kl = kl.reshape(2 * t - 1, nblk, gpb, S5_GROUP, S5_GROUP).transpose(0, 1, 4, 2, 3)
    kl = kl.reshape(2 * t - 1, nblk, 1, S5_GROUP, LANES)
    lagblk = jnp.where(own_chan, kl, 0.0).transpose(1, 0, 2, 3, 4).reshape(nblk, 2 * t - 1, LANES, LANES)

    def blockdiag_rows(q):
        return jnp.where(own_state, q[:, :, None], 0.0).reshape(nblk, t * LANES, ns)

    def state_in(d, pw):
        prs = pr[pw, d].reshape(t, nblk, 1, ns).transpose(1, 0, 2, 3)
        pis = pi[pw, d].reshape(t, nblk, 1, ns).transpose(1, 0, 2, 3)
        lay = lambda q: q.reshape(nblk, gpb, S5_STATE, S5_GROUP).transpose(0, 3, 1, 2).reshape(nblk, 1, S5_GROUP, ns)
        br, bi = lay(bbr[d]), lay(bbi[d])
        return jnp.concatenate([blockdiag_rows(prs * br - pis * bi), blockdiag_rows(prs * bi + pis * br)],
                               axis=-1).astype(BF16)

    def state_out(d, pw):
        lay = lambda q: q.reshape(t, nblk, gpb, S5_GROUP, S5_STATE).transpose(1, 0, 3, 2, 4).reshape(
            nblk, t, S5_GROUP, ns)
        return jnp.concatenate([blockdiag_rows(lay(car[pw, d])), blockdiag_rows(lay(-cai[pw, d]))],
                               axis=-1).astype(BF16)

    coef = lambda d: jnp.stack([pr[t, d].reshape(nblk, ns), pi[t, d].reshape(nblk, ns)], axis=1)
    fwd = (lagblk.astype(BF16), state_in(0, t - 1 - jnp.arange(t)), state_out(0, jnp.arange(t) + 1), coef(0))
    bwd = (None, state_in(1, jnp.arange(t)), state_out(1, t - jnp.arange(t)), coef(1))
    return fwd, bwd


def _s5_rows(u5):
    nblk, b, nc, t, w = u5.shape
    return u5.transpose(0, 2, 1, 3, 4).reshape(nblk, nc * b, t * w)


def _s5_unrows(y, b):
    nblk, rows, kw = y.shape
    return y.reshape(nblk, rows // b, b, S5_CHUNK, kw // S5_CHUNK).transpose(0, 2, 1, 3, 4)


def _out_kernel(x_ref, gate_ref, yf_ref, yb_ref, bonus_ref, gr_ref, u_ref, gs_ref, ys_ref, lg_ref, lb_ref, d_ref,
                wg_ref, wo_ref, fg_ref, e_ref, o_ref):
    dr = bonus_ref.shape[2]
    y = yf_ref[0] + yb_ref[0]
    mean = _segsum(y, e_ref) * (1.0 / HEAD)
    dev = y - mean
    var = _segsum(dev * dev, e_ref) * (1.0 / HEAD)
    yn = dev * lax.rsqrt(var + LN_X_EPS) * lg_ref[...] + lb_ref[...]
    gr = gr_ref[0]
    y_r = (yn + bonus_ref[0]) * (gr * _sigmoid(gr))

    tm = u_ref.shape[1]
    ys = jnp.concatenate([ys_ref[j, 0].reshape(tm, LANES) for j in range(ys_ref.shape[0])], axis=1)
    s = ys + d_ref[...] * u_ref[0]
    s = 0.5 * s * (1.0 + jnp.tanh(math.sqrt(2.0 / math.pi) * (s + 0.044715 * (s * s * s))))
    gl = jnp.dot(s.astype(BF16), wg_ref[...], preferred_element_type=F32)
    ds = gl.shape[1] // 2
    gs = gs_ref[0]
    y_s = gl[:, :ds] * _sigmoid(gl[:, ds:]) * (gs * _sigmoid(gs))

    out = (jnp.dot(y_r.astype(BF16), wo_ref[0:dr, :], preferred_element_type=F32)
           + jnp.dot(y_s.astype(BF16), wo_ref[dr:, :], preferred_element_type=F32))
    xo = x_ref[0] + gate_ref[0] * out
    ms = jnp.mean(xo * xo, axis=-1, keepdims=True)
    o_ref[0] = (xo * lax.rsqrt(ms + EPS) * fg_ref[...]).astype(o_ref.dtype)


def _out_stage(x, gate, yf, yb, bonus, z2, ys, lnx_g, lnx_b, s5_d, w_glu, w_out, final_g, e):
    b, l, d = x.shape
    dr = bonus.shape[2]
    ds = ys.shape[0] * LANES
    assert dr == ds
    tm = 512
    tok = lambda w, cb: pl.BlockSpec((1, tm, w), lambda i, j: (i, j, cb))
    const = lambda *shape: pl.BlockSpec(shape, lambda i, j: (0,) * len(shape))
    return pl.pallas_call(
        _out_kernel,
        grid=(b, l // tm),
        in_specs=[tok(d, 0), pl.BlockSpec((1, 1, d), lambda i, j: (i, 0, 0)),
                  tok(dr, 0), tok(dr, 0), tok(dr, 0), tok(dr, 0), tok(ds, 1), tok(ds, 2),
                  pl.BlockSpec((ys.shape[0], 1, tm // S5_CHUNK, S5_CHUNK, LANES), lambda i, j: (0, i, j, 0, 0)),
                  const(1, dr), const(1, dr), const(1, ds), const(ds, 2 * ds), const(dr + ds, d),
                  const(1, d), const(dr, dr)],
        out_specs=tok(d, 0),
        out_shape=jax.ShapeDtypeStruct((b, l, d), x.dtype),
        compiler_params=_params("arbitrary", "arbitrary"),
        name="out_stage",
    )(x, gate, yf, yb, bonus, z2, z2, z2, ys, lnx_g.reshape(1, dr), lnx_b.reshape(1, dr), s5_d.reshape(1, ds),
      w_glu.astype(BF16), w_out.astype(BF16), final_g.reshape(1, d), e)


def kernel(x, c, ctx, c_ctx, norm_g, w_ada, b_ada, w_in, mu_shift, rwkv_w0, rwkv_w2, rwkv_a0, rwkv_a2, rwkv_k_k, rwkv_k_a, rwkv_r_k, lnx_g, lnx_b, s5_lam_re, s5_lam_im, s5_log_dt, s5_b_re, s5_b_im, s5_c_re, s5_c_im, s5_d, s5_w_glu, w_out, final_g):
    assert norm_g.shape[0] == 1, "one layer"
    b, l, d = x.shape
    lc = ctx.shape[1]
    dr = rwkv_k_k.shape[1]
    ds = s5_d.shape[1]
    sw = mu_shift.shape[1]
    assert sw == 3 * dr + 4 * LORA and l % (4 * GRID_W) == 0 and lc % WKV_CHUNK == 0 and b % 8 == 0

    cond = jnp.zeros((2 * b, d), F32).at[:b].set(c).at[b].set(c_ctx)
    m = _modulation(cond, w_ada[0], b_ada[0])
    shift, scale, gate = m[:b, :d], m[:b, d:2 * d], m[:b, 2 * d:]
    cshift = jnp.broadcast_to(m[b, :d], (b, d))
    cscale = jnp.broadcast_to(m[b, d:2 * d], (b, d))

    w1 = w_in[0][:, :sw].astype(BF16)
    w2 = w_in[0][:, sw:].astype(BF16)
    z1c, _, u5c = _inproj(ctx, norm_g[0], cscale[:, None], cshift[:, None], w1, w2)
    z1, z2, u5 = _inproj(x, norm_g[0], scale[:, None], shift[:, None], w1, w2)

    hid = jnp.arange(dr) // HEAD
    e = (hid[:, None] == hid[None, :]).astype(BF16)
    pp = (mu_shift, rwkv_w0[0], rwkv_w2[0], rwkv_a0[0], rwkv_a2[0], rwkv_k_k, rwkv_k_a,
          rwkv_r_k[0].reshape(1, dr), e)
    rc, vc, aac, wc, kdc, bbc, _ = _prep(z1c, pp, grid2d=False)
    rx, vx, aax, wx, kdx, bbx, bonus = _prep(z1, pp, grid2d=True)

    s0 = jnp.zeros((2, b, dr // (WKV_PACK * HEAD), HEAD, WKV_PACK * HEAD), F32)
    _, _, s_ctx = _wkv(wc, kdc, bbc, aac, rc, vc, s0)
    yf, yb, _ = _wkv(wx, kdx, bbx, aax, rx, vx, s_ctx)

    s5f, s5b = _s5_matrices(s5_lam_re[0], s5_lam_im[0], s5_log_dt[0], s5_b_re[0], s5_b_im[0],
                            s5_c_re[0], s5_c_im[0])
    uc, ux = _s5_rows(u5c), _s5_rows(u5)
    h0 = jnp.zeros((uc.shape[0], 2, b, s5f[3].shape[2]), F32)
    _, hcf = _s5_scan(uc, *s5f, h0, b, False)
    _, hcb = _s5_scan(uc, *s5b, h0, b, True)
    ysf, _ = _s5_scan(ux, *s5f, hcf, b, False)
    ysb, _ = _s5_scan(ux, *s5b, hcb, b, True)
    ys = _s5_unrows(ysf + ysb, b)

    return _out_stage(x, gate[:, None], yf, yb, bonus, z2, ys, lnx_g[0], lnx_b[0], s5_d[0], s5_w_glu[0],
                      w_out[0], final_g, e)
```

```python
import functools
import math

import jax
import jax.numpy as jnp
from jax import lax
from jax.experimental import pallas as pl
from jax.experimental.pallas import tpu as pltpu

F32 = jnp.float32
BF16 = jnp.bfloat16

GRID_W = 64
HEAD = 64
LORA = 64
S5_GROUP = 16
S5_STATE = 64
S5_CHUNK = 16
S5_ROWS = 256
WKV_CHUNK = 64
WKV_ROWS = 2
WKV_PACK = 2
EPS = 1e-6
LN_X_EPS = 64e-5
EXP_M05 = math.exp(-0.5)

LANES = 128
VMEM_LIMIT = 56 * 1024 * 1024


def _params(*sem):
    return pltpu.CompilerParams(dimension_semantics=sem, vmem_limit_bytes=VMEM_LIMIT)


def _sigmoid(x):
    return 1.0 / (1.0 + jnp.exp(-x))


def _dot_bf16(a, b):
    return jnp.dot(a.astype(BF16), b.astype(BF16), preferred_element_type=F32)


def _dot_f32(a, b):
    return jnp.dot(a, b, preferred_element_type=F32, precision=lax.Precision.HIGHEST)


def _segsum(x, e_ref):
    hi = x.astype(BF16)
    lo = (x - hi.astype(F32)).astype(BF16)
    e = e_ref[...]
    return (jnp.dot(hi, e, preferred_element_type=F32) + jnp.dot(lo, e, preferred_element_type=F32))


def _mod_kernel(c_ref, w_ref, b_ref, o_ref):
    c = c_ref[...]
    o_ref[...] = _dot_f32(c * _sigmoid(c), w_ref[...]) + b_ref[...]


def _modulation(cond, w_ada, b_ada):
    rows, d = cond.shape
    n = w_ada.shape[1]
    tn = 512
    return pl.pallas_call(
        _mod_kernel,
        grid=(n // tn,),
        in_specs=[pl.BlockSpec((rows, d), lambda j: (0, 0)),
                  pl.BlockSpec((d, tn), lambda j: (0, j)),
                  pl.BlockSpec((1, tn), lambda j: (0, j))],
        out_specs=pl.BlockSpec((rows, tn), lambda j: (0, j)),
        out_shape=jax.ShapeDtypeStruct((rows, n), F32),
        compiler_params=_params("arbitrary"),
        name="modulation",
    )(cond, w_ada, b_ada.reshape(1, n))


def _inproj_kernel(x_ref, g_ref, sc_ref, sh_ref, w1_ref, w2_ref, z1_ref, z2_ref, u_ref):
    x = x_ref[0]
    ms = jnp.mean(x * x, axis=-1, keepdims=True)
    h = x * lax.rsqrt(ms + EPS) * g_ref[...]
    h = (h * (1.0 + sc_ref[0]) + sh_ref[0]).astype(BF16)
    z1_ref[0] = jnp.dot(h, w1_ref[...], preferred_element_type=F32).astype(z1_ref.dtype)
    z2 = jnp.dot(h, w2_ref[...], preferred_element_type=F32)
    z2_ref[0] = z2.astype(z2_ref.dtype)
    nblk, _, nch = u_ref.shape[0], u_ref.shape[1], u_ref.shape[2]
    w = z2.shape[1] // 3
    for j in range(nblk):
        u_ref[j, 0] = z2[:, w + LANES * j:w + LANES * (j + 1)].reshape(nch, S5_CHUNK, LANES).astype(u_ref.dtype)


def _inproj(x, norm_g, scale, shift, w1, w2):
    b, l, d = x.shape
    n1, n2 = w1.shape[1], w2.shape[1]
    tm = min(l, 512)
    nblk = n2 // 3 // LANES
    return pl.pallas_call(
        _inproj_kernel,
        grid=(b, l // tm),
        in_specs=[pl.BlockSpec((1, tm, d), lambda i, j: (i, j, 0)),
                  pl.BlockSpec((1, d), lambda i, j: (0, 0)),
                  pl.BlockSpec((1, 1, d), lambda i, j: (i, 0, 0)),
                  pl.BlockSpec((1, 1, d), lambda i, j: (i, 0, 0)),
                  pl.BlockSpec((d, n1), lambda i, j: (0, 0)),
                  pl.BlockSpec((d, n2), lambda i, j: (0, 0))],
        out_specs=[pl.BlockSpec((1, tm, n1), lambda i, j: (i, j, 0)),
                   pl.BlockSpec((1, tm, n2), lambda i, j: (i, j, 0)),
                   pl.BlockSpec((nblk, 1, tm // S5_CHUNK, S5_CHUNK, LANES), lambda i, j: (0, i, j, 0, 0))],
        out_shape=[jax.ShapeDtypeStruct((b, l, n1), BF16),
                   jax.ShapeDtypeStruct((b, l, n2), BF16),
                   jax.ShapeDtypeStruct((nblk, b, l // S5_CHUNK, S5_CHUNK, LANES), BF16)],
        compiler_params=_params("arbitrary", "arbitrary"),
        name="inproj",
    )(x, norm_g.reshape(1, d), scale, shift, w1, w2)


def _prep_body(z, up, down, mu_ref, w0_ref, w2_ref, a0_ref, a2_ref, kk_ref, ka_ref, rk_ref, e_ref, outs,
               grid2d):
    t, sw = z.shape
    dr = (sw - 4 * LORA) // 3
    row = lax.broadcasted_iota(jnp.int32, (t, 1), 0)
    slot = lax.broadcasted_iota(jnp.int32, (1, sw), 1) & 3
    prev = pltpu.roll(z, 1, 0)
    nxt = pltpu.roll(z, t - 1, 0)
    if grid2d:
        col = row & (GRID_W - 1)
        prev = jnp.where(col == 0, 0.0, prev)
        nxt = jnp.where(col == GRID_W - 1, 0.0, nxt)
        sh = jnp.where(slot == 0, prev, jnp.where(slot == 1, nxt, jnp.where(slot == 2, up, down)))
    else:
        prev = jnp.where(row == 0, 0.0, prev)
        nxt = jnp.where(row == t - 1, 0.0, nxt)
        sh = jnp.where((slot & 1) == 0, prev, nxt)
    zs = z + mu_ref[...] * (sh - z)

    r = zs[:, 0:dr]
    k = zs[:, dr:2 * dr]
    v = zs[:, 2 * dr:3 * dr]
    kk = k * kk_ref[...]
    ss = _segsum(kk * kk, e_ref)
    kk = kk / jnp.maximum(jnp.sqrt(ss), 1e-12)
    r_o, v_o, aa_o, w_o, kd_o, bb_o, bonus_o = outs
    r_o[0] = r.astype(r_o.dtype)
    v_o[0] = v.astype(v_o.dtype)
    aa_o[0] = (-kk).astype(aa_o.dtype)
    ksum = None
    for d in range(2):
        zw = zs[:, 3 * dr + LORA * d:3 * dr + LORA * (d + 1)]
        za = zs[:, 3 * dr + 2 * LORA + LORA * d:3 * dr + 2 * LORA + LORA * (d + 1)]
        wl = w0_ref[d:d + 1, :] + _dot_f32(jnp.tanh(zw), w2_ref[d])
        w_o[d, 0] = -EXP_M05 * _sigmoid(wl)
        asig = _sigmoid(a0_ref[d:d + 1, :] + _dot_f32(za, a2_ref[d]))
        kd = k * (1.0 + (asig - 1.0) * ka_ref[...])
        kd_o[d, 0] = kd.astype(kd_o.dtype)
        bb_o[d, 0] = (kk * asig).astype(bb_o.dtype)
        ksum = kd if ksum is None else ksum + kd
    bonus_o[0] = (_segsum(r * (0.5 * ksum) * rk_ref[...], e_ref) * v).astype(bonus_o.dtype)


def _prep2d_kernel(zc_ref, zu_ref, zd_ref, *rest):
    params, outs = rest[:9], rest[9:]
    j = pl.program_id(1)
    nj = pl.num_programs(1)
    z = zc_ref[0].astype(F32)
    t = z.shape[0]
    row = lax.broadcasted_iota(jnp.int32, (t, 1), 0)
    up = jnp.concatenate([zu_ref[0].astype(F32), z[:t - GRID_W]], axis=0)
    up = jnp.where(jnp.logical_and(j == 0, row < GRID_W), 0.0, up)
    down = jnp.concatenate([z[GRID_W:], zd_ref[0].astype(F32)], axis=0)
    down = jnp.where(jnp.logical_and(j == nj - 1, row >= t - GRID_W), 0.0, down)
    _prep_body(z, up, down, *params, outs, grid2d=True)


def _prep1d_kernel(zc_ref, *rest):
    params, outs = rest[:9], rest[9:]
    _prep_body(zc_ref[0].astype(F32), None, None, *params, outs, grid2d=False)


def _prep(z1, pp, grid2d):
    b, l, sw = z1.shape
    dr = (sw - 4 * LORA) // 3
    tt = 256 if grid2d else l
    nj = l // tt
    rb = tt // GRID_W
    const = lambda *shape: pl.BlockSpec(shape, lambda i, j: (0,) * len(shape))
    p_specs = [const(1, sw), const(2, dr), const(2, LORA, dr), const(2, dr), const(2, LORA, dr),
               const(1, dr), const(1, dr), const(1, dr), const(dr, dr)]
    cur = pl.BlockSpec((1, tt, sw), lambda i, j: (i, j, 0))
    if grid2d:
        nrow = l // GRID_W
        in_specs = [cur,
                    pl.BlockSpec((1, GRID_W, sw), lambda i, j: (i, jnp.maximum(j * rb - 1, 0), 0)),
                    pl.BlockSpec((1, GRID_W, sw), lambda i, j: (i, jnp.minimum((j + 1) * rb, nrow - 1), 0))]
        args = (z1, z1, z1)
        body = _prep2d_kernel
    else:
        in_specs = [cur]
        args = (z1,)
        body = _prep1d_kernel
    o1 = pl.BlockSpec((1, tt, dr), lambda i, j: (i, j, 0))
    o2 = pl.BlockSpec((2, 1, tt, dr), lambda i, j: (0, i, j, 0))
    s1 = jax.ShapeDtypeStruct((b, l, dr), BF16)
    s2 = jax.ShapeDtypeStruct((2, b, l, dr), BF16)
    s2w = jax.ShapeDtypeStruct((2, b, l, dr), F32)
    return pl.pallas_call(
        body,
        grid=(b, nj),
        in_specs=in_specs + p_specs,
        out_specs=[o1, o1, o1, o2, o2, o2, o1],
        out_shape=[s1, s1, s1, s2w, s2, s2, s1],
        compiler_params=_params("arbitrary", "arbitrary"),
        name="prep2d" if grid2d else "prep1d",
    )(*args, *pp)


def _dot_nt(a, b):
    return lax.dot_general(a, b, (((1,), (1,)), ((), ())), preferred_element_type=F32)


def _dot_tn(a, b):
    return lax.dot_general(a, b, (((0,), (0,)), ((), ())), preferred_element_type=F32)


def _wkv_operands(lw_ref, kd_ref, bb_ref, aa_ref, r_ref, v_ref, bi, backward):
    c = aa_ref.shape[1]
    row = lax.broadcasted_iota(jnp.int32, (c, c), 0)
    col = lax.broadcasted_iota(jnp.int32, (c, c), 1)
    lw = lw_ref[0, bi]
    lw_hi = lw.astype(BF16)
    lw_lo = (lw - lw_hi.astype(F32)).astype(BF16)
    lc = (col >= row if backward else col <= row).astype(BF16)
    cum = jnp.dot(lc, lw_hi, preferred_element_type=F32) + jnp.dot(lc, lw_lo, preferred_element_type=F32)
    tot = cum[0:1, :] if backward else cum[c - 1:c, :]
    pinv = jnp.exp(-cum)
    pend = jnp.exp(tot - cum)
    bb, kd = bb_ref[0, bi].astype(F32), kd_ref[0, bi].astype(F32)
    rt = r_ref[bi].astype(F32) * jnp.exp(cum)
    return dict(
        lw_hi=lw_hi, lw_lo=lw_lo, at=(aa_ref[bi].astype(F32) * jnp.exp(cum - lw)).astype(BF16),
        rt=rt, rt_b=rt.astype(BF16),
        bt=(bb * pinv).astype(BF16), kt=(kd * pinv).astype(BF16),
        be=(bb * pend).astype(BF16), ke=(kd * pend).astype(BF16), v=v_ref[bi].astype(BF16))


def _wkv_kernel(lwf_ref, kdf_ref, bbf_ref, aaf_ref, rf_ref, vf_ref, lwb_ref, kdb_ref, bbb_ref, aab_ref, rb_ref,
                vb_ref, s0_ref, yf_ref, yb_ref, sout_ref, s_scr):
    ci = pl.program_id(1)
    nb, c, width = aaf_ref.shape
    gw = WKV_PACK * HEAD
    ng = width // gw
    assert c == HEAD and width % gw == 0

    @pl.when(ci == 0)
    def _():
        s_scr[...] = s0_ref[...]

    ops = {(0, bi): _wkv_operands(lwf_ref, kdf_ref, bbf_ref, aaf_ref, rf_ref, vf_ref, bi, False) for bi in range(nb)}
    ops.update({(1, bi): _wkv_operands(lwb_ref, kdb_ref, bbb_ref, aab_ref, rb_ref, vb_ref, bi, True)
                for bi in range(nb)})
    chains = [(d, bi, gi) for gi in range(ng) for bi in range(nb) for d in range(2)]
    n = range(len(chains))
    cut = lambda name: [ops[d, bi][name][:, gi * gw:(gi + 1) * gw] for d, bi, gi in chains]
    at, rt, rt_b, bt, kt, be, ke, v, lw_hi, lw_lo = (
        cut(k) for k in ("at", "rt", "rt_b", "bt", "kt", "be", "ke", "v", "lw_hi", "lw_lo"))
    mm = lambda a, b: jnp.dot(a, b, preferred_element_type=F32)

    r64 = lax.broadcasted_iota(jnp.int32, (c, gw), 0)
    c64 = lax.broadcasted_iota(jnp.int32, (c, gw), 1) & (HEAD - 1)
    eye_f = (r64 == c64).astype(F32)
    causal = {0: (c64 < r64, c64 <= r64), 1: (c64 > r64, c64 >= r64)}
    strict = [causal[d][0] for d, _, _ in chains]
    incl = [causal[d][1] for d, _, _ in chains]
    same_head = (lax.broadcasted_iota(jnp.int32, (gw, gw), 0) // HEAD
                 == lax.broadcasted_iota(jnp.int32, (gw, gw), 1) // HEAD)

    def bd(xp):
        return jnp.where(same_head, jnp.concatenate([xp] * (gw // c), axis=0), jnp.zeros((), xp.dtype))

    def diag(o):
        head = lax.broadcasted_iota(jnp.int32, (c, gw), 1) // HEAD
        out = o[:c]
        for h in range(1, gw // c):
            out = jnp.where(head == h, o[h * c:(h + 1) * c], out)
        return out

    rows = lambda a, b: jnp.concatenate([a, b], axis=0)
    cols = lambda a, b: jnp.concatenate([a, b], axis=1)
    g = [_dot_nt(rows(at[i], rt_b[i]), rows(bd(bt[i]), bd(kt[i]))) for i in n]
    x = [jnp.where(strict[i], g[i][:c, :gw], 0.0) for i in n]
    a_ak = [jnp.where(strict[i], g[i][:c, gw:], 0.0).astype(BF16) for i in n]
    m_rb = [jnp.where(incl[i], g[i][c:, :gw], 0.0).astype(BF16) for i in n]
    m_rk = [jnp.where(incl[i], g[i][c:, gw:], 0.0).astype(BF16) for i in n]
    t = [eye_f + x[i] for i in n]
    p = [x[i].astype(BF16) for i in n]
    p = [mm(p[i], bd(p[i])).astype(BF16) for i in n]
    for _ in range((c - 1).bit_length() - 2):
        o = [mm(rows(p[i], t[i].astype(BF16)), bd(p[i])) for i in n]
        p = [o[i][:c].astype(BF16) for i in n]
        t = [t[i] + o[i][c:] for i in n]
    t = [t[i] + mm(t[i].astype(BF16), bd(p[i])) for i in n]
    ov = [mm(rows(a_ak[i], m_rk[i]), bd(v[i])) for i in n]
    za = [mm(t[i].astype(BF16), cols(bd(at[i]), bd(ov[i][:c].astype(BF16)))).astype(BF16) for i in n]
    w2 = [mm(m_rb[i], cols(bd(za[i][:, :gw]), bd(za[i][:, gw:]))) for i in n]
    yp = [w2[i][:, gw:] + ov[i][c:] for i in n]
    tn = [_dot_tn(rows(be[i], ke[i]), rows(za[i], cols(jnp.zeros((c, gw), BF16), v[i]))) for i in n]
    gd = [diag(tn[i][:, :gw]) for i in n]
    hv = [diag(tn[i][:, gw:]) for i in n]
    ones = jnp.ones((c, gw), BF16)
    pc = [jnp.exp(diag(_dot_tn(lw_hi[i], ones) + _dot_tn(lw_lo[i], ones))) for i in n]
    s = [s_scr[d, bi, gi] for d, bi, gi in chains]
    os_ = [mm(rows((rt[i] + w2[i][:, :gw]).astype(BF16), gd[i].astype(BF16)), bd(s[i].astype(BF16))) for i in n]
    y = [os_[i][:c] + yp[i] for i in n]
    for i, (d, bi, gi) in enumerate(chains):
        s_scr[d, bi, gi] = pc[i] * s[i] + os_[i][c:] + hv[i]
    for d, y_ref in enumerate((yf_ref, yb_ref)):
        for bi in range(nb):
            y_ref[bi] = jnp.concatenate([y[chains.index((d, bi, gi))] for gi in range(ng)],
                                        axis=1).astype(y_ref.dtype)

    @pl.when(ci == pl.num_programs(1) - 1)
    def _():
        sout_ref[...] = s_scr[...]


def _wkv(lw, kd, bb, aa, r, v, s0):
    _, b, l, width = lw.shape
    c = WKV_CHUNK
    nc = l // c
    nb = WKV_ROWS
    dirs = lambda d: pl.BlockSpec((1, nb, c, width), lambda i, j: (d, i, (nc - 1 - j) if d else j, 0))
    both = lambda d: pl.BlockSpec((nb, c, width), lambda i, j: (i, (nc - 1 - j) if d else j, 0))
    stspec = pl.BlockSpec((2, nb) + s0.shape[2:], lambda i, j: (0, i, 0, 0, 0))
    return pl.pallas_call(
        _wkv_kernel,
        grid=(b // nb, nc),
        in_specs=[dirs(0), dirs(0), dirs(0), both(0), both(0), both(0),
                  dirs(1), dirs(1), dirs(1), both(1), both(1), both(1), stspec],
        out_specs=[both(0), both(1), stspec],
        out_shape=[jax.ShapeDtypeStruct((b, l, width), BF16), jax.ShapeDtypeStruct((b, l, width), BF16),
                   jax.ShapeDtypeStruct(s0.shape, F32)],
        scratch_shapes=[pltpu.VMEM((2, nb) + s0.shape[2:], F32)],
        compiler_params=_params("arbitrary", "arbitrary"),
        name="wkv",
    )(lw, kd, bb, aa, r, v, lw, kd, bb, aa, r, v, s0)


def _s5_kernel(*refs, nb, reverse, toeplitz):
    if toeplitz:
        (u_ref, lag_ref, win_ref, wout_ref, a_ref, h0_ref, y_ref, hout_ref,
         x_scr, hh_scr, hre_scr, him_scr, wt_scr) = refs
    else:
        (u_ref, yprev_ref, win_ref, wout_ref, a_ref, h0_ref, y_ref, hout_ref,
         x_scr, hh_scr, hre_scr, him_scr) = refs
    i = pl.program_id(1)
    ns = hre_scr.shape[1]

    @pl.when(i == 0)
    def _():
        hre_scr[...] = h0_ref[0, 0]
        him_scr[...] = h0_ref[0, 1]
        if toeplitz:
            t = (lag_ref.shape[1] + 1) // 2
            for sig in range(t):
                for tau in range(t):
                    wt_scr[sig * LANES:(sig + 1) * LANES, tau * LANES:(tau + 1) * LANES] = lag_ref[0, tau - sig + t - 1]

    u = u_ref[0].astype(BF16)
    x_scr[...] = jnp.dot(u, win_ref[0], preferred_element_type=F32)
    ar, ai = a_ref[0, 0:1, :], a_ref[0, 1:2, :]
    nch = x_scr.shape[0] // nb

    def body(k, carry):
        hr, hi = carry
        c = nch - 1 - k if reverse else k
        rows = pl.ds(pl.multiple_of(c * nb, nb), nb)
        hh_scr[rows, 0:ns] = hr
        hh_scr[rows, ns:2 * ns] = hi
        return (hr * ar - hi * ai + x_scr[rows, 0:ns], hr * ai + hi * ar + x_scr[rows, ns:2 * ns])

    hr, hi = lax.fori_loop(0, nch, body, (hre_scr[...], him_scr[...]))
    hre_scr[...] = hr
    him_scr[...] = hi
    y = _dot_nt(hh_scr[...].astype(BF16), wout_ref[0])
    if toeplitz:
        y = y + jnp.dot(u, wt_scr[...], preferred_element_type=F32)
    else:
        y = y + yprev_ref[0].astype(F32)
    y_ref[0] = y.astype(y_ref.dtype)

    @pl.when(i == pl.num_programs(1) - 1)
    def _():
        hout_ref[0, 0] = hr
        hout_ref[0, 1] = hi


def _s5_scan(u, first, win, wout, acoef, h0, nb, reverse):
    nblk, rows, kw = u.shape
    ns = win.shape[2] // 2
    rt = min(rows, S5_ROWS)
    nt = rows // rt
    tile = pl.BlockSpec((1, rt, kw), lambda g, i: (g, (nt - 1 - i) if reverse else i, 0))
    per_blk = lambda a: pl.BlockSpec((1,) + a.shape[1:], lambda g, i: (g,) + (0,) * (a.ndim - 1))
    weights = [win, wout, acoef, h0]
    toeplitz = not reverse
    return pl.pallas_call(
        functools.partial(_s5_kernel, nb=nb, reverse=reverse, toeplitz=toeplitz),
        grid=(nblk, nt),
        in_specs=[tile, per_blk(first) if toeplitz else tile] + [per_blk(a) for a in weights],
        out_specs=[tile, per_blk(h0)],
        out_shape=[jax.ShapeDtypeStruct((nblk, rows, kw), BF16), jax.ShapeDtypeStruct(h0.shape, F32)],
        scratch_shapes=[pltpu.VMEM((rt, 2 * ns), F32), pltpu.VMEM((rt, 2 * ns), F32),
                        pltpu.VMEM((nb, ns), F32), pltpu.VMEM((nb, ns), F32)]
        + ([pltpu.VMEM((kw, kw), BF16)] if toeplitz else []),
        compiler_params=_params("arbitrary", "arbitrary"),
        name="s5_bwd" if reverse else "s5_fwd",
    )(u, first, *weights)


def _s5_matrices(lam_re, lam_im, log_dt, b_re, b_im, c_re, c_im):
    hp = lax.Precision.HIGHEST
    t = S5_CHUNK
    dt = jnp.exp(log_dt.astype(F32))[..., None]
    lr, li = lam_re.astype(F32), lam_im.astype(F32)
    j = jnp.arange(t + 1, dtype=F32)[:, None, None, None]
    mag = jnp.exp(j * dt * lr)
    pr, pi = mag * jnp.cos(j * dt * li), mag * jnp.sin(j * dt * li)
    ar, ai = pr[1], pi[1]
    den = lr * lr + li * li
    fr = ((ar - 1.0) * lr + ai * li) / den
    fi = (ai * lr - (ar - 1.0) * li) / den
    bbr = fr[..., None] * b_re - fi[..., None] * b_im
    bbi = fr[..., None] * b_im + fi[..., None] * b_re
    car = c_re[None, None] * pr[:, :, :, None, :] - c_im[None, None] * pi[:, :, :, None, :]
    cai = c_re[None, None] * pi[:, :, :, None, :] + c_im[None, None] * pr[:, :, :, None, :]
    kj = (jnp.einsum('jdgcp,dgpe->jdgce', car, bbr, precision=hp)
          - jnp.einsum('jdgcp,dgpe->jdgce', cai, bbi, precision=hp))
    g = lr.shape[1]
    gpb = LANES // S5_GROUP
    nblk = g // gpb
    ns = gpb * S5_STATE
    own_chan = (jnp.arange(gpb)[:, None, None] == (jnp.arange(LANES) // S5_GROUP)[None, None, :])
    own_state = (jnp.arange(gpb)[:, None, None] == (jnp.arange(ns) // S5_STATE)[None, None, :])

    kl = jnp.concatenate([kj[t - 1:0:-1, 1], (kj[0, 0] + kj[0, 1])[None], kj[1:t, 0]], axis=0)
    kl = kl.reshape(2 * t - 1, nblk, gpb, S5_GROUP, S5_GROUP).transpose(0, 1, 4, 2, 3)
    kl = kl.reshape(2 * t - 1, nblk, 1, S5_GROUP, LANES)
    lagblk = jnp.where(own_chan, kl, 0.0).transpose(1, 0, 2, 3, 4).reshape(nblk, 2 * t - 1, LANES, LANES)

    def blockdiag_rows(q):
        return jnp.where(own_state, q[:, :, None], 0.0).reshape(nblk, t * LANES, ns)

    def state_in(d, pw):
        prs = pr[pw, d].reshape(t, nblk, 1, ns).transpose(1, 0, 2, 3)
        pis = pi[pw, d].reshape(t, nblk, 1, ns).transpose(1, 0, 2, 3)
        lay = lambda q: q.reshape(nblk, gpb, S5_STATE, S5_GROUP).transpose(0, 3, 1, 2).reshape(nblk, 1, S5_GROUP, ns)
        br, bi = lay(bbr[d]), lay(bbi[d])
        return jnp.concatenate([blockdiag_rows(prs * br - pis * bi), blockdiag_rows(prs * bi + pis * br)],
                               axis=-1).astype(BF16)

    def state_out(d, pw):
        lay = lambda q: q.reshape(t, nblk, gpb, S5_GROUP, S5_STATE).transpose(1, 0, 3, 2, 4).reshape(
            nblk, t, S5_GROUP, ns)
        return jnp.concatenate([blockdiag_rows(lay(car[pw, d])), blockdiag_rows(lay(-cai[pw, d]))],
                               axis=-1).astype(BF16)

    coef = lambda d: jnp.stack([pr[t, d].reshape(nblk, ns), pi[t, d].reshape(nblk, ns)], axis=1)
    fwd = (lagblk.astype(BF16), state_in(0, t - 1 - jnp.arange(t)), state_out(0, jnp.arange(t) + 1), coef(0))
    bwd = (state_in(1, jnp.arange(t)), state_out(1, t - jnp.arange(t)), coef(1))
    return fwd, bwd


def _s5_rows(u5):
    nblk, b, nc, t, w = u5.shape
    return u5.transpose(0, 2, 1, 3, 4).reshape(nblk, nc * b, t * w)


def _s5_unrows(y, b):
    nblk, rows, kw = y.shape
    return y.reshape(nblk, rows // b, b, S5_CHUNK, kw // S5_CHUNK).transpose(0, 2, 1, 3, 4)


def _out_kernel(x_ref, gate_ref, yf_ref, yb_ref, bonus_ref, gr_ref, u_ref, gs_ref, ys_ref, lg_ref, lb_ref, d_ref,
                wg_ref, wo_ref, fg_ref, e_ref, o_ref):
    dr = bonus_ref.shape[2]
    y = yf_ref[0].astype(F32) + yb_ref[0].astype(F32)
    mean = _segsum(y, e_ref) * (1.0 / HEAD)
    dev = y - mean
    var = _segsum(dev * dev, e_ref) * (1.0 / HEAD)
    yn = dev * lax.rsqrt(var + LN_X_EPS) * lg_ref[...] + lb_ref[...]
    gr = gr_ref[0].astype(F32)
    y_r = (yn + bonus_ref[0].astype(F32)) * (gr * _sigmoid(gr))

    tm = u_ref.shape[1]
    ys = jnp.concatenate([ys_ref[j, 0].reshape(tm, LANES) for j in range(ys_ref.shape[0])], axis=1)
    s = ys.astype(F32) + d_ref[...] * u_ref[0].astype(F32)
    s = 0.5 * s * (1.0 + jnp.tanh(math.sqrt(2.0 / math.pi) * (s + 0.044715 * (s * s * s))))
    gl = jnp.dot(s.astype(BF16), wg_ref[...], preferred_element_type=F32)
    ds = gl.shape[1] // 2
    gs = gs_ref[0].astype(F32)
    y_s = gl[:, :ds] * _sigmoid(gl[:, ds:]) * (gs * _sigmoid(gs))

    out = (jnp.dot(y_r.astype(BF16), wo_ref[0:dr, :], preferred_element_type=F32)
           + jnp.dot(y_s.astype(BF16), wo_ref[dr:, :], preferred_element_type=F32))
    xo = x_ref[0] + gate_ref[0] * out
    ms = jnp.mean(xo * xo, axis=-1, keepdims=True)
    o_ref[0] = (xo * lax.rsqrt(ms + EPS) * fg_ref[...]).astype(o_ref.dtype)


def _out_stage(x, gate, yf, yb, bonus, z2, ys, lnx_g, lnx_b, s5_d, w_glu, w_out, final_g, e):
    b, l, d = x.shape
    dr = bonus.shape[2]
    ds = ys.shape[0] * LANES
    assert dr == ds
    tm = 512
    tok = lambda w, cb: pl.BlockSpec((1, tm, w), lambda i, j: (i, j, cb))
    const = lambda *shape: pl.BlockSpec(shape, lambda i, j: (0,) * len(shape))
    return pl.pallas_call(
        _out_kernel,
        grid=(b, l // tm),
        in_specs=[tok(d, 0), pl.BlockSpec((1, 1, d), lambda i, j: (i, 0, 0)),
                  tok(dr, 0), tok(dr, 0), tok(dr, 0), tok(dr, 0), tok(ds, 1), tok(ds, 2),
                  pl.BlockSpec((ys.shape[0], 1, tm // S5_CHUNK, S5_CHUNK, LANES), lambda i, j: (0, i, j, 0, 0)),
                  const(1, dr), const(1, dr), const(1, ds), const(ds, 2 * ds), const(dr + ds, d),
                  const(1, d), const(dr, dr)],
        out_specs=tok(d, 0),
        out_shape=jax.ShapeDtypeStruct((b, l, d), x.dtype),
        compiler_params=_params("arbitrary", "arbitrary"),
        name="out_stage",
    )(x, gate, yf, yb, bonus, z2, z2, z2, ys, lnx_g.reshape(1, dr), lnx_b.reshape(1, dr), s5_d.reshape(1, ds),
      w_glu.astype(BF16), w_out.astype(BF16), final_g.reshape(1, d), e)


def kernel(x, c, ctx, c_ctx, norm_g, w_ada, b_ada, w_in, mu_shift, rwkv_w0, rwkv_w2, rwkv_a0, rwkv_a2, rwkv_k_k, rwkv_k_a, rwkv_r_k, lnx_g, lnx_b, s5_lam_re, s5_lam_im, s5_log_dt, s5_b_re, s5_b_im, s5_c_re, s5_c_im, s5_d, s5_w_glu, w_out, final_g):
    assert norm_g.shape[0] == 1, "one layer"
    b, l, d = x.shape
    lc = ctx.shape[1]
    dr = rwkv_k_k.shape[1]
    ds = s5_d.shape[1]
    sw = mu_shift.shape[1]
    assert sw == 3 * dr + 4 * LORA and l % (4 * GRID_W) == 0 and lc % WKV_CHUNK == 0 and b % 8 == 0

    cond = jnp.zeros((2 * b, d), F32).at[:b].set(c).at[b].set(c_ctx)
    m = _modulation(cond, w_ada[0], b_ada[0])
    shift, scale, gate = m[:b, :d], m[:b, d:2 * d], m[:b, 2 * d:]
    cshift = jnp.broadcast_to(m[b, :d], (b, d))
    cscale = jnp.broadcast_to(m[b, d:2 * d], (b, d))

    w1 = w_in[0][:, :sw].astype(BF16)
    w2 = w_in[0][:, sw:].astype(BF16)
    z1c, _, u5c = _inproj(ctx, norm_g[0], cscale[:, None], cshift[:, None], w1, w2)
    z1, z2, u5 = _inproj(x, norm_g[0], scale[:, None], shift[:, None], w1, w2)

    hid = jnp.arange(dr) // HEAD
    e = (hid[:, None] == hid[None, :]).astype(BF16)
    pp = (mu_shift, rwkv_w0[0], rwkv_w2[0], rwkv_a0[0], rwkv_a2[0], rwkv_k_k, rwkv_k_a,
          rwkv_r_k[0].reshape(1, dr), e)
    rc, vc, aac, wc, kdc, bbc, _ = _prep(z1c, pp, grid2d=False)
    rx, vx, aax, wx, kdx, bbx, bonus = _prep(z1, pp, grid2d=True)

    s0 = jnp.zeros((2, b, dr // (WKV_PACK * HEAD), HEAD, WKV_PACK * HEAD), F32)
    _, _, s_ctx = _wkv(wc, kdc, bbc, aac, rc, vc, s0)
    yf, yb, _ = _wkv(wx, kdx, bbx, aax, rx, vx, s_ctx)

    s5f, s5b = _s5_matrices(s5_lam_re[0], s5_lam_im[0], s5_log_dt[0], s5_b_re[0], s5_b_im[0],
                            s5_c_re[0], s5_c_im[0])
    uc, ux = _s5_rows(u5c), _s5_rows(u5)
    h0 = jnp.zeros((uc.shape[0], 2, b, s5f[3].shape[2]), F32)
    ysc, hcf = _s5_scan(uc, *s5f, h0, b, False)
    _, hcb = _s5_scan(uc, ysc, *s5b, h0, b, True)
    ysf, _ = _s5_scan(ux, *s5f, hcf, b, False)
    ysfb, _ = _s5_scan(ux, ysf, *s5b, hcb, b, True)
    ys = _s5_unrows(ysfb, b)

    return _out_stage(x, gate[:, None], yf, yb, bonus, z2, ys, lnx_g[0], lnx_b[0], s5_d[0], s5_w_glu[0],
                      w_out[0], final_g, e)
```

```python
import functools
import math

import jax
import jax.numpy as jnp
from jax import lax
from jax.experimental import pallas as pl
from jax.experimental.pallas import tpu as pltpu

F32 = jnp.float32
BF16 = jnp.bfloat16

GRID_W = 64
HEAD = 64
LORA = 64
S5_GROUP = 16
S5_STATE = 64
S5_CHUNK = 16
S5_ROWS = 256
WKV_CHUNK = 64
WKV_ROWS = 4
WKV_PACK = 2
EPS = 1e-6
LN_X_EPS = 64e-5
EXP_M05 = math.exp(-0.5)

LANES = 128
VMEM_LIMIT = 56 * 1024 * 1024


def _params(*sem):
    return pltpu.CompilerParams(dimension_semantics=sem, vmem_limit_bytes=VMEM_LIMIT)


def _sigmoid(x):
    return 1.0 / (1.0 + jnp.exp(-x))


def _dot_bf16(a, b):
    return jnp.dot(a.astype(BF16), b.astype(BF16), preferred_element_type=F32)


def _dot_f32(a, b):
    return jnp.dot(a, b, preferred_element_type=F32, precision=lax.Precision.HIGHEST)


def _segsum(x, e_ref):
    return jnp.dot(x.astype(BF16), e_ref[...], preferred_element_type=F32)


def _mod_kernel(c_ref, w_ref, b_ref, o_ref):
    c = c_ref[...]
    o_ref[...] = _dot_f32(c * _sigmoid(c), w_ref[...]) + b_ref[...]


def _modulation(cond, w_ada, b_ada):
    rows, d = cond.shape
    n = w_ada.shape[1]
    tn = 512
    return pl.pallas_call(
        _mod_kernel,
        grid=(n // tn,),
        in_specs=[pl.BlockSpec((rows, d), lambda j: (0, 0)),
                  pl.BlockSpec((d, tn), lambda j: (0, j)),
                  pl.BlockSpec((1, tn), lambda j: (0, j))],
        out_specs=pl.BlockSpec((rows, tn), lambda j: (0, j)),
        out_shape=jax.ShapeDtypeStruct((rows, n), F32),
        compiler_params=_params("arbitrary"),
        name="modulation",
    )(cond, w_ada, b_ada.reshape(1, n))


def _inproj_kernel(x_ref, g_ref, sc_ref, sh_ref, w1_ref, w2_ref, z1_ref, z2_ref, u_ref):
    x = x_ref[0]
    ms = jnp.mean(x * x, axis=-1, keepdims=True)
    h = x * lax.rsqrt(ms + EPS) * g_ref[...]
    h = (h * (1.0 + sc_ref[0]) + sh_ref[0]).astype(BF16)
    z1_ref[0] = jnp.dot(h, w1_ref[...], preferred_element_type=F32).astype(z1_ref.dtype)
    z2 = jnp.dot(h, w2_ref[...], preferred_element_type=F32)
    z2_ref[0] = z2.astype(z2_ref.dtype)
    nblk, _, nch = u_ref.shape[0], u_ref.shape[1], u_ref.shape[2]
    w = z2.shape[1] // 3
    for j in range(nblk):
        u_ref[j, 0] = z2[:, w + LANES * j:w + LANES * (j + 1)].reshape(nch, S5_CHUNK, LANES).astype(u_ref.dtype)


def _inproj(x, norm_g, scale, shift, w1, w2):
    b, l, d = x.shape
    n1, n2 = w1.shape[1], w2.shape[1]
    tm = min(l, 512)
    nblk = n2 // 3 // LANES
    return pl.pallas_call(
        _inproj_kernel,
        grid=(b, l // tm),
        in_specs=[pl.BlockSpec((1, tm, d), lambda i, j: (i, j, 0)),
                  pl.BlockSpec((1, d), lambda i, j: (0, 0)),
                  pl.BlockSpec((1, 1, d), lambda i, j: (i, 0, 0)),
                  pl.BlockSpec((1, 1, d), lambda i, j: (i, 0, 0)),
                  pl.BlockSpec((d, n1), lambda i, j: (0, 0)),
                  pl.BlockSpec((d, n2), lambda i, j: (0, 0))],
        out_specs=[pl.BlockSpec((1, tm, n1), lambda i, j: (i, j, 0)),
                   pl.BlockSpec((1, tm, n2), lambda i, j: (i, j, 0)),
                   pl.BlockSpec((nblk, 1, tm // S5_CHUNK, S5_CHUNK, LANES), lambda i, j: (0, i, j, 0, 0))],
        out_shape=[jax.ShapeDtypeStruct((b, l, n1), BF16),
                   jax.ShapeDtypeStruct((b, l, n2), BF16),
                   jax.ShapeDtypeStruct((nblk, b, l // S5_CHUNK, S5_CHUNK, LANES), BF16)],
        compiler_params=_params("arbitrary", "arbitrary"),
        name="inproj",
    )(x, norm_g.reshape(1, d), scale, shift, w1, w2)


def _prep_body(z, up, down, mu_ref, w0_ref, w2_ref, a0_ref, a2_ref, kk_ref, ka_ref, rk_ref, e_ref, outs,
               grid2d):
    t, sw = z.shape
    dr = (sw - 4 * LORA) // 3
    row = lax.broadcasted_iota(jnp.int32, (t, 1), 0)
    slot = lax.broadcasted_iota(jnp.int32, (1, sw), 1) & 3
    prev = pltpu.roll(z, 1, 0)
    nxt = pltpu.roll(z, t - 1, 0)
    if grid2d:
        col = row & (GRID_W - 1)
        prev = jnp.where(col == 0, 0.0, prev)
        nxt = jnp.where(col == GRID_W - 1, 0.0, nxt)
        sh = jnp.where(slot == 0, prev, jnp.where(slot == 1, nxt, jnp.where(slot == 2, up, down)))
    else:
        prev = jnp.where(row == 0, 0.0, prev)
        nxt = jnp.where(row == t - 1, 0.0, nxt)
        sh = jnp.where((slot & 1) == 0, prev, nxt)
    zs = z + mu_ref[...] * (sh - z)

    r = zs[:, 0:dr]
    k = zs[:, dr:2 * dr]
    v = zs[:, 2 * dr:3 * dr]
    kk = k * kk_ref[...]
    ss = _segsum(kk * kk, e_ref)
    kk = kk / jnp.maximum(jnp.sqrt(ss), 1e-12)
    r_o, v_o, aa_o, w_o, kd_o, bb_o, bonus_o = outs
    r_o[0] = r.astype(r_o.dtype)
    v_o[0] = v.astype(v_o.dtype)
    aa_o[0] = (-kk).astype(aa_o.dtype)
    ksum = None
    for d in range(2):
        zw = zs[:, 3 * dr + LORA * d:3 * dr + LORA * (d + 1)]
        za = zs[:, 3 * dr + 2 * LORA + LORA * d:3 * dr + 2 * LORA + LORA * (d + 1)]
        wl = w0_ref[d:d + 1, :] + _dot_bf16(jnp.tanh(zw), w2_ref[d])
        w_o[d, 0] = -EXP_M05 * _sigmoid(wl)
        asig = _sigmoid(a0_ref[d:d + 1, :] + _dot_bf16(za, a2_ref[d]))
        kd = k * (1.0 + (asig - 1.0) * ka_ref[...])
        kd_o[d, 0] = kd.astype(kd_o.dtype)
        bb_o[d, 0] = (kk * asig).astype(bb_o.dtype)
        ksum = kd if ksum is None else ksum + kd
    bonus_o[0] = (_segsum(r * (0.5 * ksum) * rk_ref[...], e_ref) * v).astype(bonus_o.dtype)


def _prep2d_kernel(zc_ref, zu_ref, zd_ref, *rest):
    params, outs = rest[:9], rest[9:]
    j = pl.program_id(1)
    nj = pl.num_programs(1)
    z = zc_ref[0].astype(F32)
    t = z.shape[0]
    row = lax.broadcasted_iota(jnp.int32, (t, 1), 0)
    up = jnp.concatenate([zu_ref[0].astype(F32), z[:t - GRID_W]], axis=0)
    up = jnp.where(jnp.logical_and(j == 0, row < GRID_W), 0.0, up)
    down = jnp.concatenate([z[GRID_W:], zd_ref[0].astype(F32)], axis=0)
    down = jnp.where(jnp.logical_and(j == nj - 1, row >= t - GRID_W), 0.0, down)
    _prep_body(z, up, down, *params, outs, grid2d=True)


def _prep1d_kernel(zc_ref, *rest):
    params, outs = rest[:9], rest[9:]
    _prep_body(zc_ref[0].astype(F32), None, None, *params, outs, grid2d=False)


def _prep(z1, pp, grid2d):
    b, l, sw = z1.shape
    dr = (sw - 4 * LORA) // 3
    tt = 256 if grid2d else l
    nj = l // tt
    rb = tt // GRID_W
    const = lambda *shape: pl.BlockSpec(shape, lambda i, j: (0,) * len(shape))
    p_specs = [const(1, sw), const(2, dr), const(2, LORA, dr), const(2, dr), const(2, LORA, dr),
               const(1, dr), const(1, dr), const(1, dr), const(dr, dr)]
    cur = pl.BlockSpec((1, tt, sw), lambda i, j: (i, j, 0))
    if grid2d:
        nrow = l // GRID_W
        in_specs = [cur,
                    pl.BlockSpec((1, GRID_W, sw), lambda i, j: (i, jnp.maximum(j * rb - 1, 0), 0)),
                    pl.BlockSpec((1, GRID_W, sw), lambda i, j: (i, jnp.minimum((j + 1) * rb, nrow - 1), 0))]
        args = (z1, z1, z1)
        body = _prep2d_kernel
    else:
        in_specs = [cur]
        args = (z1,)
        body = _prep1d_kernel
    o1 = pl.BlockSpec((1, tt, dr), lambda i, j: (i, j, 0))
    o2 = pl.BlockSpec((2, 1, tt, dr), lambda i, j: (0, i, j, 0))
    s1 = jax.ShapeDtypeStruct((b, l, dr), BF16)
    s2 = jax.ShapeDtypeStruct((2, b, l, dr), BF16)
    s2w = jax.ShapeDtypeStruct((2, b, l, dr), F32)
    return pl.pallas_call(
        body,
        grid=(b, nj),
        in_specs=in_specs + p_specs,
        out_specs=[o1, o1, o1, o2, o2, o2, o1],
        out_shape=[s1, s1, s1, s2w, s2, s2, s1],
        compiler_params=_params("arbitrary", "arbitrary"),
        name="prep2d" if grid2d else "prep1d",
    )(*args, *pp)


def _dot_nt(a, b):
    return lax.dot_general(a, b, (((1,), (1,)), ((), ())), preferred_element_type=F32)


def _dot_tn(a, b):
    return lax.dot_general(a, b, (((0,), (0,)), ((), ())), preferred_element_type=F32)


def _wkv_operands(lw_ref, kd_ref, bb_ref, aa_ref, r_ref, v_ref, bi, backward):
    c = aa_ref.shape[1]
    row = lax.broadcasted_iota(jnp.int32, (c, c), 0)
    col = lax.broadcasted_iota(jnp.int32, (c, c), 1)
    lw = lw_ref[0, bi]
    lw_hi = lw.astype(BF16)
    lw_lo = (lw - lw_hi.astype(F32)).astype(BF16)
    lc = (col >= row if backward else col <= row).astype(BF16)
    cum = jnp.dot(lc, lw_hi, preferred_element_type=F32) + jnp.dot(lc, lw_lo, preferred_element_type=F32)
    tot = cum[0:1, :] if backward else cum[c - 1:c, :]
    pinv = jnp.exp(-cum)
    pend = jnp.exp(tot - cum)
    bb, kd = bb_ref[0, bi].astype(F32), kd_ref[0, bi].astype(F32)
    rt = r_ref[bi].astype(F32) * jnp.exp(cum)
    return dict(
        lw_hi=lw_hi, lw_lo=lw_lo, at=(aa_ref[bi].astype(F32) * jnp.exp(cum - lw)).astype(BF16),
        rt=rt, rt_b=rt.astype(BF16),
        bt=(bb * pinv).astype(BF16), kt=(kd * pinv).astype(BF16),
        be=(bb * pend).astype(BF16), ke=(kd * pend).astype(BF16), v=v_ref[bi].astype(BF16))


def _wkv_kernel(lwf_ref, kdf_ref, bbf_ref, aaf_ref, rf_ref, vf_ref, lwb_ref, kdb_ref, bbb_ref, aab_ref, rb_ref,
                vb_ref, s0_ref, yf_ref, yb_ref, sout_ref, s_scr):
    ci = pl.program_id(1)
    nb, c, width = aaf_ref.shape
    gw = WKV_PACK * HEAD
    ng = width // gw
    assert c == HEAD and width % gw == 0

    @pl.when(ci == 0)
    def _():
        s_scr[...] = s0_ref[...]

    ops = {(0, bi): _wkv_operands(lwf_ref, kdf_ref, bbf_ref, aaf_ref, rf_ref, vf_ref, bi, False) for bi in range(nb)}
    ops.update({(1, bi): _wkv_operands(lwb_ref, kdb_ref, bbb_ref, aab_ref, rb_ref, vb_ref, bi, True)
                for bi in range(nb)})
    chains = [(d, bi, gi) for gi in range(ng) for bi in range(nb) for d in range(2)]
    n = range(len(chains))
    cut = lambda name: [ops[d, bi][name][:, gi * gw:(gi + 1) * gw] for d, bi, gi in chains]
    at, rt, rt_b, bt, kt, be, ke, v, lw_hi, lw_lo = (
        cut(k) for k in ("at", "rt", "rt_b", "bt", "kt", "be", "ke", "v", "lw_hi", "lw_lo"))
    mm = lambda a, b: jnp.dot(a, b, preferred_element_type=F32)

    r64 = lax.broadcasted_iota(jnp.int32, (c, gw), 0)
    c64 = lax.broadcasted_iota(jnp.int32, (c, gw), 1) & (HEAD - 1)
    eye_f = (r64 == c64).astype(F32)
    causal = {0: (c64 < r64, c64 <= r64), 1: (c64 > r64, c64 >= r64)}
    strict = [causal[d][0] for d, _, _ in chains]
    incl = [causal[d][1] for d, _, _ in chains]
    same_head = (lax.broadcasted_iota(jnp.int32, (gw, gw), 0) // HEAD
                 == lax.broadcasted_iota(jnp.int32, (gw, gw), 1) // HEAD)

    def bd(xp):
        return jnp.where(same_head, jnp.concatenate([xp] * (gw // c), axis=0), jnp.zeros((), xp.dtype))

    def diag(o):
        head = lax.broadcasted_iota(jnp.int32, (c, gw), 1) // HEAD
        out = o[:c]
        for h in range(1, gw // c):
            out = jnp.where(head == h, o[h * c:(h + 1) * c], out)
        return out

    rows = lambda a, b: jnp.concatenate([a, b], axis=0)
    cols = lambda a, b: jnp.concatenate([a, b], axis=1)
    g = [_dot_nt(rows(at[i], rt_b[i]), rows(bd(bt[i]), bd(kt[i]))) for i in n]
    x = [jnp.where(strict[i], g[i][:c, :gw], 0.0) for i in n]
    a_ak = [jnp.where(strict[i], g[i][:c, gw:], 0.0).astype(BF16) for i in n]
    m_rb = [jnp.where(incl[i], g[i][c:, :gw], 0.0).astype(BF16) for i in n]
    m_rk = [jnp.where(incl[i], g[i][c:, gw:], 0.0).astype(BF16) for i in n]
    t = [eye_f + x[i] for i in n]
    p = [x[i].astype(BF16) for i in n]
    p = [mm(p[i], bd(p[i])).astype(BF16) for i in n]
    for _ in range((c - 1).bit_length() - 2):
        o = [mm(rows(p[i], t[i].astype(BF16)), bd(p[i])) for i in n]
        p = [o[i][:c].astype(BF16) for i in n]
        t = [t[i] + o[i][c:] for i in n]
    t = [t[i] + mm(t[i].astype(BF16), bd(p[i])) for i in n]
    ov = [mm(rows(a_ak[i], m_rk[i]), bd(v[i])) for i in n]
    za = [mm(t[i].astype(BF16), cols(bd(at[i]), bd(ov[i][:c].astype(BF16)))).astype(BF16) for i in n]
    w2 = [mm(m_rb[i], cols(bd(za[i][:, :gw]), bd(za[i][:, gw:]))) for i in n]
    yp = [w2[i][:, gw:] + ov[i][c:] for i in n]
    tn = [_dot_tn(rows(be[i], ke[i]), rows(za[i], cols(jnp.zeros((c, gw), BF16), v[i]))) for i in n]
    gd = [diag(tn[i][:, :gw]) for i in n]
    hv = [diag(tn[i][:, gw:]) for i in n]
    ones = jnp.ones((c, gw), BF16)
    pc = [jnp.exp(diag(_dot_tn(lw_hi[i], ones) + _dot_tn(lw_lo[i], ones))) for i in n]
    s = [s_scr[d, bi, gi] for d, bi, gi in chains]
    os_ = [mm(rows((rt[i] + w2[i][:, :gw]).astype(BF16), gd[i].astype(BF16)), bd(s[i].astype(BF16))) for i in n]
    y = [os_[i][:c] + yp[i] for i in n]
    for i, (d, bi, gi) in enumerate(chains):
        s_scr[d, bi, gi] = pc[i] * s[i] + os_[i][c:] + hv[i]
    for d, y_ref in enumerate((yf_ref, yb_ref)):
        for bi in range(nb):
            y_ref[bi] = jnp.concatenate([y[chains.index((d, bi, gi))] for gi in range(ng)],
                                        axis=1).astype(y_ref.dtype)

    @pl.when(ci == pl.num_programs(1) - 1)
    def _():
        sout_ref[...] = s_scr[...]


def _wkv(lw, kd, bb, aa, r, v, s0):
    _, b, l, width = lw.shape
    c = WKV_CHUNK
    nc = l // c
    nb = WKV_ROWS
    dirs = lambda d: pl.BlockSpec((1, nb, c, width), lambda i, j: (d, i, (nc - 1 - j) if d else j, 0))
    both = lambda d: pl.BlockSpec((nb, c, width), lambda i, j: (i, (nc - 1 - j) if d else j, 0))
    stspec = pl.BlockSpec((2, nb) + s0.shape[2:], lambda i, j: (0, i, 0, 0, 0))
    return pl.pallas_call(
        _wkv_kernel,
        grid=(b // nb, nc),
        in_specs=[dirs(0), dirs(0), dirs(0), both(0), both(0), both(0),
                  dirs(1), dirs(1), dirs(1), both(1), both(1), both(1), stspec],
        out_specs=[both(0), both(1), stspec],
        out_shape=[jax.ShapeDtypeStruct((b, l, width), BF16), jax.ShapeDtypeStruct((b, l, width), BF16),
                   jax.ShapeDtypeStruct(s0.shape, F32)],
        scratch_shapes=[pltpu.VMEM((2, nb) + s0.shape[2:], F32)],
        compiler_params=_params("arbitrary", "arbitrary"),
        name="wkv",
    )(lw, kd, bb, aa, r, v, lw, kd, bb, aa, r, v, s0)


def _s5_kernel(*refs, nb, reverse, toeplitz):
    if toeplitz:
        (u_ref, lag_ref, win_ref, wout_ref, a_ref, h0_ref, y_ref, hout_ref,
         x_scr, hh_scr, hre_scr, him_scr, win_scr, wout_scr, wt_scr) = refs
    else:
        (u_ref, yprev_ref, win_ref, wout_ref, a_ref, h0_ref, y_ref, hout_ref,
         x_scr, hh_scr, hre_scr, him_scr, win_scr, wout_scr) = refs
    i = pl.program_id(1)
    ns = hre_scr.shape[1]
    t = win_ref.shape[1]

    @pl.when(i == 0)
    def _():
        hre_scr[...] = h0_ref[0, 0]
        him_scr[...] = h0_ref[0, 1]
        lane_group = (lax.broadcasted_iota(jnp.int32, (S5_GROUP, 2 * ns), 1) % ns) // S5_STATE
        for s in range(t):
            for g in range(LANES // S5_GROUP):
                r0 = s * LANES + g * S5_GROUP
                own = lane_group == g
                win_scr[r0:r0 + S5_GROUP, :] = jnp.where(own, win_ref[0, s], jnp.zeros((), BF16))
                wout_scr[r0:r0 + S5_GROUP, :] = jnp.where(own, wout_ref[0, s], jnp.zeros((), BF16))
        if toeplitz:
            for sig in range(t):
                for tau in range(t):
                    wt_scr[sig * LANES:(sig + 1) * LANES, tau * LANES:(tau + 1) * LANES] = lag_ref[0, tau - sig + t - 1]

    u = u_ref[0].astype(BF16)
    x_scr[...] = jnp.dot(u, win_scr[...], preferred_element_type=F32)
    ar, ai = a_ref[0, 0:1, :], a_ref[0, 1:2, :]
    nch = x_scr.shape[0] // nb

    def body(k, carry):
        hr, hi = carry
        c = nch - 1 - k if reverse else k
        rows = pl.ds(pl.multiple_of(c * nb, nb), nb)
        hh_scr[rows, 0:ns] = hr
        hh_scr[rows, ns:2 * ns] = hi
        return (hr * ar - hi * ai + x_scr[rows, 0:ns], hr * ai + hi * ar + x_scr[rows, ns:2 * ns])

    hr, hi = lax.fori_loop(0, nch, body, (hre_scr[...], him_scr[...]))
    hre_scr[...] = hr
    him_scr[...] = hi
    y = _dot_nt(hh_scr[...].astype(BF16), wout_scr[...])
    if toeplitz:
        y = y + jnp.dot(u, wt_scr[...], preferred_element_type=F32)
    else:
        y = y + yprev_ref[0].astype(F32)
    y_ref[0] = y.astype(y_ref.dtype)

    @pl.when(i == pl.num_programs(1) - 1)
    def _():
        hout_ref[0, 0] = hr
        hout_ref[0, 1] = hi


def _s5_scan(u, first, win, wout, acoef, h0, nb, reverse):
    nblk, rows, kw = u.shape
    ns = win.shape[3] // 2
    rt = min(rows, S5_ROWS)
    nt = rows // rt
    tile = pl.BlockSpec((1, rt, kw), lambda g, i: (g, (nt - 1 - i) if reverse else i, 0))
    per_blk = lambda a: pl.BlockSpec((1,) + a.shape[1:], lambda g, i: (g,) + (0,) * (a.ndim - 1))
    weights = [win, wout, acoef, h0]
    toeplitz = not reverse
    return pl.pallas_call(
        functools.partial(_s5_kernel, nb=nb, reverse=reverse, toeplitz=toeplitz),
        grid=(nblk, nt),
        in_specs=[tile, per_blk(first) if toeplitz else tile] + [per_blk(a) for a in weights],
        out_specs=[tile, per_blk(h0)],
        out_shape=[jax.ShapeDtypeStruct((nblk, rows, kw), BF16), jax.ShapeDtypeStruct(h0.shape, F32)],
        scratch_shapes=[pltpu.VMEM((rt, 2 * ns), F32), pltpu.VMEM((rt, 2 * ns), F32),
                        pltpu.VMEM((nb, ns), F32), pltpu.VMEM((nb, ns), F32),
                        pltpu.VMEM((kw, 2 * ns), BF16), pltpu.VMEM((kw, 2 * ns), BF16)]
        + ([pltpu.VMEM((kw, kw), BF16)] if toeplitz else []),
        compiler_params=_params("arbitrary", "arbitrary"),
        name="s5_bwd" if reverse else "s5_fwd",
    )(u, first, *weights)


def _s5_matrices(lam_re, lam_im, log_dt, b_re, b_im, c_re, c_im):
    hp = lax.Precision.HIGHEST
    t = S5_CHUNK
    dt = jnp.exp(log_dt.astype(F32))[..., None]
    lr, li = lam_re.astype(F32), lam_im.astype(F32)
    j = jnp.arange(t + 1, dtype=F32)[:, None, None, None]
    mag = jnp.exp(j * dt * lr)
    pr, pi = mag * jnp.cos(j * dt * li), mag * jnp.sin(j * dt * li)
    ar, ai = pr[1], pi[1]
    den = lr * lr + li * li
    fr = ((ar - 1.0) * lr + ai * li) / den
    fi = (ai * lr - (ar - 1.0) * li) / den
    bbr = fr[..., None] * b_re - fi[..., None] * b_im
    bbi = fr[..., None] * b_im + fi[..., None] * b_re
    car = c_re[None, None] * pr[:, :, :, None, :] - c_im[None, None] * pi[:, :, :, None, :]
    cai = c_re[None, None] * pi[:, :, :, None, :] + c_im[None, None] * pr[:, :, :, None, :]
    kj = (jnp.einsum('jdgcp,dgpe->jdgce', car, bbr, precision=hp)
          - jnp.einsum('jdgcp,dgpe->jdgce', cai, bbi, precision=hp))
    g = lr.shape[1]
    gpb = LANES // S5_GROUP
    nblk = g // gpb
    ns = gpb * S5_STATE
    own_chan = (jnp.arange(gpb)[:, None, None] == (jnp.arange(LANES) // S5_GROUP)[None, None, :])

    kl = jnp.concatenate([kj[t - 1:0:-1, 1], (kj[0, 0] + kj[0, 1])[None], kj[1:t, 0]], axis=0)
    kl = kl.reshape(2 * t - 1, nblk, gpb, S5_GROUP, S5_GROUP).transpose(0, 1, 4, 2, 3)
    kl = kl.reshape(2 * t - 1, nblk, 1, S5_GROUP, LANES)
    lagblk = jnp.where(own_chan, kl, 0.0).transpose(1, 0, 2, 3, 4).reshape(nblk, 2 * t - 1, LANES, LANES)

    def state_in(d, pw):
        prs = pr[pw, d].reshape(t, nblk, 1, ns).transpose(1, 0, 2, 3)
        pis = pi[pw, d].reshape(t, nblk, 1, ns).transpose(1, 0, 2, 3)
        lay = lambda q: q.reshape(nblk, gpb, S5_STATE, S5_GROUP).transpose(0, 3, 1, 2).reshape(nblk, 1, S5_GROUP, ns)
        br, bi = lay(bbr[d]), lay(bbi[d])
        return jnp.concatenate([prs * br - pis * bi, prs * bi + pis * br], axis=-1).astype(BF16)

    def state_out(d, pw):
        lay = lambda q: q.reshape(t, nblk, gpb, S5_GROUP, S5_STATE).transpose(1, 0, 3, 2, 4).reshape(
            nblk, t, S5_GROUP, ns)
        return jnp.concatenate([lay(car[pw, d]), lay(-cai[pw, d])], axis=-1).astype(BF16)

    coef = lambda d: jnp.stack([pr[t, d].reshape(nblk, ns), pi[t, d].reshape(nblk, ns)], axis=1)
    fwd = (lagblk.astype(BF16), state_in(0, t - 1 - jnp.arange(t)), state_out(0, jnp.arange(t) + 1), coef(0))
    bwd = (state_in(1, jnp.arange(t)), state_out(1, t - jnp.arange(t)), coef(1))
    return fwd, bwd


def _s5_rows(u5):
    nblk, b, nc, t, w = u5.shape
    return u5.transpose(0, 2, 1, 3, 4).reshape(nblk, nc * b, t * w)


def _s5_unrows(y, b):
    nblk, rows, kw = y.shape
    return y.reshape(nblk, rows // b, b, S5_CHUNK, kw // S5_CHUNK).transpose(0, 2, 1, 3, 4)


def _out_kernel(x_ref, gate_ref, yf_ref, yb_ref, bonus_ref, gr_ref, u_ref, gs_ref, ys_ref, lg_ref, lb_ref, d_ref,
                wg_ref, wo_ref, fg_ref, e_ref, o_ref):
    dr = bonus_ref.shape[2]
    y = yf_ref[0].astype(F32) + yb_ref[0].astype(F32)
    mean = _segsum(y, e_ref) * (1.0 / HEAD)
    dev = y - mean
    var = _segsum(dev * dev, e_ref) * (1.0 / HEAD)
    yn = dev * lax.rsqrt(var + LN_X_EPS) * lg_ref[...] + lb_ref[...]
    gr = gr_ref[0].astype(F32)
    y_r = (yn + bonus_ref[0].astype(F32)) * (gr * _sigmoid(gr))

    tm = u_ref.shape[1]
    ys = jnp.concatenate([ys_ref[j, 0].reshape(tm, LANES) for j in range(ys_ref.shape[0])], axis=1)
    s = ys.astype(F32) + d_ref[...] * u_ref[0].astype(F32)
    s = 0.5 * s * (1.0 + jnp.tanh(math.sqrt(2.0 / math.pi) * (s + 0.044715 * (s * s * s))))
    gl = jnp.dot(s.astype(BF16), wg_ref[...], preferred_element_type=F32)
    ds = gl.shape[1] // 2
    gs = gs_ref[0].astype(F32)
    y_s = gl[:, :ds] * _sigmoid(gl[:, ds:]) * (gs * _sigmoid(gs))

    out = (jnp.dot(y_r.astype(BF16), wo_ref[0:dr, :], preferred_element_type=F32)
           + jnp.dot(y_s.astype(BF16), wo_ref[dr:, :], preferred_element_type=F32))
    xo = x_ref[0] + gate_ref[0] * out
    ms = jnp.mean(xo * xo, axis=-1, keepdims=True)
    o_ref[0] = (xo * lax.rsqrt(ms + EPS) * fg_ref[...]).astype(o_ref.dtype)


def _out_stage(x, gate, yf, yb, bonus, z2, ys, lnx_g, lnx_b, s5_d, w_glu, w_out, final_g, e):
    b, l, d = x.shape
    dr = bonus.shape[2]
    ds = ys.shape[0] * LANES
    assert dr == ds
    tm = 512
    tok = lambda w, cb: pl.BlockSpec((1, tm, w), lambda i, j: (i, j, cb))
    const = lambda *shape: pl.BlockSpec(shape, lambda i, j: (0,) * len(shape))
    return pl.pallas_call(
        _out_kernel,
        grid=(b, l // tm),
        in_specs=[tok(d, 0), pl.BlockSpec((1, 1, d), lambda i, j: (i, 0, 0)),
                  tok(dr, 0), tok(dr, 0), tok(dr, 0), tok(dr, 0), tok(ds, 1), tok(ds, 2),
                  pl.BlockSpec((ys.shape[0], 1, tm // S5_CHUNK, S5_CHUNK, LANES), lambda i, j: (0, i, j, 0, 0)),
                  const(1, dr), const(1, dr), const(1, ds), const(ds, 2 * ds), const(dr + ds, d),
                  const(1, d), const(dr, dr)],
        out_specs=tok(d, 0),
        out_shape=jax.ShapeDtypeStruct((b, l, d), x.dtype),
        compiler_params=_params("arbitrary", "arbitrary"),
        name="out_stage",
    )(x, gate, yf, yb, bonus, z2, z2, z2, ys, lnx_g.reshape(1, dr), lnx_b.reshape(1, dr), s5_d.reshape(1, ds),
      w_glu.astype(BF16), w_out.astype(BF16), final_g.reshape(1, d), e)


def kernel(x, c, ctx, c_ctx, norm_g, w_ada, b_ada, w_in, mu_shift, rwkv_w0, rwkv_w2, rwkv_a0, rwkv_a2, rwkv_k_k, rwkv_k_a, rwkv_r_k, lnx_g, lnx_b, s5_lam_re, s5_lam_im, s5_log_dt, s5_b_re, s5_b_im, s5_c_re, s5_c_im, s5_d, s5_w_glu, w_out, final_g):
    assert norm_g.shape[0] == 1, "one layer"
    b, l, d = x.shape
    lc = ctx.shape[1]
    dr = rwkv_k_k.shape[1]
    ds = s5_d.shape[1]
    sw = mu_shift.shape[1]
    assert sw == 3 * dr + 4 * LORA and l % (4 * GRID_W) == 0 and lc % WKV_CHUNK == 0 and b % 8 == 0

    cond = jnp.zeros((2 * b, d), F32).at[:b].set(c).at[b].set(c_ctx)
    m = _modulation(cond, w_ada[0], b_ada[0])
    shift, scale, gate = m[:b, :d], m[:b, d:2 * d], m[:b, 2 * d:]
    cshift = jnp.broadcast_to(m[b, :d], (b, d))
    cscale = jnp.broadcast_to(m[b, d:2 * d], (b, d))

    w1 = w_in[0][:, :sw].astype(BF16)
    w2 = w_in[0][:, sw:].astype(BF16)
    z1c, _, u5c = _inproj(ctx, norm_g[0], cscale[:, None], cshift[:, None], w1, w2)
    z1, z2, u5 = _inproj(x, norm_g[0], scale[:, None], shift[:, None], w1, w2)

    hid = jnp.arange(dr) // HEAD
    e = (hid[:, None] == hid[None, :]).astype(BF16)
    pp = (mu_shift, rwkv_w0[0], rwkv_w2[0], rwkv_a0[0], rwkv_a2[0], rwkv_k_k, rwkv_k_a,
          rwkv_r_k[0].reshape(1, dr), e)
    rc, vc, aac, wc, kdc, bbc, _ = _prep(z1c, pp, grid2d=False)
    rx, vx, aax, wx, kdx, bbx, bonus = _prep(z1, pp, grid2d=True)

    s0 = jnp.zeros((2, b, dr // (WKV_PACK * HEAD), HEAD, WKV_PACK * HEAD), F32)
    _, _, s_ctx = _wkv(wc, kdc, bbc, aac, rc, vc, s0)
    yf, yb, _ = _wkv(wx, kdx, bbx, aax, rx, vx, s_ctx)

    s5f, s5b = _s5_matrices(s5_lam_re[0], s5_lam_im[0], s5_log_dt[0], s5_b_re[0], s5_b_im[0],
                            s5_c_re[0], s5_c_im[0])
    uc, ux = _s5_rows(u5c), _s5_rows(u5)
    h0 = jnp.zeros((uc.shape[0], 2, b, s5f[3].shape[2]), F32)
    ysc, hcf = _s5_scan(uc, *s5f, h0, b, False)
    _, hcb = _s5_scan(uc, ysc, *s5b, h0, b, True)
    ysf, _ = _s5_scan(ux, *s5f, hcf, b, False)
    ysfb, _ = _s5_scan(ux, ysf, *s5b, hcb, b, True)
    ys = _s5_unrows(ysfb, b)

    return _out_stage(x, gate[:, None], yf, yb, bonus, z2, ys, lnx_g[0], lnx_b[0], s5_d[0], s5_w_glu[0],
                      w_out[0], final_g, e)
```

```python
import functools
import math

import jax
import jax.numpy as jnp
from jax import lax
from jax.experimental import pallas as pl
from jax.experimental.pallas import tpu as pltpu

F32 = jnp.float32
BF16 = jnp.bfloat16

GRID_W = 64
HEAD = 64
LORA = 64
S5_GROUP = 16
S5_STATE = 64
S5_CHUNK = 16
S5_ROWS = 256
WKV_CHUNK = 64
WKV_ROWS = 4
WKV_PACK = 2
EPS = 1e-6
LN_X_EPS = 64e-5
EXP_M05 = math.exp(-0.5)

LANES = 128
VMEM_LIMIT = 56 * 1024 * 1024


def _params(*sem):
    return pltpu.CompilerParams(dimension_semantics=sem, vmem_limit_bytes=VMEM_LIMIT)


def _sigmoid(x):
    return 1.0 / (1.0 + jnp.exp(-x))


def _dot_bf16(a, b):
    return jnp.dot(a.astype(BF16), b.astype(BF16), preferred_element_type=F32)


def _dot_f32(a, b):
    return jnp.dot(a, b, preferred_element_type=F32, precision=lax.Precision.HIGHEST)


def _segsum(x, e_ref):
    return jnp.dot(x.astype(BF16), e_ref[...], preferred_element_type=F32)


def _mod_kernel(c_ref, w_ref, b_ref, o_ref):
    c = c_ref[...]
    o_ref[...] = _dot_f32(c * _sigmoid(c), w_ref[...]) + b_ref[...]


def _modulation(cond, w_ada, b_ada):
    rows, d = cond.shape
    n = w_ada.shape[1]
    tn = 512
    return pl.pallas_call(
        _mod_kernel,
        grid=(n // tn,),
        in_specs=[pl.BlockSpec((rows, d), lambda j: (0, 0)),
                  pl.BlockSpec((d, tn), lambda j: (0, j)),
                  pl.BlockSpec((1, tn), lambda j: (0, j))],
        out_specs=pl.BlockSpec((rows, tn), lambda j: (0, j)),
        out_shape=jax.ShapeDtypeStruct((rows, n), F32),
        compiler_params=_params("arbitrary"),
        name="modulation",
    )(cond, w_ada, b_ada.reshape(1, n))


def _inproj_kernel(x_ref, g_ref, sc_ref, sh_ref, w1_ref, w2_ref, z1_ref, z2_ref, u_ref):
    x = x_ref[0]
    ms = jnp.mean(x * x, axis=-1, keepdims=True)
    h = x * lax.rsqrt(ms + EPS) * g_ref[...]
    h = (h * (1.0 + sc_ref[0]) + sh_ref[0]).astype(BF16)
    z1_ref[0] = jnp.dot(h, w1_ref[...], preferred_element_type=F32).astype(z1_ref.dtype)
    z2 = jnp.dot(h, w2_ref[...], preferred_element_type=F32)
    z2_ref[0] = z2.astype(z2_ref.dtype)
    nblk, _, nch = u_ref.shape[0], u_ref.shape[1], u_ref.shape[2]
    w = z2.shape[1] // 3
    for j in range(nblk):
        u_ref[j, 0] = z2[:, w + LANES * j:w + LANES * (j + 1)].reshape(nch, S5_CHUNK, LANES).astype(u_ref.dtype)


def _inproj(x, norm_g, scale, shift, w1, w2):
    b, l, d = x.shape
    n1, n2 = w1.shape[1], w2.shape[1]
    tm = min(l, 512)
    nblk = n2 // 3 // LANES
    return pl.pallas_call(
        _inproj_kernel,
        grid=(b, l // tm),
        in_specs=[pl.BlockSpec((1, tm, d), lambda i, j: (i, j, 0)),
                  pl.BlockSpec((1, d), lambda i, j: (0, 0)),
                  pl.BlockSpec((1, 1, d), lambda i, j: (i, 0, 0)),
                  pl.BlockSpec((1, 1, d), lambda i, j: (i, 0, 0)),
                  pl.BlockSpec((d, n1), lambda i, j: (0, 0)),
                  pl.BlockSpec((d, n2), lambda i, j: (0, 0))],
        out_specs=[pl.BlockSpec((1, tm, n1), lambda i, j: (i, j, 0)),
                   pl.BlockSpec((1, tm, n2), lambda i, j: (i, j, 0)),
                   pl.BlockSpec((nblk, 1, tm // S5_CHUNK, S5_CHUNK, LANES), lambda i, j: (0, i, j, 0, 0))],
        out_shape=[jax.ShapeDtypeStruct((b, l, n1), BF16),
                   jax.ShapeDtypeStruct((b, l, n2), BF16),
                   jax.ShapeDtypeStruct((nblk, b, l // S5_CHUNK, S5_CHUNK, LANES), BF16)],
        compiler_params=_params("arbitrary", "arbitrary"),
        name="inproj",
    )(x, norm_g.reshape(1, d), scale, shift, w1, w2)


def _prep_body(z, up, down, mu_ref, w0_ref, w2_ref, a0_ref, a2_ref, kk_ref, ka_ref, rk_ref, e_ref, outs,
               grid2d):
    t, sw = z.shape
    dr = (sw - 4 * LORA) // 3
    row = lax.broadcasted_iota(jnp.int32, (t, 1), 0)
    slot = lax.broadcasted_iota(jnp.int32, (1, sw), 1) & 3
    prev = pltpu.roll(z, 1, 0)
    nxt = pltpu.roll(z, t - 1, 0)
    if grid2d:
        col = row & (GRID_W - 1)
        prev = jnp.where(col == 0, 0.0, prev)
        nxt = jnp.where(col == GRID_W - 1, 0.0, nxt)
        sh = jnp.where(slot == 0, prev, jnp.where(slot == 1, nxt, jnp.where(slot == 2, up, down)))
    else:
        prev = jnp.where(row == 0, 0.0, prev)
        nxt = jnp.where(row == t - 1, 0.0, nxt)
        sh = jnp.where((slot & 1) == 0, prev, nxt)
    zs = z + mu_ref[...] * (sh - z)

    r = zs[:, 0:dr]
    k = zs[:, dr:2 * dr]
    v = zs[:, 2 * dr:3 * dr]
    kk = k * kk_ref[...]
    ss = _segsum(kk * kk, e_ref)
    kk = kk / jnp.maximum(jnp.sqrt(ss), 1e-12)
    r_o, v_o, aa_o, w_o, kd_o, bb_o, bonus_o = outs
    r_o[0] = r.astype(r_o.dtype)
    v_o[0] = v.astype(v_o.dtype)
    aa_o[0] = (-kk).astype(aa_o.dtype)
    ksum = None
    for d in range(2):
        zw = zs[:, 3 * dr + LORA * d:3 * dr + LORA * (d + 1)]
        za = zs[:, 3 * dr + 2 * LORA + LORA * d:3 * dr + 2 * LORA + LORA * (d + 1)]
        wl = w0_ref[d:d + 1, :] + _dot_bf16(jnp.tanh(zw), w2_ref[d])
        w_o[d, 0] = -EXP_M05 * _sigmoid(wl)
        asig = _sigmoid(a0_ref[d:d + 1, :] + _dot_bf16(za, a2_ref[d]))
        kd = k * (1.0 + (asig - 1.0) * ka_ref[...])
        kd_o[d, 0] = kd.astype(kd_o.dtype)
        bb_o[d, 0] = (kk * asig).astype(bb_o.dtype)
        ksum = kd if ksum is None else ksum + kd
    bonus_o[0] = (_segsum(r * (0.5 * ksum) * rk_ref[...], e_ref) * v).astype(bonus_o.dtype)


def _prep2d_kernel(zc_ref, zu_ref, zd_ref, *rest):
    params, outs = rest[:9], rest[9:]
    j = pl.program_id(1)
    nj = pl.num_programs(1)
    z = zc_ref[0].astype(F32)
    t = z.shape[0]
    row = lax.broadcasted_iota(jnp.int32, (t, 1), 0)
    up = jnp.concatenate([zu_ref[0].astype(F32), z[:t - GRID_W]], axis=0)
    up = jnp.where(jnp.logical_and(j == 0, row < GRID_W), 0.0, up)
    down = jnp.concatenate([z[GRID_W:], zd_ref[0].astype(F32)], axis=0)
    down = jnp.where(jnp.logical_and(j == nj - 1, row >= t - GRID_W), 0.0, down)
    _prep_body(z, up, down, *params, outs, grid2d=True)


def _prep1d_kernel(zc_ref, *rest):
    params, outs = rest[:9], rest[9:]
    _prep_body(zc_ref[0].astype(F32), None, None, *params, outs, grid2d=False)


def _prep(z1, pp, grid2d):
    b, l, sw = z1.shape
    dr = (sw - 4 * LORA) // 3
    tt = 256 if grid2d else l
    nj = l // tt
    rb = tt // GRID_W
    const = lambda *shape: pl.BlockSpec(shape, lambda i, j: (0,) * len(shape))
    p_specs = [const(1, sw), const(2, dr), const(2, LORA, dr), const(2, dr), const(2, LORA, dr),
               const(1, dr), const(1, dr), const(1, dr), const(dr, dr)]
    cur = pl.BlockSpec((1, tt, sw), lambda i, j: (i, j, 0))
    if grid2d:
        nrow = l // GRID_W
        in_specs = [cur,
                    pl.BlockSpec((1, GRID_W, sw), lambda i, j: (i, jnp.maximum(j * rb - 1, 0), 0)),
                    pl.BlockSpec((1, GRID_W, sw), lambda i, j: (i, jnp.minimum((j + 1) * rb, nrow - 1), 0))]
        args = (z1, z1, z1)
        body = _prep2d_kernel
    else:
        in_specs = [cur]
        args = (z1,)
        body = _prep1d_kernel
    o1 = pl.BlockSpec((1, tt, dr), lambda i, j: (i, j, 0))
    o2 = pl.BlockSpec((2, 1, tt, dr), lambda i, j: (0, i, j, 0))
    s1 = jax.ShapeDtypeStruct((b, l, dr), BF16)
    s2 = jax.ShapeDtypeStruct((2, b, l, dr), BF16)
    s2w = jax.ShapeDtypeStruct((2, b, l, dr), F32)
    return pl.pallas_call(
        body,
        grid=(b, nj),
        in_specs=in_specs + p_specs,
        out_specs=[o1, o1, o1, o2, o2, o2, o1],
        out_shape=[s1, s1, s1, s2w, s2, s2, s1],
        compiler_params=_params("arbitrary", "arbitrary"),
        name="prep2d" if grid2d else "prep1d",
    )(*args, *pp)


def _dot_nt(a, b):
    return lax.dot_general(a, b, (((1,), (1,)), ((), ())), preferred_element_type=F32)


def _dot_tn(a, b):
    return lax.dot_general(a, b, (((0,), (0,)), ((), ())), preferred_element_type=F32)


def _wkv_operands(lw_ref, kd_ref, bb_ref, aa_ref, r_ref, v_ref, bi, backward):
    c = aa_ref.shape[1]
    row = lax.broadcasted_iota(jnp.int32, (c, c), 0)
    col = lax.broadcasted_iota(jnp.int32, (c, c), 1)
    lw = lw_ref[0, bi]
    lw_hi = lw.astype(BF16)
    lw_lo = (lw - lw_hi.astype(F32)).astype(BF16)
    lc = (col >= row if backward else col <= row).astype(BF16)
    cum = jnp.dot(lc, lw_hi, preferred_element_type=F32) + jnp.dot(lc, lw_lo, preferred_element_type=F32)
    tot = cum[0:1, :] if backward else cum[c - 1:c, :]
    pinv = jnp.exp(-cum)
    pend = jnp.exp(tot - cum)
    bb, kd = bb_ref[0, bi].astype(F32), kd_ref[0, bi].astype(F32)
    rt = r_ref[bi].astype(F32) * jnp.exp(cum)
    return dict(
        tot=tot, at=(aa_ref[bi].astype(F32) * jnp.exp(cum - lw)).astype(BF16),
        rt=rt, rt_b=rt.astype(BF16),
        bt=(bb * pinv).astype(BF16), kt=(kd * pinv).astype(BF16),
        be=(bb * pend).astype(BF16), ke=(kd * pend).astype(BF16), v=v_ref[bi].astype(BF16))


def _wkv_kernel(lwf_ref, kdf_ref, bbf_ref, aaf_ref, rf_ref, vf_ref, lwb_ref, kdb_ref, bbb_ref, aab_ref, rb_ref,
                vb_ref, s0_ref, yf_ref, yb_ref, sout_ref, s_scr):
    ci = pl.program_id(1)
    nb, c, width = aaf_ref.shape
    gw = WKV_PACK * HEAD
    ng = width // gw
    assert c == HEAD and width % gw == 0

    @pl.when(ci == 0)
    def _():
        s_scr[...] = s0_ref[...]

    ops = {(0, bi): _wkv_operands(lwf_ref, kdf_ref, bbf_ref, aaf_ref, rf_ref, vf_ref, bi, False) for bi in range(nb)}
    ops.update({(1, bi): _wkv_operands(lwb_ref, kdb_ref, bbb_ref, aab_ref, rb_ref, vb_ref, bi, True)
                for bi in range(nb)})
    chains = [(d, bi, gi) for gi in range(ng) for bi in range(nb) for d in range(2)]
    n = range(len(chains))
    cut = lambda name: [ops[d, bi][name][:, gi * gw:(gi + 1) * gw] for d, bi, gi in chains]
    at, rt, rt_b, bt, kt, be, ke, v, tot = (cut(k) for k in ("at", "rt", "rt_b", "bt", "kt", "be", "ke", "v", "tot"))
    mm = lambda a, b: jnp.dot(a, b, preferred_element_type=F32)

    r64 = lax.broadcasted_iota(jnp.int32, (c, gw), 0)
    c64 = lax.broadcasted_iota(jnp.int32, (c, gw), 1) & (HEAD - 1)
    eye_f = (r64 == c64).astype(F32)
    causal = {0: (c64 < r64, c64 <= r64), 1: (c64 > r64, c64 >= r64)}
    strict = [causal[d][0] for d, _, _ in chains]
    incl = [causal[d][1] for d, _, _ in chains]
    same_head = (lax.broadcasted_iota(jnp.int32, (gw, gw), 0) // HEAD
                 == lax.broadcasted_iota(jnp.int32, (gw, gw), 1) // HEAD)

    def bd(xp):
        return jnp.where(same_head, jnp.concatenate([xp] * (gw // c), axis=0), jnp.zeros((), xp.dtype))

    def diag(o):
        head = lax.broadcasted_iota(jnp.int32, (c, gw), 1) // HEAD
        out = o[:c]
        for h in range(1, gw // c):
            out = jnp.where(head == h, o[h * c:(h + 1) * c], out)
        return out

    rows = lambda a, b: jnp.concatenate([a, b], axis=0)
    cols = lambda a, b: jnp.concatenate([a, b], axis=1)
    g = [_dot_nt(rows(at[i], rt_b[i]), rows(bd(bt[i]), bd(kt[i]))) for i in n]
    x = [jnp.where(strict[i], g[i][:c, :gw], 0.0) for i in n]
    a_ak = [jnp.where(strict[i], g[i][:c, gw:], 0.0).astype(BF16) for i in n]
    m_rb = [jnp.where(incl[i], g[i][c:, :gw], 0.0).astype(BF16) for i in n]
    m_rk = [jnp.where(incl[i], g[i][c:, gw:], 0.0).astype(BF16) for i in n]
    t = [eye_f + x[i] for i in n]
    p = [x[i].astype(BF16) for i in n]
    p = [mm(p[i], bd(p[i])).astype(BF16) for i in n]
    for _ in range((c - 1).bit_length() - 2):
        o = [mm(rows(p[i], t[i].astype(BF16)), bd(p[i])) for i in n]
        p = [o[i][:c].astype(BF16) for i in n]
        t = [t[i] + o[i][c:] for i in n]
    t = [t[i] + mm(t[i].astype(BF16), bd(p[i])) for i in n]
    ov = [mm(rows(a_ak[i], m_rk[i]), bd(v[i])) for i in n]
    za = [mm(t[i].astype(BF16), cols(bd(at[i]), bd(ov[i][:c].astype(BF16)))).astype(BF16) for i in n]
    w2 = [mm(m_rb[i], cols(bd(za[i][:, :gw]), bd(za[i][:, gw:]))) for i in n]
    yp = [w2[i][:, gw:] + ov[i][c:] for i in n]
    tn = [_dot_tn(rows(be[i], ke[i]), rows(za[i], cols(jnp.zeros((c, gw), BF16), v[i]))) for i in n]
    gd = [diag(tn[i][:, :gw]) for i in n]
    hv = [diag(tn[i][:, gw:]) for i in n]
    head_sum = same_head.astype(BF16)
    tdiag = [jnp.where(r64 == c64, tot[i], 0.0) for i in n]
    thi = [tdiag[i].astype(BF16) for i in n]
    tlo = [(tdiag[i] - thi[i].astype(F32)).astype(BF16) for i in n]
    pc = [jnp.exp(mm(thi[i], head_sum) + mm(tlo[i], head_sum)) for i in n]
    s = [s_scr[d, bi, gi] for d, bi, gi in chains]
    os_ = [mm(rows((rt[i] + w2[i][:, :gw]).astype(BF16), gd[i].astype(BF16)), bd(s[i].astype(BF16))) for i in n]
    y = [os_[i][:c] + yp[i] for i in n]
    for i, (d, bi, gi) in enumerate(chains):
        s_scr[d, bi, gi] = pc[i] * s[i] + os_[i][c:] + hv[i]
    for d, y_ref in enumerate((yf_ref, yb_ref)):
        for bi in range(nb):
            y_ref[bi] = jnp.concatenate([y[chains.index((d, bi, gi))] for gi in range(ng)],
                                        axis=1).astype(y_ref.dtype)

    @pl.when(ci == pl.num_programs(1) - 1)
    def _():
        sout_ref[...] = s_scr[...]


def _wkv(lw, kd, bb, aa, r, v, s0):
    _, b, l, width = lw.shape
    c = WKV_CHUNK
    nc = l // c
    nb = WKV_ROWS
    dirs = lambda d: pl.BlockSpec((1, nb, c, width), lambda i, j: (d, i, (nc - 1 - j) if d else j, 0))
    both = lambda d: pl.BlockSpec((nb, c, width), lambda i, j: (i, (nc - 1 - j) if d else j, 0))
    stspec = pl.BlockSpec((2, nb) + s0.shape[2:], lambda i, j: (0, i, 0, 0, 0))
    return pl.pallas_call(
        _wkv_kernel,
        grid=(b // nb, nc),
        in_specs=[dirs(0), dirs(0), dirs(0), both(0), both(0), both(0),
                  dirs(1), dirs(1), dirs(1), both(1), both(1), both(1), stspec],
        out_specs=[both(0), both(1), stspec],
        out_shape=[jax.ShapeDtypeStruct((b, l, width), BF16), jax.ShapeDtypeStruct((b, l, width), BF16),
                   jax.ShapeDtypeStruct(s0.shape, F32)],
        scratch_shapes=[pltpu.VMEM((2, nb) + s0.shape[2:], F32)],
        compiler_params=_params("arbitrary", "arbitrary"),
        name="wkv",
    )(lw, kd, bb, aa, r, v, lw, kd, bb, aa, r, v, s0)


def _s5_kernel(*refs, nb, reverse, toeplitz):
    if toeplitz:
        (u_ref, lag_ref, win_ref, wout_ref, a_ref, h0_ref, y_ref, hout_ref,
         x_scr, hh_scr, hre_scr, him_scr, win_scr, wout_scr, wt_scr) = refs
    else:
        (u_ref, yprev_ref, win_ref, wout_ref, a_ref, h0_ref, y_ref, hout_ref,
         x_scr, hh_scr, hre_scr, him_scr, win_scr, wout_scr) = refs
    i = pl.program_id(1)
    ns = hre_scr.shape[1]
    t = win_ref.shape[1]

    @pl.when(i == 0)
    def _():
        hre_scr[...] = h0_ref[0, 0]
        him_scr[...] = h0_ref[0, 1]
        lane_group = (lax.broadcasted_iota(jnp.int32, (S5_GROUP, 2 * ns), 1) % ns) // S5_STATE
        for s in range(t):
            for g in range(LANES // S5_GROUP):
                r0 = s * LANES + g * S5_GROUP
                own = lane_group == g
                win_scr[r0:r0 + S5_GROUP, :] = jnp.where(own, win_ref[0, s], jnp.zeros((), BF16))
                wout_scr[r0:r0 + S5_GROUP, :] = jnp.where(own, wout_ref[0, s], jnp.zeros((), BF16))
        if toeplitz:
            for sig in range(t):
                for tau in range(t):
                    wt_scr[sig * LANES:(sig + 1) * LANES, tau * LANES:(tau + 1) * LANES] = lag_ref[0, tau - sig + t - 1]

    u = u_ref[0].astype(BF16)
    x_scr[...] = jnp.dot(u, win_scr[...], preferred_element_type=F32)
    ar, ai = a_ref[0, 0:1, :], a_ref[0, 1:2, :]
    nch = x_scr.shape[0] // nb

    hr, hi = hre_scr[...], him_scr[...]
    for k in range(nch):
        c = nch - 1 - k if reverse else k
        rows = slice(c * nb, (c + 1) * nb)
        hh_scr[rows, 0:ns] = hr
        hh_scr[rows, ns:2 * ns] = hi
        hr, hi = (hr * ar - hi * ai + x_scr[rows, 0:ns], hr * ai + hi * ar + x_scr[rows, ns:2 * ns])
    hre_scr[...] = hr
    him_scr[...] = hi
    y = _dot_nt(hh_scr[...].astype(BF16), wout_scr[...])
    if toeplitz:
        y = y + jnp.dot(u, wt_scr[...], preferred_element_type=F32)
    else:
        y = y + yprev_ref[0].astype(F32)
    y_ref[0] = y.astype(y_ref.dtype)

    @pl.when(i == pl.num_programs(1) - 1)
    def _():
        hout_ref[0, 0] = hr
        hout_ref[0, 1] = hi


def _s5_scan(u, first, win, wout, acoef, h0, nb, reverse):
    nblk, rows, kw = u.shape
    ns = win.shape[3] // 2
    rt = min(rows, S5_ROWS)
    nt = rows // rt
    tile = pl.BlockSpec((1, rt, kw), lambda g, i: (g, (nt - 1 - i) if reverse else i, 0))
    per_blk = lambda a: pl.BlockSpec((1,) + a.shape[1:], lambda g, i: (g,) + (0,) * (a.ndim - 1))
    weights = [win, wout, acoef, h0]
    toeplitz = not reverse
    return pl.pallas_call(
        functools.partial(_s5_kernel, nb=nb, reverse=reverse, toeplitz=toeplitz),
        grid=(nblk, nt),
        in_specs=[tile, per_blk(first) if toeplitz else tile] + [per_blk(a) for a in weights],
        out_specs=[tile, per_blk(h0)],
        out_shape=[jax.ShapeDtypeStruct((nblk, rows, kw), BF16), jax.ShapeDtypeStruct(h0.shape, F32)],
        scratch_shapes=[pltpu.VMEM((rt, 2 * ns), F32), pltpu.VMEM((rt, 2 * ns), F32),
                        pltpu.VMEM((nb, ns), F32), pltpu.VMEM((nb, ns), F32),
                        pltpu.VMEM((kw, 2 * ns), BF16), pltpu.VMEM((kw, 2 * ns), BF16)]
        + ([pltpu.VMEM((kw, kw), BF16)] if toeplitz else []),
        compiler_params=_params("arbitrary", "arbitrary"),
        name="s5_bwd" if reverse else "s5_fwd",
    )(u, first, *weights)


def _s5_matrices(lam_re, lam_im, log_dt, b_re, b_im, c_re, c_im):
    hp = lax.Precision.HIGHEST
    t = S5_CHUNK
    dt = jnp.exp(log_dt.astype(F32))[..., None]
    lr, li = lam_re.astype(F32), lam_im.astype(F32)
    j = jnp.arange(t + 1, dtype=F32)[:, None, None, None]
    mag = jnp.exp(j * dt * lr)
    pr, pi = mag * jnp.cos(j * dt * li), mag * jnp.sin(j * dt * li)
    ar, ai = pr[1], pi[1]
    den = lr * lr + li * li
    fr = ((ar - 1.0) * lr + ai * li) / den
    fi = (ai * lr - (ar - 1.0) * li) / den
    bbr = fr[..., None] * b_re - fi[..., None] * b_im
    bbi = fr[..., None] * b_im + fi[..., None] * b_re
    car = c_re[None, None] * pr[:, :, :, None, :] - c_im[None, None] * pi[:, :, :, None, :]
    cai = c_re[None, None] * pi[:, :, :, None, :] + c_im[None, None] * pr[:, :, :, None, :]
    g = lr.shape[1]
    ca = jnp.concatenate([car, -cai], axis=-1).transpose(1, 2, 0, 3, 4).reshape(2, g, (t + 1) * S5_GROUP, 2 * S5_STATE)
    kj = jnp.einsum('dgxq,dgqe->dgxe', ca, jnp.concatenate([bbr, bbi], axis=2), precision=hp)
    kj = kj.reshape(2, g, t + 1, S5_GROUP, S5_GROUP).transpose(2, 0, 1, 3, 4)
    gpb = LANES // S5_GROUP
    nblk = g // gpb
    ns = gpb * S5_STATE
    own_chan = (jnp.arange(gpb)[:, None, None] == (jnp.arange(LANES) // S5_GROUP)[None, None, :])

    kl = jnp.concatenate([kj[t - 1:0:-1, 1], (kj[0, 0] + kj[0, 1])[None], kj[1:t, 0]], axis=0)
    kl = kl.reshape(2 * t - 1, nblk, gpb, S5_GROUP, S5_GROUP).transpose(0, 1, 4, 2, 3)
    kl = kl.reshape(2 * t - 1, nblk, 1, S5_GROUP, LANES)
    lagblk = jnp.where(own_chan, kl, 0.0).transpose(1, 0, 2, 3, 4).reshape(nblk, 2 * t - 1, LANES, LANES)

    def state_in(d, pw):
        prs = pr[pw, d].reshape(t, nblk, 1, ns).transpose(1, 0, 2, 3)
        pis = pi[pw, d].reshape(t, nblk, 1, ns).transpose(1, 0, 2, 3)
        lay = lambda q: q.reshape(nblk, gpb, S5_STATE, S5_GROUP).transpose(0, 3, 1, 2).reshape(nblk, 1, S5_GROUP, ns)
        br, bi = lay(bbr[d]), lay(bbi[d])
        return jnp.concatenate([prs * br - pis * bi, prs * bi + pis * br], axis=-1).astype(BF16)

    def state_out(d, pw):
        lay = lambda q: q.reshape(t, nblk, gpb, S5_GROUP, S5_STATE).transpose(1, 0, 3, 2, 4).reshape(
            nblk, t, S5_GROUP, ns)
        return jnp.concatenate([lay(car[pw, d]), lay(-cai[pw, d])], axis=-1).astype(BF16)

    coef = lambda d: jnp.stack([pr[t, d].reshape(nblk, ns), pi[t, d].reshape(nblk, ns)], axis=1)
    fwd = (lagblk.astype(BF16), state_in(0, t - 1 - jnp.arange(t)), state_out(0, jnp.arange(t) + 1), coef(0))
    bwd = (state_in(1, jnp.arange(t)), state_out(1, t - jnp.arange(t)), coef(1))
    return fwd, bwd


def _s5_rows(u5):
    nblk, b, nc, t, w = u5.shape
    return u5.transpose(0, 2, 1, 3, 4).reshape(nblk, nc * b, t * w)


def _s5_unrows(y, b):
    nblk, rows, kw = y.shape
    return y.reshape(nblk, rows // b, b, S5_CHUNK, kw // S5_CHUNK).transpose(0, 2, 1, 3, 4)


def _out_kernel(x_ref, gate_ref, yf_ref, yb_ref, bonus_ref, gr_ref, u_ref, gs_ref, ys_ref, lg_ref, lb_ref, d_ref,
                wg_ref, wo_ref, fg_ref, e_ref, o_ref):
    dr = bonus_ref.shape[2]
    y = yf_ref[0].astype(F32) + yb_ref[0].astype(F32)
    mean = _segsum(y, e_ref) * (1.0 / HEAD)
    dev = y - mean
    var = _segsum(dev * dev, e_ref) * (1.0 / HEAD)
    yn = dev * lax.rsqrt(var + LN_X_EPS) * lg_ref[...] + lb_ref[...]
    gr = gr_ref[0].astype(F32)
    y_r = (yn + bonus_ref[0].astype(F32)) * (gr * _sigmoid(gr))

    tm = u_ref.shape[1]
    ys = jnp.concatenate([ys_ref[j, 0].reshape(tm, LANES) for j in range(ys_ref.shape[0])], axis=1)
    s = ys.astype(F32) + d_ref[...] * u_ref[0].astype(F32)
    s = 0.5 * s * (1.0 + jnp.tanh(math.sqrt(2.0 / math.pi) * (s + 0.044715 * (s * s * s))))
    gl = jnp.dot(s.astype(BF16), wg_ref[...], preferred_element_type=F32)
    ds = gl.shape[1] // 2
    gs = gs_ref[0].astype(F32)
    y_s = gl[:, :ds] * _sigmoid(gl[:, ds:]) * (gs * _sigmoid(gs))

    out = (jnp.dot(y_r.astype(BF16), wo_ref[0:dr, :], preferred_element_type=F32)
           + jnp.dot(y_s.astype(BF16), wo_ref[dr:, :], preferred_element_type=F32))
    xo = x_ref[0] + gate_ref[0] * out
    ms = jnp.mean(xo * xo, axis=-1, keepdims=True)
    o_ref[0] = (xo * lax.rsqrt(ms + EPS) * fg_ref[...]).astype(o_ref.dtype)


def _out_stage(x, gate, yf, yb, bonus, z2, ys, lnx_g, lnx_b, s5_d, w_glu, w_out, final_g, e):
    b, l, d = x.shape
    dr = bonus.shape[2]
    ds = ys.shape[0] * LANES
    assert dr == ds
    tm = 512
    tok = lambda w, cb: pl.BlockSpec((1, tm, w), lambda i, j: (i, j, cb))
    const = lambda *shape: pl.BlockSpec(shape, lambda i, j: (0,) * len(shape))
    return pl.pallas_call(
        _out_kernel,
        grid=(b, l // tm),
        in_specs=[tok(d, 0), pl.BlockSpec((1, 1, d), lambda i, j: (i, 0, 0)),
                  tok(dr, 0), tok(dr, 0), tok(dr, 0), tok(dr, 0), tok(ds, 1), tok(ds, 2),
                  pl.BlockSpec((ys.shape[0], 1, tm // S5_CHUNK, S5_CHUNK, LANES), lambda i, j: (0, i, j, 0, 0)),
                  const(1, dr), const(1, dr), const(1, ds), const(ds, 2 * ds), const(dr + ds, d),
                  const(1, d), const(dr, dr)],
        out_specs=tok(d, 0),
        out_shape=jax.ShapeDtypeStruct((b, l, d), x.dtype),
        compiler_params=_params("arbitrary", "arbitrary"),
        name="out_stage",
    )(x, gate, yf, yb, bonus, z2, z2, z2, ys, lnx_g.reshape(1, dr), lnx_b.reshape(1, dr), s5_d.reshape(1, ds),
      w_glu.astype(BF16), w_out.astype(BF16), final_g.reshape(1, d), e)


def kernel(x, c, ctx, c_ctx, norm_g, w_ada, b_ada, w_in, mu_shift, rwkv_w0, rwkv_w2, rwkv_a0, rwkv_a2, rwkv_k_k, rwkv_k_a, rwkv_r_k, lnx_g, lnx_b, s5_lam_re, s5_lam_im, s5_log_dt, s5_b_re, s5_b_im, s5_c_re, s5_c_im, s5_d, s5_w_glu, w_out, final_g):
    assert norm_g.shape[0] == 1, "one layer"
    b, l, d = x.shape
    lc = ctx.shape[1]
    dr = rwkv_k_k.shape[1]
    ds = s5_d.shape[1]
    sw = mu_shift.shape[1]
    assert sw == 3 * dr + 4 * LORA and l % (4 * GRID_W) == 0 and lc % WKV_CHUNK == 0 and b % 8 == 0

    cond = jnp.zeros((2 * b, d), F32).at[:b].set(c).at[b].set(c_ctx)
    m = _modulation(cond, w_ada[0], b_ada[0])
    shift, scale, gate = m[:b, :d], m[:b, d:2 * d], m[:b, 2 * d:]
    cshift = jnp.broadcast_to(m[b, :d], (b, d))
    cscale = jnp.broadcast_to(m[b, d:2 * d], (b, d))

    w1 = w_in[0][:, :sw].astype(BF16)
    w2 = w_in[0][:, sw:].astype(BF16)
    z1c, _, u5c = _inproj(ctx, norm_g[0], cscale[:, None], cshift[:, None], w1, w2)
    z1, z2, u5 = _inproj(x, norm_g[0], scale[:, None], shift[:, None], w1, w2)

    hid = jnp.arange(dr) // HEAD
    e = (hid[:, None] == hid[None, :]).astype(BF16)
    pp = (mu_shift, rwkv_w0[0], rwkv_w2[0], rwkv_a0[0], rwkv_a2[0], rwkv_k_k, rwkv_k_a,
          rwkv_r_k[0].reshape(1, dr), e)
    rc, vc, aac, wc, kdc, bbc, _ = _prep(z1c, pp, grid2d=False)
    rx, vx, aax, wx, kdx, bbx, bonus = _prep(z1, pp, grid2d=True)

    s0 = jnp.zeros((2, b, dr // (WKV_PACK * HEAD), HEAD, WKV_PACK * HEAD), F32)
    _, _, s_ctx = _wkv(wc, kdc, bbc, aac, rc, vc, s0)
    yf, yb, _ = _wkv(wx, kdx, bbx, aax, rx, vx, s_ctx)

    s5f, s5b = _s5_matrices(s5_lam_re[0], s5_lam_im[0], s5_log_dt[0], s5_b_re[0], s5_b_im[0],
                            s5_c_re[0], s5_c_im[0])
    uc, ux = _s5_rows(u5c), _s5_rows(u5)
    h0 = jnp.zeros((uc.shape[0], 2, b, s5f[3].shape[2]), F32)
    ysc, hcf = _s5_scan(uc, *s5f, h0, b, False)
    _, hcb = _s5_scan(uc, ysc, *s5b, h0, b, True)
    ysf, _ = _s5_scan(ux, *s5f, hcf, b, False)
    ysfb, _ = _s5_scan(ux, ysf, *s5b, hcb, b, True)
    ys = _s5_unrows(ysfb, b)

    return _out_stage(x, gate[:, None], yf, yb, bonus, z2, ys, lnx_g[0], lnx_b[0], s5_d[0], s5_w_glu[0],
                      w_out[0], final_g, e)
```

```python
import functools
import math

import jax
import jax.numpy as jnp
from jax import lax
from jax.experimental import pallas as pl
from jax.experimental.pallas import tpu as pltpu

F32 = jnp.float32
BF16 = jnp.bfloat16

GRID_W = 64
HEAD = 64
LORA = 64
S5_GROUP = 16
S5_STATE = 64
S5_CHUNK = 16
S5_ROWS = 256
WKV_CHUNK = 64
WKV_ROWS = 4
WKV_PACK = 2
EPS = 1e-6
LN_X_EPS = 64e-5
EXP_M05 = math.exp(-0.5)

LANES = 128
VMEM_LIMIT = 56 * 1024 * 1024


def _params(*sem):
    return pltpu.CompilerParams(dimension_semantics=sem, vmem_limit_bytes=VMEM_LIMIT)


def _sigmoid(x):
    return 1.0 / (1.0 + jnp.exp(-x))


def _dot_bf16(a, b):
    return jnp.dot(a.astype(BF16), b.astype(BF16), preferred_element_type=F32)


def _dot_f32(a, b):
    return jnp.dot(a, b, preferred_element_type=F32, precision=lax.Precision.HIGHEST)


def _segsum(x, e_ref):
    return jnp.dot(x.astype(BF16), e_ref[...], preferred_element_type=F32)


def _mod_kernel(c_ref, w_ref, b_ref, o_ref):
    c = c_ref[...]
    o_ref[...] = _dot_f32(c * _sigmoid(c), w_ref[...]) + b_ref[...]


def _modulation(cond, w_ada, b_ada):
    rows, d = cond.shape
    n = w_ada.shape[1]
    tn = 512
    return pl.pallas_call(
        _mod_kernel,
        grid=(n // tn,),
        in_specs=[pl.BlockSpec((rows, d), lambda j: (0, 0)),
                  pl.BlockSpec((d, tn), lambda j: (0, j)),
                  pl.BlockSpec((1, tn), lambda j: (0, j))],
        out_specs=pl.BlockSpec((rows, tn), lambda j: (0, j)),
        out_shape=jax.ShapeDtypeStruct((rows, n), F32),
        compiler_params=_params("arbitrary"),
        name="modulation",
    )(cond, w_ada, b_ada.reshape(1, n))


def _inproj_kernel(x_ref, g_ref, sc_ref, sh_ref, w1_ref, w2_ref, z1_ref, z2_ref, u_ref):
    x = x_ref[0]
    ms = jnp.mean(x * x, axis=-1, keepdims=True)
    h = x * lax.rsqrt(ms + EPS) * g_ref[...]
    h = (h * (1.0 + sc_ref[0]) + sh_ref[0]).astype(BF16)
    z1_ref[0] = jnp.dot(h, w1_ref[...], preferred_element_type=F32).astype(z1_ref.dtype)
    z2 = jnp.dot(h, w2_ref[...], preferred_element_type=F32)
    z2_ref[0] = z2.astype(z2_ref.dtype)
    nblk, nch = u_ref.shape[0], u_ref.shape[1]
    w = z2.shape[1] // 3
    for j in range(nblk):
        u_ref[j, :, 0] = z2[:, w + LANES * j:w + LANES * (j + 1)].reshape(nch, S5_CHUNK, LANES).astype(u_ref.dtype)


def _inproj(x, norm_g, scale, shift, w1, w2):
    b, l, d = x.shape
    n1, n2 = w1.shape[1], w2.shape[1]
    tm = min(l, 512)
    nblk = n2 // 3 // LANES
    return pl.pallas_call(
        _inproj_kernel,
        grid=(b, l // tm),
        in_specs=[pl.BlockSpec((1, tm, d), lambda i, j: (i, j, 0)),
                  pl.BlockSpec((1, d), lambda i, j: (0, 0)),
                  pl.BlockSpec((1, 1, d), lambda i, j: (i, 0, 0)),
                  pl.BlockSpec((1, 1, d), lambda i, j: (i, 0, 0)),
                  pl.BlockSpec((d, n1), lambda i, j: (0, 0)),
                  pl.BlockSpec((d, n2), lambda i, j: (0, 0))],
        out_specs=[pl.BlockSpec((1, tm, n1), lambda i, j: (i, j, 0)),
                   pl.BlockSpec((1, tm, n2), lambda i, j: (i, j, 0)),
                   pl.BlockSpec((nblk, tm // S5_CHUNK, 1, S5_CHUNK, LANES), lambda i, j: (0, j, i, 0, 0))],
        out_shape=[jax.ShapeDtypeStruct((b, l, n1), BF16),
                   jax.ShapeDtypeStruct((b, l, n2), BF16),
                   jax.ShapeDtypeStruct((nblk, l // S5_CHUNK, b, S5_CHUNK, LANES), BF16)],
        compiler_params=_params("arbitrary", "arbitrary"),
        name="inproj",
    )(x, norm_g.reshape(1, d), scale, shift, w1, w2)


def _prep_body(zb, up, down, mu_ref, w0_ref, w2_ref, a0_ref, a2_ref, kk_ref, ka_ref, rk_ref, e_ref, sh_ref, outs,
               grid2d):
    t, sw = zb.shape
    dr = (sw - 4 * LORA) // 3
    z = zb.astype(F32)
    slot = lax.broadcasted_iota(jnp.int32, (1, sw), 1) & 3
    pn = jnp.dot(sh_ref[...], zb, preferred_element_type=F32)
    prev, nxt = pn[:t], pn[t:]
    if grid2d:
        sh = jnp.where(slot == 0, prev, jnp.where(slot == 1, nxt, jnp.where(slot == 2, up, down)))
    else:
        sh = jnp.where((slot & 1) == 0, prev, nxt)
    zs = z + mu_ref[...] * (sh - z)

    r = zs[:, 0:dr]
    k = zs[:, dr:2 * dr]
    v = zs[:, 2 * dr:3 * dr]
    kk = k * kk_ref[...]
    ss = _segsum(kk * kk, e_ref)
    kk = kk / jnp.maximum(jnp.sqrt(ss), 1e-12)
    r_o, v_o, aa_o, w_o, kd_o, bb_o, bonus_o = outs
    r_o[0] = r.astype(r_o.dtype)
    v_o[0] = v.astype(v_o.dtype)
    aa_o[0] = (-kk).astype(aa_o.dtype)
    ksum = None
    for d in range(2):
        zw = zs[:, 3 * dr + LORA * d:3 * dr + LORA * (d + 1)]
        za = zs[:, 3 * dr + 2 * LORA + LORA * d:3 * dr + 2 * LORA + LORA * (d + 1)]
        wl = w0_ref[d:d + 1, :] + _dot_bf16(jnp.tanh(zw), w2_ref[d])
        w_o[d, 0] = -EXP_M05 * _sigmoid(wl)
        asig = _sigmoid(a0_ref[d:d + 1, :] + _dot_bf16(za, a2_ref[d]))
        kd = k * (1.0 + (asig - 1.0) * ka_ref[...])
        kd_o[d, 0] = kd.astype(kd_o.dtype)
        bb_o[d, 0] = (kk * asig).astype(bb_o.dtype)
        ksum = kd if ksum is None else ksum + kd
    bonus_o[0] = (_segsum(r * (0.5 * ksum) * rk_ref[...], e_ref) * v).astype(bonus_o.dtype)


def _prep2d_kernel(zc_ref, zu_ref, zd_ref, *rest):
    params, outs = rest[:10], rest[10:]
    j = pl.program_id(1)
    nj = pl.num_programs(1)
    zb = zc_ref[0]
    z = zb.astype(F32)
    t = z.shape[0]
    row = lax.broadcasted_iota(jnp.int32, (t, 1), 0)
    up = jnp.concatenate([zu_ref[0].astype(F32), z[:t - GRID_W]], axis=0)
    up = jnp.where(jnp.logical_and(j == 0, row < GRID_W), 0.0, up)
    down = jnp.concatenate([z[GRID_W:], zd_ref[0].astype(F32)], axis=0)
    down = jnp.where(jnp.logical_and(j == nj - 1, row >= t - GRID_W), 0.0, down)
    _prep_body(zb, up, down, *params, outs, grid2d=True)


def _prep1d_kernel(zc_ref, *rest):
    params, outs = rest[:10], rest[10:]
    _prep_body(zc_ref[0], None, None, *params, outs, grid2d=False)


def _prep(z1, pp, grid2d):
    b, l, sw = z1.shape
    dr = (sw - 4 * LORA) // 3
    tt = 256 if grid2d else l
    nj = l // tt
    rb = tt // GRID_W
    const = lambda *shape: pl.BlockSpec(shape, lambda i, j: (0,) * len(shape))
    p_specs = [const(1, sw), const(2, dr), const(2, LORA, dr), const(2, dr), const(2, LORA, dr),
               const(1, dr), const(1, dr), const(1, dr), const(dr, dr), const(2 * tt, tt)]
    edge = GRID_W if grid2d else tt
    tok = jnp.arange(tt)
    nbr = jnp.concatenate([jnp.where(tok % edge != 0, tok - 1, -1), jnp.where(tok % edge != edge - 1, tok + 1, -1)])
    shift = (nbr[:, None] == tok[None, :]).astype(BF16)
    cur = pl.BlockSpec((1, tt, sw), lambda i, j: (i, j, 0))
    if grid2d:
        nrow = l // GRID_W
        in_specs = [cur,
                    pl.BlockSpec((1, GRID_W, sw), lambda i, j: (i, jnp.maximum(j * rb - 1, 0), 0)),
                    pl.BlockSpec((1, GRID_W, sw), lambda i, j: (i, jnp.minimum((j + 1) * rb, nrow - 1), 0))]
        args = (z1, z1, z1)
        body = _prep2d_kernel
    else:
        in_specs = [cur]
        args = (z1,)
        body = _prep1d_kernel
    o1 = pl.BlockSpec((1, tt, dr), lambda i, j: (i, j, 0))
    o2 = pl.BlockSpec((2, 1, tt, dr), lambda i, j: (0, i, j, 0))
    s1 = jax.ShapeDtypeStruct((b, l, dr), BF16)
    s2 = jax.ShapeDtypeStruct((2, b, l, dr), BF16)
    s2w = jax.ShapeDtypeStruct((2, b, l, dr), F32)
    return pl.pallas_call(
        body,
        grid=(b, nj),
        in_specs=in_specs + p_specs,
        out_specs=[o1, o1, o1, o2, o2, o2, o1],
        out_shape=[s1, s1, s1, s2w, s2, s2, s1],
        compiler_params=_params("arbitrary", "arbitrary"),
        name="prep2d" if grid2d else "prep1d",
    )(*args, *pp, shift)


def _dot_nt(a, b):
    return lax.dot_general(a, b, (((1,), (1,)), ((), ())), preferred_element_type=F32)


def _dot_tn(a, b):
    return lax.dot_general(a, b, (((0,), (0,)), ((), ())), preferred_element_type=F32)


def _wkv_operands(lw_ref, kd_ref, bb_ref, aa_ref, r_ref, v_ref, bi, backward):
    c = aa_ref.shape[1]
    row = lax.broadcasted_iota(jnp.int32, (c, c), 0)
    col = lax.broadcasted_iota(jnp.int32, (c, c), 1)
    lw = lw_ref[0, bi]
    lw_hi = lw.astype(BF16)
    lw_lo = (lw - lw_hi.astype(F32)).astype(BF16)
    lc = (col >= row if backward else col <= row).astype(BF16)
    cum = jnp.dot(lc, lw_hi, preferred_element_type=F32) + jnp.dot(lc, lw_lo, preferred_element_type=F32)
    tot = cum[0:1, :] if backward else cum[c - 1:c, :]
    pinv = jnp.exp(-cum)
    pend = jnp.exp(tot - cum)
    bb, kd = bb_ref[0, bi].astype(F32), kd_ref[0, bi].astype(F32)
    rt = r_ref[bi].astype(F32) * jnp.exp(cum)
    return dict(
        tot=tot, at=(aa_ref[bi].astype(F32) * jnp.exp(cum - lw)).astype(BF16),
        rt=rt, rt_b=rt.astype(BF16),
        bt=(bb * pinv).astype(BF16), kt=(kd * pinv).astype(BF16),
        be=(bb * pend).astype(BF16), ke=(kd * pend).astype(BF16), v=v_ref[bi].astype(BF16))


def _wkv_kernel(lwf_ref, kdf_ref, bbf_ref, aaf_ref, rf_ref, vf_ref, lwb_ref, kdb_ref, bbb_ref, aab_ref, rb_ref,
                vb_ref, s0_ref, yf_ref, yb_ref, sout_ref, s_scr):
    ci = pl.program_id(1)
    nb, c, width = aaf_ref.shape
    gw = WKV_PACK * HEAD
    ng = width // gw
    assert c == HEAD and width % gw == 0

    @pl.when(ci == 0)
    def _():
        s_scr[...] = s0_ref[...]

    ops = {(0, bi): _wkv_operands(lwf_ref, kdf_ref, bbf_ref, aaf_ref, rf_ref, vf_ref, bi, False) for bi in range(nb)}
    ops.update({(1, bi): _wkv_operands(lwb_ref, kdb_ref, bbb_ref, aab_ref, rb_ref, vb_ref, bi, True)
                for bi in range(nb)})
    chains = [(d, bi, gi) for gi in range(ng) for bi in range(nb) for d in range(2)]
    n = range(len(chains))
    cut = lambda name: [ops[d, bi][name][:, gi * gw:(gi + 1) * gw] for d, bi, gi in chains]
    at, rt, rt_b, bt, kt, be, ke, v, tot = (cut(k) for k in ("at", "rt", "rt_b", "bt", "kt", "be", "ke", "v", "tot"))
    mm = lambda a, b: jnp.dot(a, b, preferred_element_type=F32)

    r64 = lax.broadcasted_iota(jnp.int32, (c, gw), 0)
    c64 = lax.broadcasted_iota(jnp.int32, (c, gw), 1) & (HEAD - 1)
    eye_f = (r64 == c64).astype(F32)
    causal = {0: (c64 < r64, c64 <= r64), 1: (c64 > r64, c64 >= r64)}
    strict = [causal[d][0] for d, _, _ in chains]
    incl = [causal[d][1] for d, _, _ in chains]
    same_head = (lax.broadcasted_iota(jnp.int32, (gw, gw), 0) // HEAD
                 == lax.broadcasted_iota(jnp.int32, (gw, gw), 1) // HEAD)

    def bd(xp):
        return jnp.where(same_head, jnp.concatenate([xp] * (gw // c), axis=0), jnp.zeros((), xp.dtype))

    def diag(o):
        head = lax.broadcasted_iota(jnp.int32, (c, gw), 1) // HEAD
        out = o[:c]
        for h in range(1, gw // c):
            out = jnp.where(head == h, o[h * c:(h + 1) * c], out)
        return out

    rows = lambda a, b: jnp.concatenate([a, b], axis=0)
    cols = lambda a, b: jnp.concatenate([a, b], axis=1)
    g = [_dot_nt(rows(at[i], rt_b[i]), rows(bd(bt[i]), bd(kt[i]))) for i in n]
    x = [jnp.where(strict[i], g[i][:c, :gw], 0.0) for i in n]
    a_ak = [jnp.where(strict[i], g[i][:c, gw:], 0.0).astype(BF16) for i in n]
    m_rb = [jnp.where(incl[i], g[i][c:, :gw], 0.0).astype(BF16) for i in n]
    m_rk = [jnp.where(incl[i], g[i][c:, gw:], 0.0).astype(BF16) for i in n]
    t = [eye_f + x[i] for i in n]
    p = [x[i].astype(BF16) for i in n]
    p = [mm(p[i], bd(p[i])).astype(BF16) for i in n]
    for _ in range((c - 1).bit_length() - 2):
        o = [mm(rows(p[i], t[i].astype(BF16)), bd(p[i])) for i in n]
        p = [o[i][:c].astype(BF16) for i in n]
        t = [t[i] + o[i][c:] for i in n]
    t = [t[i] + mm(t[i].astype(BF16), bd(p[i])) for i in n]
    ov = [mm(rows(a_ak[i], m_rk[i]), bd(v[i])) for i in n]
    za = [mm(t[i].astype(BF16), cols(bd(at[i]), bd(ov[i][:c].astype(BF16)))).astype(BF16) for i in n]
    w2 = [mm(m_rb[i], cols(bd(za[i][:, :gw]), bd(za[i][:, gw:]))) for i in n]
    yp = [w2[i][:, gw:] + ov[i][c:] for i in n]
    tn = [_dot_tn(rows(be[i], ke[i]), rows(za[i], cols(jnp.zeros((c, gw), BF16), v[i]))) for i in n]
    gd = [diag(tn[i][:, :gw]) for i in n]
    hv = [diag(tn[i][:, gw:]) for i in n]
    head_sum = same_head.astype(BF16)
    tdiag = [jnp.where(r64 == c64, tot[i], 0.0) for i in n]
    thi = [tdiag[i].astype(BF16) for i in n]
    tlo = [(tdiag[i] - thi[i].astype(F32)).astype(BF16) for i in n]
    pc = [jnp.exp(mm(thi[i], head_sum) + mm(tlo[i], head_sum)) for i in n]
    s = [s_scr[d, bi, gi] for d, bi, gi in chains]
    os_ = [mm(rows((rt[i] + w2[i][:, :gw]).astype(BF16), gd[i].astype(BF16)), bd(s[i].astype(BF16))) for i in n]
    y = [os_[i][:c] + yp[i] for i in n]
    for i, (d, bi, gi) in enumerate(chains):
        s_scr[d, bi, gi] = pc[i] * s[i] + os_[i][c:] + hv[i]
    for d, y_ref in enumerate((yf_ref, yb_ref)):
        for bi in range(nb):
            y_ref[bi] = jnp.concatenate([y[chains.index((d, bi, gi))] for gi in range(ng)],
                                        axis=1).astype(y_ref.dtype)

    @pl.when(ci == pl.num_programs(1) - 1)
    def _():
        sout_ref[...] = s_scr[...]


def _wkv(lw, kd, bb, aa, r, v, s0):
    _, b, l, width = lw.shape
    c = WKV_CHUNK
    nc = l // c
    nb = WKV_ROWS
    dirs = lambda d: pl.BlockSpec((1, nb, c, width), lambda i, j: (d, i, (nc - 1 - j) if d else j, 0))
    both = lambda d: pl.BlockSpec((nb, c, width), lambda i, j: (i, (nc - 1 - j) if d else j, 0))
    stspec = pl.BlockSpec((2, nb) + s0.shape[2:], lambda i, j: (0, i, 0, 0, 0))
    return pl.pallas_call(
        _wkv_kernel,
        grid=(b // nb, nc),
        in_specs=[dirs(0), dirs(0), dirs(0), both(0), both(0), both(0),
                  dirs(1), dirs(1), dirs(1), both(1), both(1), both(1), stspec],
        out_specs=[both(0), both(1), stspec],
        out_shape=[jax.ShapeDtypeStruct((b, l, width), BF16), jax.ShapeDtypeStruct((b, l, width), BF16),
                   jax.ShapeDtypeStruct(s0.shape, F32)],
        scratch_shapes=[pltpu.VMEM((2, nb) + s0.shape[2:], F32)],
        compiler_params=_params("arbitrary", "arbitrary"),
        name="wkv",
    )(lw, kd, bb, aa, r, v, lw, kd, bb, aa, r, v, s0)


def _s5_kernel(*refs, nb, reverse, toeplitz):
    if toeplitz:
        (u_ref, lag_ref, win_ref, wout_ref, a_ref, h0_ref, y_ref, hout_ref,
         x_scr, hh_scr, hre_scr, him_scr, win_scr, wout_scr, wt_scr) = refs
    else:
        (u_ref, yprev_ref, win_ref, wout_ref, a_ref, h0_ref, y_ref, hout_ref,
         x_scr, hh_scr, hre_scr, him_scr, win_scr, wout_scr) = refs
    i = pl.program_id(1)
    ns = hre_scr.shape[1]
    t = win_ref.shape[1]

    @pl.when(i == 0)
    def _():
        hre_scr[...] = h0_ref[0, 0]
        him_scr[...] = h0_ref[0, 1]
        lane_group = (lax.broadcasted_iota(jnp.int32, (S5_GROUP, 2 * ns), 1) % ns) // S5_STATE
        for s in range(t):
            for g in range(LANES // S5_GROUP):
                r0 = s * LANES + g * S5_GROUP
                own = lane_group == g
                win_scr[r0:r0 + S5_GROUP, :] = jnp.where(own, win_ref[0, s], jnp.zeros((), BF16))
                wout_scr[r0:r0 + S5_GROUP, :] = jnp.where(own, wout_ref[0, s], jnp.zeros((), BF16))
        if toeplitz:
            for sig in range(t):
                for tau in range(t):
                    wt_scr[sig * LANES:(sig + 1) * LANES, tau * LANES:(tau + 1) * LANES] = lag_ref[0, tau - sig + t - 1]

    u = u_ref[0].astype(BF16)
    x_scr[...] = jnp.dot(u, win_scr[...], preferred_element_type=F32)
    ar, ai = a_ref[0, 0:1, :], a_ref[0, 1:2, :]
    nch = x_scr.shape[0] // nb

    hr, hi = hre_scr[...], him_scr[...]
    for k in range(nch):
        c = nch - 1 - k if reverse else k
        rows = slice(c * nb, (c + 1) * nb)
        hh_scr[rows, 0:ns] = hr
        hh_scr[rows, ns:2 * ns] = hi
        hr, hi = (hr * ar - hi * ai + x_scr[rows, 0:ns], hr * ai + hi * ar + x_scr[rows, ns:2 * ns])
    hre_scr[...] = hr
    him_scr[...] = hi
    y = _dot_nt(hh_scr[...].astype(BF16), wout_scr[...])
    if toeplitz:
        y = y + jnp.dot(u, wt_scr[...], preferred_element_type=F32)
    else:
        y = y + yprev_ref[0].astype(F32)
    y_ref[0] = y.astype(y_ref.dtype)

    @pl.when(i == pl.num_programs(1) - 1)
    def _():
        hout_ref[0, 0] = hr
        hout_ref[0, 1] = hi


def _s5_scan(u, first, win, wout, acoef, h0, nb, reverse):
    nblk, rows, kw = u.shape
    ns = win.shape[3] // 2
    rt = min(rows, S5_ROWS)
    nt = rows // rt
    tile = pl.BlockSpec((1, rt, kw), lambda g, i: (g, (nt - 1 - i) if reverse else i, 0))
    per_blk = lambda a: pl.BlockSpec((1,) + a.shape[1:], lambda g, i: (g,) + (0,) * (a.ndim - 1))
    weights = [win, wout, acoef, h0]
    toeplitz = not reverse
    return pl.pallas_call(
        functools.partial(_s5_kernel, nb=nb, reverse=reverse, toeplitz=toeplitz),
        grid=(nblk, nt),
        in_specs=[tile, per_blk(first) if toeplitz else tile] + [per_blk(a) for a in weights],
        out_specs=[tile, per_blk(h0)],
        out_shape=[jax.ShapeDtypeStruct((nblk, rows, kw), BF16), jax.ShapeDtypeStruct(h0.shape, F32)],
        scratch_shapes=[pltpu.VMEM((rt, 2 * ns), F32), pltpu.VMEM((rt, 2 * ns), F32),
                        pltpu.VMEM((nb, ns), F32), pltpu.VMEM((nb, ns), F32),
                        pltpu.VMEM((kw, 2 * ns), BF16), pltpu.VMEM((kw, 2 * ns), BF16)]
        + ([pltpu.VMEM((kw, kw), BF16)] if toeplitz else []),
        compiler_params=_params("arbitrary", "arbitrary"),
        name="s5_bwd" if reverse else "s5_fwd",
    )(u, first, *weights)


def _s5_matrices(lam_re, lam_im, log_dt, b_re, b_im, c_re, c_im):
    hp = lax.Precision.HIGHEST
    t = S5_CHUNK
    dt = jnp.exp(log_dt.astype(F32))[..., None]
    lr, li = lam_re.astype(F32), lam_im.astype(F32)
    j = jnp.arange(t + 1, dtype=F32)[:, None, None, None]
    mag = jnp.exp(j * dt * lr)
    pr, pi = mag * jnp.cos(j * dt * li), mag * jnp.sin(j * dt * li)
    ar, ai = pr[1], pi[1]
    den = lr * lr + li * li
    fr = ((ar - 1.0) * lr + ai * li) / den
    fi = (ai * lr - (ar - 1.0) * li) / den
    bbr = fr[..., None] * b_re - fi[..., None] * b_im
    bbi = fr[..., None] * b_im + fi[..., None] * b_re
    car = c_re[None, None] * pr[:, :, :, None, :] - c_im[None, None] * pi[:, :, :, None, :]
    cai = c_re[None, None] * pi[:, :, :, None, :] + c_im[None, None] * pr[:, :, :, None, :]
    g = lr.shape[1]
    ca = jnp.concatenate([car, -cai], axis=-1).transpose(1, 2, 0, 3, 4).reshape(2, g, (t + 1) * S5_GROUP, 2 * S5_STATE)
    kj = jnp.einsum('dgxq,dgqe->dgxe', ca, jnp.concatenate([bbr, bbi], axis=2), precision=hp)
    kj = kj.reshape(2, g, t + 1, S5_GROUP, S5_GROUP).transpose(2, 0, 1, 3, 4)
    gpb = LANES // S5_GROUP
    nblk = g // gpb
    ns = gpb * S5_STATE
    own_chan = (jnp.arange(gpb)[:, None, None] == (jnp.arange(LANES) // S5_GROUP)[None, None, :])

    kl = jnp.concatenate([kj[t - 1:0:-1, 1], (kj[0, 0] + kj[0, 1])[None], kj[1:t, 0]], axis=0)
    kl = kl.reshape(2 * t - 1, nblk, gpb, S5_GROUP, S5_GROUP).transpose(0, 1, 4, 2, 3)
    kl = kl.reshape(2 * t - 1, nblk, 1, S5_GROUP, LANES)
    lagblk = jnp.where(own_chan, kl, 0.0).transpose(1, 0, 2, 3, 4).reshape(nblk, 2 * t - 1, LANES, LANES)

    def state_in(d, pw):
        prs = pr[pw, d].reshape(t, nblk, 1, ns).transpose(1, 0, 2, 3)
        pis = pi[pw, d].reshape(t, nblk, 1, ns).transpose(1, 0, 2, 3)
        lay = lambda q: q.reshape(nblk, gpb, S5_STATE, S5_GROUP).transpose(0, 3, 1, 2).reshape(nblk, 1, S5_GROUP, ns)
        br, bi = lay(bbr[d]), lay(bbi[d])
        return jnp.concatenate([prs * br - pis * bi, prs * bi + pis * br], axis=-1).astype(BF16)

    def state_out(d, pw):
        lay = lambda q: q.reshape(t, nblk, gpb, S5_GROUP, S5_STATE).transpose(1, 0, 3, 2, 4).reshape(
            nblk, t, S5_GROUP, ns)
        return jnp.concatenate([lay(car[pw, d]), lay(-cai[pw, d])], axis=-1).astype(BF16)

    coef = lambda d: jnp.stack([pr[t, d].reshape(nblk, ns), pi[t, d].reshape(nblk, ns)], axis=1)
    fwd = (lagblk.astype(BF16), state_in(0, t - 1 - jnp.arange(t)), state_out(0, jnp.arange(t) + 1), coef(0))
    bwd = (state_in(1, jnp.arange(t)), state_out(1, t - jnp.arange(t)), coef(1))
    return fwd, bwd


def _s5_rows(u5):
    nblk, nc, b, t, w = u5.shape
    return u5.reshape(nblk, nc * b, t * w)


def _s5_unrows(y, b):
    nblk, rows, kw = y.shape
    return y.reshape(nblk, rows // b, b, S5_CHUNK, kw // S5_CHUNK)


def _out_kernel(x_ref, gate_ref, yf_ref, yb_ref, bonus_ref, gr_ref, u_ref, gs_ref, ys_ref, lg_ref, lb_ref, d_ref,
                wg_ref, wo_ref, fg_ref, e_ref, o_ref):
    dr = bonus_ref.shape[2]
    y = yf_ref[0].astype(F32) + yb_ref[0].astype(F32)
    mean = _segsum(y, e_ref) * (1.0 / HEAD)
    dev = y - mean
    var = _segsum(dev * dev, e_ref) * (1.0 / HEAD)
    yn = dev * lax.rsqrt(var + LN_X_EPS) * lg_ref[...] + lb_ref[...]
    gr = gr_ref[0].astype(F32)
    y_r = (yn + bonus_ref[0].astype(F32)) * (gr * _sigmoid(gr))

    tm = u_ref.shape[1]
    ys = jnp.concatenate([ys_ref[j, :, 0].reshape(tm, LANES) for j in range(ys_ref.shape[0])], axis=1)
    s = ys.astype(F32) + d_ref[...] * u_ref[0].astype(F32)
    s = 0.5 * s * (1.0 + jnp.tanh(math.sqrt(2.0 / math.pi) * (s + 0.044715 * (s * s * s))))
    gl = jnp.dot(s.astype(BF16), wg_ref[...], preferred_element_type=F32)
    ds = gl.shape[1] // 2
    gs = gs_ref[0].astype(F32)
    y_s = gl[:, :ds] * _sigmoid(gl[:, ds:]) * (gs * _sigmoid(gs))

    out = (jnp.dot(y_r.astype(BF16), wo_ref[0:dr, :], preferred_element_type=F32)
           + jnp.dot(y_s.astype(BF16), wo_ref[dr:, :], preferred_element_type=F32))
    xo = x_ref[0] + gate_ref[0] * out
    ms = jnp.mean(xo * xo, axis=-1, keepdims=True)
    o_ref[0] = (xo * lax.rsqrt(ms + EPS) * fg_ref[...]).astype(o_ref.dtype)


def _out_stage(x, gate, yf, yb, bonus, z2, ys, lnx_g, lnx_b, s5_d, w_glu, w_out, final_g, e):
    b, l, d = x.shape
    dr = bonus.shape[2]
    ds = ys.shape[0] * LANES
    assert dr == ds
    tm = 512
    tok = lambda w, cb: pl.BlockSpec((1, tm, w), lambda i, j: (i, j, cb))
    const = lambda *shape: pl.BlockSpec(shape, lambda i, j: (0,) * len(shape))
    return pl.pallas_call(
        _out_kernel,
        grid=(b, l // tm),
        in_specs=[tok(d, 0), pl.BlockSpec((1, 1, d), lambda i, j: (i, 0, 0)),
                  tok(dr, 0), tok(dr, 0), tok(dr, 0), tok(dr, 0), tok(ds, 1), tok(ds, 2),
                  pl.BlockSpec((ys.shape[0], tm // S5_CHUNK, 1, S5_CHUNK, LANES), lambda i, j: (0, j, i, 0, 0)),
                  const(1, dr), const(1, dr), const(1, ds), const(ds, 2 * ds), const(dr + ds, d),
                  const(1, d), const(dr, dr)],
        out_specs=tok(d, 0),
        out_shape=jax.ShapeDtypeStruct((b, l, d), x.dtype),
        compiler_params=_params("arbitrary", "arbitrary"),
        name="out_stage",
    )(x, gate, yf, yb, bonus, z2, z2, z2, ys, lnx_g.reshape(1, dr), lnx_b.reshape(1, dr), s5_d.reshape(1, ds),
      w_glu.astype(BF16), w_out.astype(BF16), final_g.reshape(1, d), e)


def kernel(x, c, ctx, c_ctx, norm_g, w_ada, b_ada, w_in, mu_shift, rwkv_w0, rwkv_w2, rwkv_a0, rwkv_a2, rwkv_k_k, rwkv_k_a, rwkv_r_k, lnx_g, lnx_b, s5_lam_re, s5_lam_im, s5_log_dt, s5_b_re, s5_b_im, s5_c_re, s5_c_im, s5_d, s5_w_glu, w_out, final_g):
    assert norm_g.shape[0] == 1, "one layer"
    b, l, d = x.shape
    lc = ctx.shape[1]
    dr = rwkv_k_k.shape[1]
    ds = s5_d.shape[1]
    sw = mu_shift.shape[1]
    assert sw == 3 * dr + 4 * LORA and l % (4 * GRID_W) == 0 and lc % WKV_CHUNK == 0 and b % 8 == 0

    cond = jnp.zeros((2 * b, d), F32).at[:b].set(c).at[b].set(c_ctx)
    m = _modulation(cond, w_ada[0], b_ada[0])
    shift, scale, gate = m[:b, :d], m[:b, d:2 * d], m[:b, 2 * d:]
    cshift = jnp.broadcast_to(m[b, :d], (b, d))
    cscale = jnp.broadcast_to(m[b, d:2 * d], (b, d))

    w1 = w_in[0][:, :sw].astype(BF16)
    w2 = w_in[0][:, sw:].astype(BF16)
    z1c, _, u5c = _inproj(ctx, norm_g[0], cscale[:, None], cshift[:, None], w1, w2)
    z1, z2, u5 = _inproj(x, norm_g[0], scale[:, None], shift[:, None], w1, w2)

    hid = jnp.arange(dr) // HEAD
    e = (hid[:, None] == hid[None, :]).astype(BF16)
    pp = (mu_shift, rwkv_w0[0], rwkv_w2[0], rwkv_a0[0], rwkv_a2[0], rwkv_k_k, rwkv_k_a,
          rwkv_r_k[0].reshape(1, dr), e)
    rc, vc, aac, wc, kdc, bbc, _ = _prep(z1c, pp, grid2d=False)
    rx, vx, aax, wx, kdx, bbx, bonus = _prep(z1, pp, grid2d=True)

    s0 = jnp.zeros((2, b, dr // (WKV_PACK * HEAD), HEAD, WKV_PACK * HEAD), F32)
    _, _, s_ctx = _wkv(wc, kdc, bbc, aac, rc, vc, s0)
    yf, yb, _ = _wkv(wx, kdx, bbx, aax, rx, vx, s_ctx)

    s5f, s5b = _s5_matrices(s5_lam_re[0], s5_lam_im[0], s5_log_dt[0], s5_b_re[0], s5_b_im[0],
                            s5_c_re[0], s5_c_im[0])
    uc, ux = _s5_rows(u5c), _s5_rows(u5)
    h0 = jnp.zeros((uc.shape[0], 2, b, s5f[3].shape[2]), F32)
    ysc, hcf = _s5_scan(uc, *s5f, h0, b, False)
    _, hcb = _s5_scan(uc, ysc, *s5b, h0, b, True)
    ysf, _ = _s5_scan(ux, *s5f, hcf, b, False)
    ysfb, _ = _s5_scan(ux, ysf, *s5b, hcb, b, True)
    ys = _s5_unrows(ysfb, b)

    return _out_stage(x, gate[:, None], yf, yb, bonus, z2, ys, lnx_g[0], lnx_b[0], s5_d[0], s5_w_glu[0],
                      w_out[0], final_g, e)
```

```python
import functools
import math

import jax
import jax.numpy as jnp
from jax import lax
from jax.experimental import pallas as pl
from jax.experimental.pallas import tpu as pltpu

F32 = jnp.float32
BF16 = jnp.bfloat16

GRID_W = 64
HEAD = 64
LORA = 64
S5_GROUP = 16
S5_STATE = 64
S5_CHUNK = 16
S5_ROWS = 1024
WKV_CHUNK = 64
WKV_ROWS = 4
WKV_PACK = 2
EPS = 1e-6
LN_X_EPS = 64e-5
EXP_M05 = math.exp(-0.5)

LANES = 128
VMEM_LIMIT = 56 * 1024 * 1024


def _params(*sem):
    return pltpu.CompilerParams(dimension_semantics=sem, vmem_limit_bytes=VMEM_LIMIT)


def _sigmoid(x):
    return 1.0 / (1.0 + jnp.exp(-x))


def _dot_bf16(a, b):
    return jnp.dot(a.astype(BF16), b.astype(BF16), preferred_element_type=F32)


def _dot_f32(a, b):
    return jnp.dot(a, b, preferred_element_type=F32, precision=lax.Precision.HIGHEST)


def _segsum(x, e_ref):
    return jnp.dot(x.astype(BF16), e_ref[...], preferred_element_type=F32)


def _mod_kernel(c_ref, w_ref, b_ref, o_ref):
    c = c_ref[...]
    o_ref[...] = _dot_f32(c * _sigmoid(c), w_ref[...]) + b_ref[...]


def _modulation(cond, w_ada, b_ada):
    rows, d = cond.shape
    n = w_ada.shape[1]
    tn = 512
    return pl.pallas_call(
        _mod_kernel,
        grid=(n // tn,),
        in_specs=[pl.BlockSpec((rows, d), lambda j: (0, 0)),
                  pl.BlockSpec((d, tn), lambda j: (0, j)),
                  pl.BlockSpec((1, tn), lambda j: (0, j))],
        out_specs=pl.BlockSpec((rows, tn), lambda j: (0, j)),
        out_shape=jax.ShapeDtypeStruct((rows, n), F32),
        compiler_params=_params("arbitrary"),
        name="modulation",
    )(cond, w_ada, b_ada.reshape(1, n))


def _inproj_kernel(x_ref, g_ref, sc_ref, sh_ref, w1_ref, w2_ref, z1_ref, z2_ref, u_ref):
    x = x_ref[0]
    ms = jnp.mean(x * x, axis=-1, keepdims=True)
    h = x * lax.rsqrt(ms + EPS) * g_ref[...]
    h = (h * (1.0 + sc_ref[0]) + sh_ref[0]).astype(BF16)
    z1_ref[0] = jnp.dot(h, w1_ref[...], preferred_element_type=F32).astype(z1_ref.dtype)
    z2 = jnp.dot(h, w2_ref[...], preferred_element_type=F32)
    z2_ref[0] = z2.astype(z2_ref.dtype)
    nblk, nch = u_ref.shape[0], u_ref.shape[1]
    w = z2.shape[1] // 3
    for j in range(nblk):
        u_ref[j, :, 0] = z2[:, w + LANES * j:w + LANES * (j + 1)].reshape(nch, S5_CHUNK, LANES).astype(u_ref.dtype)


def _inproj(x, norm_g, scale, shift, w1, w2):
    b, l, d = x.shape
    n1, n2 = w1.shape[1], w2.shape[1]
    tm = min(l, 1024)
    nblk = n2 // 3 // LANES
    return pl.pallas_call(
        _inproj_kernel,
        grid=(b, l // tm),
        in_specs=[pl.BlockSpec((1, tm, d), lambda i, j: (i, j, 0)),
                  pl.BlockSpec((1, d), lambda i, j: (0, 0)),
                  pl.BlockSpec((1, 1, d), lambda i, j: (i, 0, 0)),
                  pl.BlockSpec((1, 1, d), lambda i, j: (i, 0, 0)),
                  pl.BlockSpec((d, n1), lambda i, j: (0, 0)),
                  pl.BlockSpec((d, n2), lambda i, j: (0, 0))],
        out_specs=[pl.BlockSpec((1, tm, n1), lambda i, j: (i, j, 0)),
                   pl.BlockSpec((1, tm, n2), lambda i, j: (i, j, 0)),
                   pl.BlockSpec((nblk, tm // S5_CHUNK, 1, S5_CHUNK, LANES), lambda i, j: (0, j, i, 0, 0))],
        out_shape=[jax.ShapeDtypeStruct((b, l, n1), BF16),
                   jax.ShapeDtypeStruct((b, l, n2), BF16),
                   jax.ShapeDtypeStruct((nblk, l // S5_CHUNK, b, S5_CHUNK, LANES), BF16)],
        compiler_params=_params("arbitrary", "arbitrary"),
        name="inproj",
    )(x, norm_g.reshape(1, d), scale, shift, w1, w2)


def _prep_body(zb, up, down, mu_ref, w0_ref, w2_ref, a0_ref, a2_ref, kk_ref, ka_ref, rk_ref, e_ref, sh_ref, outs,
               grid2d):
    t, sw = zb.shape
    dr = (sw - 4 * LORA) // 3
    z = zb.astype(F32)
    slot = lax.broadcasted_iota(jnp.int32, (1, sw), 1) & 3
    pn = jnp.dot(sh_ref[...], zb, preferred_element_type=F32)
    prev, nxt = pn[:t], pn[t:]
    if grid2d:
        sh = jnp.where(slot == 0, prev, jnp.where(slot == 1, nxt, jnp.where(slot == 2, up, down)))
    else:
        sh = jnp.where((slot & 1) == 0, prev, nxt)
    zs = z + mu_ref[...] * (sh - z)

    r = zs[:, 0:dr]
    k = zs[:, dr:2 * dr]
    v = zs[:, 2 * dr:3 * dr]
    kk = k * kk_ref[...]
    ss = _segsum(kk * kk, e_ref)
    kk = kk / jnp.maximum(jnp.sqrt(ss), 1e-12)
    r_o, v_o, aa_o, w_o, kd_o, bb_o, bonus_o = outs
    r_o[0] = r.astype(r_o.dtype)
    v_o[0] = v.astype(v_o.dtype)
    aa_o[0] = (-kk).astype(aa_o.dtype)
    ksum = None
    for d in range(2):
        zw = zs[:, 3 * dr + LORA * d:3 * dr + LORA * (d + 1)]
        za = zs[:, 3 * dr + 2 * LORA + LORA * d:3 * dr + 2 * LORA + LORA * (d + 1)]
        wl = w0_ref[d:d + 1, :] + _dot_bf16(jnp.tanh(zw), w2_ref[d])
        w_o[d, 0] = -EXP_M05 * _sigmoid(wl)
        asig = _sigmoid(a0_ref[d:d + 1, :] + _dot_bf16(za, a2_ref[d]))
        kd = k * (1.0 + (asig - 1.0) * ka_ref[...])
        kd_o[d, 0] = kd.astype(kd_o.dtype)
        bb_o[d, 0] = (kk * asig).astype(bb_o.dtype)
        ksum = kd if ksum is None else ksum + kd
    bonus_o[0] = (_segsum(r * (0.5 * ksum) * rk_ref[...], e_ref) * v).astype(bonus_o.dtype)


def _prep2d_kernel(zc_ref, zu_ref, zd_ref, *rest):
    params, outs = rest[:10], rest[10:]
    j = pl.program_id(1)
    nj = pl.num_programs(1)
    zb = zc_ref[0]
    z = zb.astype(F32)
    t = z.shape[0]
    row = lax.broadcasted_iota(jnp.int32, (t, 1), 0)
    up = jnp.concatenate([zu_ref[0].astype(F32), z[:t - GRID_W]], axis=0)
    up = jnp.where(jnp.logical_and(j == 0, row < GRID_W), 0.0, up)
    down = jnp.concatenate([z[GRID_W:], zd_ref[0].astype(F32)], axis=0)
    down = jnp.where(jnp.logical_and(j == nj - 1, row >= t - GRID_W), 0.0, down)
    _prep_body(zb, up, down, *params, outs, grid2d=True)


def _prep1d_kernel(zc_ref, *rest):
    params, outs = rest[:10], rest[10:]
    _prep_body(zc_ref[0], None, None, *params, outs, grid2d=False)


def _prep(z1, pp, grid2d):
    b, l, sw = z1.shape
    dr = (sw - 4 * LORA) // 3
    tt = 256 if grid2d else l
    nj = l // tt
    rb = tt // GRID_W
    const = lambda *shape: pl.BlockSpec(shape, lambda i, j: (0,) * len(shape))
    p_specs = [const(1, sw), const(2, dr), const(2, LORA, dr), const(2, dr), const(2, LORA, dr),
               const(1, dr), const(1, dr), const(1, dr), const(dr, dr), const(2 * tt, tt)]
    edge = GRID_W if grid2d else tt
    tok = jnp.arange(tt)
    nbr = jnp.concatenate([jnp.where(tok % edge != 0, tok - 1, -1), jnp.where(tok % edge != edge - 1, tok + 1, -1)])
    shift = (nbr[:, None] == tok[None, :]).astype(BF16)
    cur = pl.BlockSpec((1, tt, sw), lambda i, j: (i, j, 0))
    if grid2d:
        nrow = l // GRID_W
        in_specs = [cur,
                    pl.BlockSpec((1, GRID_W, sw), lambda i, j: (i, jnp.maximum(j * rb - 1, 0), 0)),
                    pl.BlockSpec((1, GRID_W, sw), lambda i, j: (i, jnp.minimum((j + 1) * rb, nrow - 1), 0))]
        args = (z1, z1, z1)
        body = _prep2d_kernel
    else:
        in_specs = [cur]
        args = (z1,)
        body = _prep1d_kernel
    o1 = pl.BlockSpec((1, tt, dr), lambda i, j: (i, j, 0))
    o2 = pl.BlockSpec((2, 1, tt, dr), lambda i, j: (0, i, j, 0))
    s1 = jax.ShapeDtypeStruct((b, l, dr), BF16)
    s2 = jax.ShapeDtypeStruct((2, b, l, dr), BF16)
    s2w = jax.ShapeDtypeStruct((2, b, l, dr), F32)
    return pl.pallas_call(
        body,
        grid=(b, nj),
        in_specs=in_specs + p_specs,
        out_specs=[o1, o1, o1, o2, o2, o2, o1],
        out_shape=[s1, s1, s1, s2w, s2, s2, s1],
        compiler_params=_params("arbitrary", "arbitrary"),
        name="prep2d" if grid2d else "prep1d",
    )(*args, *pp, shift)


def _dot_nt(a, b):
    return lax.dot_general(a, b, (((1,), (1,)), ((), ())), preferred_element_type=F32)


def _dot_tn(a, b):
    return lax.dot_general(a, b, (((0,), (0,)), ((), ())), preferred_element_type=F32)


def _wkv_operands(lw_ref, kd_ref, bb_ref, aa_ref, r_ref, v_ref, bi, backward):
    c = aa_ref.shape[1]
    row = lax.broadcasted_iota(jnp.int32, (c, c), 0)
    col = lax.broadcasted_iota(jnp.int32, (c, c), 1)
    lw = lw_ref[0, bi]
    lw_hi = lw.astype(BF16)
    lw_lo = (lw - lw_hi.astype(F32)).astype(BF16)
    lc = (col >= row if backward else col <= row).astype(BF16)
    cum = jnp.dot(lc, lw_hi, preferred_element_type=F32) + jnp.dot(lc, lw_lo, preferred_element_type=F32)
    tot = cum[0:1, :] if backward else cum[c - 1:c, :]
    pinv = jnp.exp(-cum)
    pend = jnp.exp(tot - cum)
    bb, kd = bb_ref[0, bi].astype(F32), kd_ref[0, bi].astype(F32)
    rt = r_ref[bi].astype(F32) * jnp.exp(cum)
    return dict(
        tot=tot, at=(aa_ref[bi].astype(F32) * jnp.exp(cum - lw)).astype(BF16),
        rt=rt, rt_b=rt.astype(BF16),
        bt=(bb * pinv).astype(BF16), kt=(kd * pinv).astype(BF16),
        be=(bb * pend).astype(BF16), ke=(kd * pend).astype(BF16), v=v_ref[bi].astype(BF16))


def _wkv_kernel(lwf_ref, kdf_ref, bbf_ref, aaf_ref, rf_ref, vf_ref, lwb_ref, kdb_ref, bbb_ref, aab_ref, rb_ref,
                vb_ref, s0_ref, yf_ref, yb_ref, sout_ref, s_scr):
    ci = pl.program_id(1)
    nb, c, width = aaf_ref.shape
    gw = WKV_PACK * HEAD
    ng = width // gw
    assert c == HEAD and width % gw == 0

    @pl.when(ci == 0)
    def _():
        s_scr[...] = s0_ref[...]

    ops = {(0, bi): _wkv_operands(lwf_ref, kdf_ref, bbf_ref, aaf_ref, rf_ref, vf_ref, bi, False) for bi in range(nb)}
    ops.update({(1, bi): _wkv_operands(lwb_ref, kdb_ref, bbb_ref, aab_ref, rb_ref, vb_ref, bi, True)
                for bi in range(nb)})
    chains = [(d, bi, gi) for gi in range(ng) for bi in range(nb) for d in range(2)]
    n = range(len(chains))
    cut = lambda name: [ops[d, bi][name][:, gi * gw:(gi + 1) * gw] for d, bi, gi in chains]
    at, rt, rt_b, bt, kt, be, ke, v, tot = (cut(k) for k in ("at", "rt", "rt_b", "bt", "kt", "be", "ke", "v", "tot"))
    mm = lambda a, b: jnp.dot(a, b, preferred_element_type=F32)

    r64 = lax.broadcasted_iota(jnp.int32, (c, gw), 0)
    c64 = lax.broadcasted_iota(jnp.int32, (c, gw), 1) & (HEAD - 1)
    eye_f = (r64 == c64).astype(F32)
    causal = {0: (c64 < r64, c64 <= r64), 1: (c64 > r64, c64 >= r64)}
    strict = [causal[d][0] for d, _, _ in chains]
    incl = [causal[d][1] for d, _, _ in chains]
    same_head = (lax.broadcasted_iota(jnp.int32, (gw, gw), 0) // HEAD
                 == lax.broadcasted_iota(jnp.int32, (gw, gw), 1) // HEAD)

    def bd(xp):
        return jnp.where(same_head, jnp.concatenate([xp] * (gw // c), axis=0), jnp.zeros((), xp.dtype))

    def diag(o):
        head = lax.broadcasted_iota(jnp.int32, (c, gw), 1) // HEAD
        out = o[:c]
        for h in range(1, gw // c):
            out = jnp.where(head == h, o[h * c:(h + 1) * c], out)
        return out

    rows = lambda a, b: jnp.concatenate([a, b], axis=0)
    cols = lambda a, b: jnp.concatenate([a, b], axis=1)
    g = [_dot_nt(rows(at[i], rt_b[i]), rows(bd(bt[i]), bd(kt[i]))) for i in n]
    x = [jnp.where(strict[i], g[i][:c, :gw], 0.0) for i in n]
    a_ak = [jnp.where(strict[i], g[i][:c, gw:], 0.0).astype(BF16) for i in n]
    m_rb = [jnp.where(incl[i], g[i][c:, :gw], 0.0).astype(BF16) for i in n]
    m_rk = [jnp.where(incl[i], g[i][c:, gw:], 0.0).astype(BF16) for i in n]
    t = [eye_f + x[i] for i in n]
    p = [x[i].astype(BF16) for i in n]
    p = [mm(p[i], bd(p[i])).astype(BF16) for i in n]
    for _ in range((c - 1).bit_length() - 2):
        o = [mm(rows(p[i], t[i].astype(BF16)), bd(p[i])) for i in n]
        p = [o[i][:c].astype(BF16) for i in n]
        t = [t[i] + o[i][c:] for i in n]
    t = [t[i] + mm(t[i].astype(BF16), bd(p[i])) for i in n]
    ov = [mm(rows(a_ak[i], m_rk[i]), bd(v[i])) for i in n]
    za = [mm(t[i].astype(BF16), cols(bd(at[i]), bd(ov[i][:c].astype(BF16)))).astype(BF16) for i in n]
    w2 = [mm(m_rb[i], cols(bd(za[i][:, :gw]), bd(za[i][:, gw:]))) for i in n]
    yp = [w2[i][:, gw:] + ov[i][c:] for i in n]
    tn = [_dot_tn(rows(be[i], ke[i]), rows(za[i], cols(jnp.zeros((c, gw), BF16), v[i]))) for i in n]
    gd = [diag(tn[i][:, :gw]) for i in n]
    hv = [diag(tn[i][:, gw:]) for i in n]
    head_sum = same_head.astype(BF16)
    tdiag = [jnp.where(r64 == c64, tot[i], 0.0) for i in n]
    thi = [tdiag[i].astype(BF16) for i in n]
    tlo = [(tdiag[i] - thi[i].astype(F32)).astype(BF16) for i in n]
    pc = [jnp.exp(mm(thi[i], head_sum) + mm(tlo[i], head_sum)) for i in n]
    s = [s_scr[d, bi, gi] for d, bi, gi in chains]
    os_ = [mm(rows((rt[i] + w2[i][:, :gw]).astype(BF16), gd[i].astype(BF16)), bd(s[i].astype(BF16))) for i in n]
    y = [os_[i][:c] + yp[i] for i in n]
    for i, (d, bi, gi) in enumerate(chains):
        s_scr[d, bi, gi] = pc[i] * s[i] + os_[i][c:] + hv[i]
    for d, y_ref in enumerate((yf_ref, yb_ref)):
        for bi in range(nb):
            y_ref[bi] = jnp.concatenate([y[chains.index((d, bi, gi))] for gi in range(ng)],
                                        axis=1).astype(y_ref.dtype)

    @pl.when(ci == pl.num_programs(1) - 1)
    def _():
        sout_ref[...] = s_scr[...]


def _wkv(lw, kd, bb, aa, r, v, s0):
    _, b, l, width = lw.shape
    c = WKV_CHUNK
    nc = l // c
    nb = WKV_ROWS
    dirs = lambda d: pl.BlockSpec((1, nb, c, width), lambda i, j: (d, i, (nc - 1 - j) if d else j, 0))
    both = lambda d: pl.BlockSpec((nb, c, width), lambda i, j: (i, (nc - 1 - j) if d else j, 0))
    stspec = pl.BlockSpec((2, nb) + s0.shape[2:], lambda i, j: (0, i, 0, 0, 0))
    return pl.pallas_call(
        _wkv_kernel,
        grid=(b // nb, nc),
        in_specs=[dirs(0), dirs(0), dirs(0), both(0), both(0), both(0),
                  dirs(1), dirs(1), dirs(1), both(1), both(1), both(1), stspec],
        out_specs=[both(0), both(1), stspec],
        out_shape=[jax.ShapeDtypeStruct((b, l, width), BF16), jax.ShapeDtypeStruct((b, l, width), BF16),
                   jax.ShapeDtypeStruct(s0.shape, F32)],
        scratch_shapes=[pltpu.VMEM((2, nb) + s0.shape[2:], F32)],
        compiler_params=_params("arbitrary", "arbitrary"),
        name="wkv",
    )(lw, kd, bb, aa, r, v, lw, kd, bb, aa, r, v, s0)


def _s5_kernel(*refs, nb, reverse, toeplitz):
    if toeplitz:
        (u_ref, lag_ref, win_ref, wout_ref, a_ref, h0_ref, y_ref, hout_ref,
         x_scr, hh_scr, hre_scr, him_scr, win_scr, wout_scr, wt_scr) = refs
    else:
        (u_ref, yprev_ref, win_ref, wout_ref, a_ref, h0_ref, y_ref, hout_ref,
         x_scr, hh_scr, hre_scr, him_scr, win_scr, wout_scr) = refs
    i = pl.program_id(1)
    ns = hre_scr.shape[1]
    t = win_ref.shape[1]

    @pl.when(i == 0)
    def _():
        hre_scr[...] = h0_ref[0, 0]
        him_scr[...] = h0_ref[0, 1]
        lane_group = (lax.broadcasted_iota(jnp.int32, (S5_GROUP, 2 * ns), 1) % ns) // S5_STATE
        for s in range(t):
            for g in range(LANES // S5_GROUP):
                r0 = s * LANES + g * S5_GROUP
                own = lane_group == g
                win_scr[r0:r0 + S5_GROUP, :] = jnp.where(own, win_ref[0, s], jnp.zeros((), BF16))
                wout_scr[r0:r0 + S5_GROUP, :] = jnp.where(own, wout_ref[0, s], jnp.zeros((), BF16))
        if toeplitz:
            for sig in range(t):
                for tau in range(t):
                    wt_scr[sig * LANES:(sig + 1) * LANES, tau * LANES:(tau + 1) * LANES] = lag_ref[0, tau - sig + t - 1]

    u = u_ref[0].astype(BF16)
    x_scr[...] = jnp.dot(u, win_scr[...], preferred_element_type=F32)
    ar, ai = a_ref[0, 0:1, :], a_ref[0, 1:2, :]
    nch = x_scr.shape[0] // nb

    hr, hi = hre_scr[...], him_scr[...]
    for k in range(nch):
        c = nch - 1 - k if reverse else k
        rows = slice(c * nb, (c + 1) * nb)
        hh_scr[rows, 0:ns] = hr
        hh_scr[rows, ns:2 * ns] = hi
        hr, hi = (hr * ar - hi * ai + x_scr[rows, 0:ns], hr * ai + hi * ar + x_scr[rows, ns:2 * ns])
    hre_scr[...] = hr
    him_scr[...] = hi
    y = _dot_nt(hh_scr[...].astype(BF16), wout_scr[...])
    if toeplitz:
        y = y + jnp.dot(u, wt_scr[...], preferred_element_type=F32)
    else:
        y = y + yprev_ref[0].astype(F32)
    y_ref[0] = y.astype(y_ref.dtype)

    @pl.when(i == pl.num_programs(1) - 1)
    def _():
        hout_ref[0, 0] = hr
        hout_ref[0, 1] = hi


def _s5_scan(u, first, win, wout, acoef, h0, nb, reverse):
    nblk, rows, kw = u.shape
    ns = win.shape[3] // 2
    rt = min(rows, S5_ROWS)
    nt = rows // rt
    tile = pl.BlockSpec((1, rt, kw), lambda g, i: (g, (nt - 1 - i) if reverse else i, 0))
    per_blk = lambda a: pl.BlockSpec((1,) + a.shape[1:], lambda g, i: (g,) + (0,) * (a.ndim - 1))
    weights = [win, wout, acoef, h0]
    toeplitz = not reverse
    return pl.pallas_call(
        functools.partial(_s5_kernel, nb=nb, reverse=reverse, toeplitz=toeplitz),
        grid=(nblk, nt),
        in_specs=[tile, per_blk(first) if toeplitz else tile] + [per_blk(a) for a in weights],
        out_specs=[tile, per_blk(h0)],
        out_shape=[jax.ShapeDtypeStruct((nblk, rows, kw), BF16), jax.ShapeDtypeStruct(h0.shape, F32)],
        scratch_shapes=[pltpu.VMEM((rt, 2 * ns), F32), pltpu.VMEM((rt, 2 * ns), F32),
                        pltpu.VMEM((nb, ns), F32), pltpu.VMEM((nb, ns), F32),
                        pltpu.VMEM((kw, 2 * ns), BF16), pltpu.VMEM((kw, 2 * ns), BF16)]
        + ([pltpu.VMEM((kw, kw), BF16)] if toeplitz else []),
        compiler_params=_params("arbitrary", "arbitrary"),
        name="s5_bwd" if reverse else "s5_fwd",
    )(u, first, *weights)


def _s5_matrices(lam_re, lam_im, log_dt, b_re, b_im, c_re, c_im):
    hp = lax.Precision.HIGHEST
    t = S5_CHUNK
    dt = jnp.exp(log_dt.astype(F32))[..., None]
    lr, li = lam_re.astype(F32), lam_im.astype(F32)
    j = jnp.arange(t + 1, dtype=F32)[:, None, None, None]
    mag = jnp.exp(j * dt * lr)
    pr, pi = mag * jnp.cos(j * dt * li), mag * jnp.sin(j * dt * li)
    ar, ai = pr[1], pi[1]
    den = lr * lr + li * li
    fr = ((ar - 1.0) * lr + ai * li) / den
    fi = (ai * lr - (ar - 1.0) * li) / den
    bbr = fr[..., None] * b_re - fi[..., None] * b_im
    bbi = fr[..., None] * b_im + fi[..., None] * b_re
    car = c_re[None, None] * pr[:, :, :, None, :] - c_im[None, None] * pi[:, :, :, None, :]
    cai = c_re[None, None] * pi[:, :, :, None, :] + c_im[None, None] * pr[:, :, :, None, :]
    g = lr.shape[1]
    ca = jnp.concatenate([car, -cai], axis=-1).transpose(1, 2, 0, 3, 4).reshape(2, g, (t + 1) * S5_GROUP, 2 * S5_STATE)
    kj = jnp.einsum('dgxq,dgqe->dgxe', ca, jnp.concatenate([bbr, bbi], axis=2), precision=hp)
    kj = kj.reshape(2, g, t + 1, S5_GROUP, S5_GROUP).transpose(2, 0, 1, 3, 4)
    gpb = LANES // S5_GROUP
    nblk = g // gpb
    ns = gpb * S5_STATE
    own_chan = (jnp.arange(gpb)[:, None, None] == (jnp.arange(LANES) // S5_GROUP)[None, None, :])

    kl = jnp.concatenate([kj[t - 1:0:-1, 1], (kj[0, 0] + kj[0, 1])[None], kj[1:t, 0]], axis=0)
    kl = kl.reshape(2 * t - 1, nblk, gpb, S5_GROUP, S5_GROUP).transpose(0, 1, 4, 2, 3)
    kl = kl.reshape(2 * t - 1, nblk, 1, S5_GROUP, LANES)
    lagblk = jnp.where(own_chan, kl, 0.0).transpose(1, 0, 2, 3, 4).reshape(nblk, 2 * t - 1, LANES, LANES)

    def state_in(d, pw):
        prs = pr[pw, d].reshape(t, nblk, 1, ns).transpose(1, 0, 2, 3)
        pis = pi[pw, d].reshape(t, nblk, 1, ns).transpose(1, 0, 2, 3)
        lay = lambda q: q.reshape(nblk, gpb, S5_STATE, S5_GROUP).transpose(0, 3, 1, 2).reshape(nblk, 1, S5_GROUP, ns)
        br, bi = lay(bbr[d]), lay(bbi[d])
        return jnp.concatenate([prs * br - pis * bi, prs * bi + pis * br], axis=-1).astype(BF16)

    def state_out(d, pw):
        lay = lambda q: q.reshape(t, nblk, gpb, S5_GROUP, S5_STATE).transpose(1, 0, 3, 2, 4).reshape(
            nblk, t, S5_GROUP, ns)
        return jnp.concatenate([lay(car[pw, d]), lay(-cai[pw, d])], axis=-1).astype(BF16)

    coef = lambda d: jnp.stack([pr[t, d].reshape(nblk, ns), pi[t, d].reshape(nblk, ns)], axis=1)
    fwd = (lagblk.astype(BF16), state_in(0, t - 1 - jnp.arange(t)), state_out(0, jnp.arange(t) + 1), coef(0))
    bwd = (state_in(1, jnp.arange(t)), state_out(1, t - jnp.arange(t)), coef(1))
    return fwd, bwd


def _s5_rows(u5):
    nblk, nc, b, t, w = u5.shape
    return u5.reshape(nblk, nc * b, t * w)


def _s5_unrows(y, b):
    nblk, rows, kw = y.shape
    return y.reshape(nblk, rows // b, b, S5_CHUNK, kw // S5_CHUNK)


def _out_kernel(x_ref, gate_ref, yf_ref, yb_ref, bonus_ref, gr_ref, u_ref, gs_ref, ys_ref, lg_ref, lb_ref, d_ref,
                wg_ref, wo_ref, fg_ref, e_ref, o_ref):
    dr = bonus_ref.shape[2]
    y = yf_ref[0].astype(F32) + yb_ref[0].astype(F32)
    mean = _segsum(y, e_ref) * (1.0 / HEAD)
    dev = y - mean
    var = _segsum(dev * dev, e_ref) * (1.0 / HEAD)
    yn = dev * lax.rsqrt(var + LN_X_EPS) * lg_ref[...] + lb_ref[...]
    gr = gr_ref[0].astype(F32)
    y_r = (yn + bonus_ref[0].astype(F32)) * (gr * _sigmoid(gr))

    tm = u_ref.shape[1]
    ys = jnp.concatenate([ys_ref[j, :, 0].reshape(tm, LANES) for j in range(ys_ref.shape[0])], axis=1)
    s = ys.astype(F32) + d_ref[...] * u_ref[0].astype(F32)
    s = 0.5 * s * (1.0 + jnp.tanh(math.sqrt(2.0 / math.pi) * (s + 0.044715 * (s * s * s))))
    gl = jnp.dot(s.astype(BF16), wg_ref[...], preferred_element_type=F32)
    ds = gl.shape[1] // 2
    gs = gs_ref[0].astype(F32)
    y_s = gl[:, :ds] * _sigmoid(gl[:, ds:]) * (gs * _sigmoid(gs))

    out = (jnp.dot(y_r.astype(BF16), wo_ref[0:dr, :], preferred_element_type=F32)
           + jnp.dot(y_s.astype(BF16), wo_ref[dr:, :], preferred_element_type=F32))
    xo = x_ref[0] + gate_ref[0] * out
    ms = jnp.mean(xo * xo, axis=-1, keepdims=True)
    o_ref[0] = (xo * lax.rsqrt(ms + EPS) * fg_ref[...]).astype(o_ref.dtype)


def _out_stage(x, gate, yf, yb, bonus, z2, ys, lnx_g, lnx_b, s5_d, w_glu, w_out, final_g, e):
    b, l, d = x.shape
    dr = bonus.shape[2]
    ds = ys.shape[0] * LANES
    assert dr == ds
    tm = 1024
    tok = lambda w, cb: pl.BlockSpec((1, tm, w), lambda i, j: (i, j, cb))
    const = lambda *shape: pl.BlockSpec(shape, lambda i, j: (0,) * len(shape))
    return pl.pallas_call(
        _out_kernel,
        grid=(b, l // tm),
        in_specs=[tok(d, 0), pl.BlockSpec((1, 1, d), lambda i, j: (i, 0, 0)),
                  tok(dr, 0), tok(dr, 0), tok(dr, 0), tok(dr, 0), tok(ds, 1), tok(ds, 2),
                  pl.BlockSpec((ys.shape[0], tm // S5_CHUNK, 1, S5_CHUNK, LANES), lambda i, j: (0, j, i, 0, 0)),
                  const(1, dr), const(1, dr), const(1, ds), const(ds, 2 * ds), const(dr + ds, d),
                  const(1, d), const(dr, dr)],
        out_specs=tok(d, 0),
        out_shape=jax.ShapeDtypeStruct((b, l, d), x.dtype),
        compiler_params=_params("arbitrary", "arbitrary"),
        name="out_stage",
    )(x, gate, yf, yb, bonus, z2, z2, z2, ys, lnx_g.reshape(1, dr), lnx_b.reshape(1, dr), s5_d.reshape(1, ds),
      w_glu.astype(BF16), w_out.astype(BF16), final_g.reshape(1, d), e)


def kernel(x, c, ctx, c_ctx, norm_g, w_ada, b_ada, w_in, mu_shift, rwkv_w0, rwkv_w2, rwkv_a0, rwkv_a2, rwkv_k_k, rwkv_k_a, rwkv_r_k, lnx_g, lnx_b, s5_lam_re, s5_lam_im, s5_log_dt, s5_b_re, s5_b_im, s5_c_re, s5_c_im, s5_d, s5_w_glu, w_out, final_g):
    assert norm_g.shape[0] == 1, "one layer"
    b, l, d = x.shape
    lc = ctx.shape[1]
    dr = rwkv_k_k.shape[1]
    ds = s5_d.shape[1]
    sw = mu_shift.shape[1]
    assert sw == 3 * dr + 4 * LORA and l % (4 * GRID_W) == 0 and lc % WKV_CHUNK == 0 and b % 8 == 0

    cond = jnp.zeros((2 * b, d), F32).at[:b].set(c).at[b].set(c_ctx)
    m = _modulation(cond, w_ada[0], b_ada[0])
    shift, scale, gate = m[:b, :d], m[:b, d:2 * d], m[:b, 2 * d:]
    cshift = jnp.broadcast_to(m[b, :d], (b, d))
    cscale = jnp.broadcast_to(m[b, d:2 * d], (b, d))

    w1 = w_in[0][:, :sw].astype(BF16)
    w2 = w_in[0][:, sw:].astype(BF16)
    z1c, _, u5c = _inproj(ctx, norm_g[0], cscale[:, None], cshift[:, None], w1, w2)
    z1, z2, u5 = _inproj(x, norm_g[0], scale[:, None], shift[:, None], w1, w2)

    hid = jnp.arange(dr) // HEAD
    e = (hid[:, None] == hid[None, :]).astype(BF16)
    pp = (mu_shift, rwkv_w0[0], rwkv_w2[0], rwkv_a0[0], rwkv_a2[0], rwkv_k_k, rwkv_k_a,
          rwkv_r_k[0].reshape(1, dr), e)
    rc, vc, aac, wc, kdc, bbc, _ = _prep(z1c, pp, grid2d=False)
    rx, vx, aax, wx, kdx, bbx, bonus = _prep(z1, pp, grid2d=True)

    s0 = jnp.zeros((2, b, dr // (WKV_PACK * HEAD), HEAD, WKV_PACK * HEAD), F32)
    _, _, s_ctx = _wkv(wc, kdc, bbc, aac, rc, vc, s0)
    yf, yb, _ = _wkv(wx, kdx, bbx, aax, rx, vx, s_ctx)

    s5f, s5b = _s5_matrices(s5_lam_re[0], s5_lam_im[0], s5_log_dt[0], s5_b_re[0], s5_b_im[0],
                            s5_c_re[0], s5_c_im[0])
    uc, ux = _s5_rows(u5c), _s5_rows(u5)
    h0 = jnp.zeros((uc.shape[0], 2, b, s5f[3].shape[2]), F32)
    ysc, hcf = _s5_scan(uc, *s5f, h0, b, False)
    _, hcb = _s5_scan(uc, ysc, *s5b, h0, b, True)
    ysf, _ = _s5_scan(ux, *s5f, hcf, b, False)
    ysfb, _ = _s5_scan(ux, ysf, *s5b, hcb, b, True)
    ys = _s5_unrows(ysfb, b)

    return _out_stage(x, gate[:, None], yf, yb, bonus, z2, ys, lnx_g[0], lnx_b[0], s5_d[0], s5_w_glu[0],
                      w_out[0], final_g, e)
```

```python
import functools
import math

import jax
import jax.numpy as jnp
from jax import lax
from jax.experimental import pallas as pl
from jax.experimental.pallas import tpu as pltpu

F32 = jnp.float32
BF16 = jnp.bfloat16

GRID_W = 64
HEAD = 64
LORA = 64
S5_GROUP = 16
S5_STATE = 64
S5_CHUNK = 16
S5_ROWS = 1024
TOKENS_IN = 512
TOKENS_PREP = 256
TOKENS_OUT = 1024
WKV_CHUNK = 64
WKV_ROWS = 4
WKV_PACK = 2
EPS = 1e-6
LN_X_EPS = 64e-5
EXP_M05 = math.exp(-0.5)

LANES = 128
VMEM_LIMIT = 56 * 1024 * 1024


def _params(*sem):
    return pltpu.CompilerParams(dimension_semantics=sem, vmem_limit_bytes=VMEM_LIMIT)


def _sigmoid(x):
    return 1.0 / (1.0 + jnp.exp(-x))


def _dot_bf16(a, b):
    return jnp.dot(a.astype(BF16), b.astype(BF16), preferred_element_type=F32)


def _dot_f32(a, b):
    return jnp.dot(a, b, preferred_element_type=F32, precision=lax.Precision.HIGHEST)


def _segsum(x, e_ref):
    return jnp.dot(x.astype(BF16), e_ref[...], preferred_element_type=F32)


def _mod_kernel(c_ref, w_ref, b_ref, o_ref):
    c = c_ref[...]
    o_ref[...] = _dot_f32(c * _sigmoid(c), w_ref[...]) + b_ref[...]


def _modulation(cond, w_ada, b_ada):
    rows, d = cond.shape
    n = w_ada.shape[1]
    tn = 512
    return pl.pallas_call(
        _mod_kernel,
        grid=(n // tn,),
        in_specs=[pl.BlockSpec((rows, d), lambda j: (0, 0)),
                  pl.BlockSpec((d, tn), lambda j: (0, j)),
                  pl.BlockSpec((1, tn), lambda j: (0, j))],
        out_specs=pl.BlockSpec((rows, tn), lambda j: (0, j)),
        out_shape=jax.ShapeDtypeStruct((rows, n), F32),
        compiler_params=_params("arbitrary"),
        name="modulation",
    )(cond, w_ada, b_ada.reshape(1, n))


def _inproj_kernel(x_ref, g_ref, sc_ref, sh_ref, w1_ref, w2_ref, z1_ref, z2_ref, u_ref):
    x = x_ref[0]
    ms = jnp.mean(x * x, axis=-1, keepdims=True)
    h = x * lax.rsqrt(ms + EPS) * g_ref[...]
    h = (h * (1.0 + sc_ref[0]) + sh_ref[0]).astype(BF16)
    z1_ref[0] = jnp.dot(h, w1_ref[...], preferred_element_type=F32).astype(z1_ref.dtype)
    z2 = jnp.dot(h, w2_ref[...], preferred_element_type=F32)
    z2_ref[0] = z2.astype(z2_ref.dtype)
    nblk, nch = u_ref.shape[0], u_ref.shape[1]
    w = z2.shape[1] // 3
    for j in range(nblk):
        u_ref[j, :, 0] = z2[:, w + LANES * j:w + LANES * (j + 1)].reshape(nch, S5_CHUNK, LANES).astype(u_ref.dtype)


def _inproj(x, norm_g, scale, shift, w1, w2):
    b, l, d = x.shape
    n1, n2 = w1.shape[1], w2.shape[1]
    tm = min(l, TOKENS_IN)
    nblk = n2 // 3 // LANES
    return pl.pallas_call(
        _inproj_kernel,
        grid=(b, l // tm),
        in_specs=[pl.BlockSpec((1, tm, d), lambda i, j: (i, j, 0)),
                  pl.BlockSpec((1, d), lambda i, j: (0, 0)),
                  pl.BlockSpec((1, 1, d), lambda i, j: (i, 0, 0)),
                  pl.BlockSpec((1, 1, d), lambda i, j: (i, 0, 0)),
                  pl.BlockSpec((d, n1), lambda i, j: (0, 0)),
                  pl.BlockSpec((d, n2), lambda i, j: (0, 0))],
        out_specs=[pl.BlockSpec((1, tm, n1), lambda i, j: (i, j, 0)),
                   pl.BlockSpec((1, tm, n2), lambda i, j: (i, j, 0)),
                   pl.BlockSpec((nblk, tm // S5_CHUNK, 1, S5_CHUNK, LANES), lambda i, j: (0, j, i, 0, 0))],
        out_shape=[jax.ShapeDtypeStruct((b, l, n1), BF16),
                   jax.ShapeDtypeStruct((b, l, n2), BF16),
                   jax.ShapeDtypeStruct((nblk, l // S5_CHUNK, b, S5_CHUNK, LANES), BF16)],
        compiler_params=_params("arbitrary", "arbitrary"),
        name="inproj",
    )(x, norm_g.reshape(1, d), scale, shift, w1, w2)


def _prep_body(zb, up, down, mu_ref, w0_ref, w2_ref, a0_ref, a2_ref, kk_ref, ka_ref, rk_ref, e_ref, sh_ref, outs,
               grid2d):
    t, sw = zb.shape
    dr = (sw - 4 * LORA) // 3
    z = zb.astype(F32)
    slot = lax.broadcasted_iota(jnp.int32, (1, sw), 1) & 3
    pn = jnp.dot(sh_ref[...], zb, preferred_element_type=F32)
    prev, nxt = pn[:t], pn[t:]
    if grid2d:
        sh = jnp.where(slot == 0, prev, jnp.where(slot == 1, nxt, jnp.where(slot == 2, up, down)))
    else:
        sh = jnp.where((slot & 1) == 0, prev, nxt)
    zs = z + mu_ref[...] * (sh - z)

    r = zs[:, 0:dr]
    k = zs[:, dr:2 * dr]
    v = zs[:, 2 * dr:3 * dr]
    kk = k * kk_ref[...]
    ss = _segsum(kk * kk, e_ref)
    kk = kk / jnp.maximum(jnp.sqrt(ss), 1e-12)
    r_o, v_o, aa_o, w_o, kd_o, bb_o, bonus_o = outs
    r_o[0] = r.astype(r_o.dtype)
    v_o[0] = v.astype(v_o.dtype)
    aa_o[0] = (-kk).astype(aa_o.dtype)
    ksum = None
    for d in range(2):
        zw = zs[:, 3 * dr + LORA * d:3 * dr + LORA * (d + 1)]
        za = zs[:, 3 * dr + 2 * LORA + LORA * d:3 * dr + 2 * LORA + LORA * (d + 1)]
        wl = w0_ref[d:d + 1, :] + _dot_bf16(jnp.tanh(zw), w2_ref[d])
        w_o[d, 0] = -EXP_M05 * _sigmoid(wl)
        asig = _sigmoid(a0_ref[d:d + 1, :] + _dot_bf16(za, a2_ref[d]))
        kd = k * (1.0 + (asig - 1.0) * ka_ref[...])
        kd_o[d, 0] = kd.astype(kd_o.dtype)
        bb_o[d, 0] = (kk * asig).astype(bb_o.dtype)
        ksum = kd if ksum is None else ksum + kd
    bonus_o[0] = (_segsum(r * (0.5 * ksum) * rk_ref[...], e_ref) * v).astype(bonus_o.dtype)


def _prep2d_kernel(zc_ref, zu_ref, zd_ref, *rest):
    params, outs = rest[:10], rest[10:]
    j = pl.program_id(1)
    nj = pl.num_programs(1)
    zb = zc_ref[0]
    z = zb.astype(F32)
    t = z.shape[0]
    row = lax.broadcasted_iota(jnp.int32, (t, 1), 0)
    up = jnp.concatenate([zu_ref[0].astype(F32), z[:t - GRID_W]], axis=0)
    up = jnp.where(jnp.logical_and(j == 0, row < GRID_W), 0.0, up)
    down = jnp.concatenate([z[GRID_W:], zd_ref[0].astype(F32)], axis=0)
    down = jnp.where(jnp.logical_and(j == nj - 1, row >= t - GRID_W), 0.0, down)
    _prep_body(zb, up, down, *params, outs, grid2d=True)


def _prep1d_kernel(zc_ref, *rest):
    params, outs = rest[:10], rest[10:]
    _prep_body(zc_ref[0], None, None, *params, outs, grid2d=False)


def _prep(z1, pp, grid2d):
    b, l, sw = z1.shape
    dr = (sw - 4 * LORA) // 3
    tt = TOKENS_PREP if grid2d else l
    nj = l // tt
    rb = tt // GRID_W
    const = lambda *shape: pl.BlockSpec(shape, lambda i, j: (0,) * len(shape))
    p_specs = [const(1, sw), const(2, dr), const(2, LORA, dr), const(2, dr), const(2, LORA, dr),
               const(1, dr), const(1, dr), const(1, dr), const(dr, dr), const(2 * tt, tt)]
    edge = GRID_W if grid2d else tt
    tok = jnp.arange(tt)
    nbr = jnp.concatenate([jnp.where(tok % edge != 0, tok - 1, -1), jnp.where(tok % edge != edge - 1, tok + 1, -1)])
    shift = (nbr[:, None] == tok[None, :]).astype(BF16)
    cur = pl.BlockSpec((1, tt, sw), lambda i, j: (i, j, 0))
    if grid2d:
        nrow = l // GRID_W
        in_specs = [cur,
                    pl.BlockSpec((1, GRID_W, sw), lambda i, j: (i, jnp.maximum(j * rb - 1, 0), 0)),
                    pl.BlockSpec((1, GRID_W, sw), lambda i, j: (i, jnp.minimum((j + 1) * rb, nrow - 1), 0))]
        args = (z1, z1, z1)
        body = _prep2d_kernel
    else:
        in_specs = [cur]
        args = (z1,)
        body = _prep1d_kernel
    o1 = pl.BlockSpec((1, tt, dr), lambda i, j: (i, j, 0))
    o2 = pl.BlockSpec((2, 1, tt, dr), lambda i, j: (0, i, j, 0))
    s1 = jax.ShapeDtypeStruct((b, l, dr), BF16)
    s2 = jax.ShapeDtypeStruct((2, b, l, dr), BF16)
    s2w = jax.ShapeDtypeStruct((2, b, l, dr), F32)
    return pl.pallas_call(
        body,
        grid=(b, nj),
        in_specs=in_specs + p_specs,
        out_specs=[o1, o1, o1, o2, o2, o2, o1],
        out_shape=[s1, s1, s1, s2w, s2, s2, s1],
        compiler_params=_params("arbitrary", "arbitrary"),
        name="prep2d" if grid2d else "prep1d",
    )(*args, *pp, shift)


def _dot_nt(a, b):
    return lax.dot_general(a, b, (((1,), (1,)), ((), ())), preferred_element_type=F32)


def _dot_tn(a, b):
    return lax.dot_general(a, b, (((0,), (0,)), ((), ())), preferred_element_type=F32)


def _wkv_operands(lw_ref, kd_ref, bb_ref, aa_ref, r_ref, v_ref, bi, backward):
    c = aa_ref.shape[1]
    row = lax.broadcasted_iota(jnp.int32, (c, c), 0)
    col = lax.broadcasted_iota(jnp.int32, (c, c), 1)
    lw = lw_ref[0, bi]
    lw_hi = lw.astype(BF16)
    lw_lo = (lw - lw_hi.astype(F32)).astype(BF16)
    lc = (col >= row if backward else col <= row).astype(BF16)
    cum = jnp.dot(lc, lw_hi, preferred_element_type=F32) + jnp.dot(lc, lw_lo, preferred_element_type=F32)
    tot = cum[0:1, :] if backward else cum[c - 1:c, :]
    pinv = jnp.exp(-cum)
    pend = jnp.exp(tot - cum)
    bb, kd = bb_ref[0, bi].astype(F32), kd_ref[0, bi].astype(F32)
    rt = r_ref[bi].astype(F32) * jnp.exp(cum)
    return dict(
        tot=tot, at=(aa_ref[bi].astype(F32) * jnp.exp(cum - lw)).astype(BF16),
        rt=rt, rt_b=rt.astype(BF16),
        bt=(bb * pinv).astype(BF16), kt=(kd * pinv).astype(BF16),
        be=(bb * pend).astype(BF16), ke=(kd * pend).astype(BF16), v=v_ref[bi].astype(BF16))


def _wkv_kernel(lwf_ref, kdf_ref, bbf_ref, aaf_ref, rf_ref, vf_ref, lwb_ref, kdb_ref, bbb_ref, aab_ref, rb_ref,
                vb_ref, s0_ref, yf_ref, yb_ref, sout_ref, s_scr):
    ci = pl.program_id(1)
    nb, c, width = aaf_ref.shape
    gw = WKV_PACK * HEAD
    ng = width // gw
    assert c == HEAD and width % gw == 0

    @pl.when(ci == 0)
    def _():
        s_scr[...] = s0_ref[...]

    ops = {(0, bi): _wkv_operands(lwf_ref, kdf_ref, bbf_ref, aaf_ref, rf_ref, vf_ref, bi, False) for bi in range(nb)}
    ops.update({(1, bi): _wkv_operands(lwb_ref, kdb_ref, bbb_ref, aab_ref, rb_ref, vb_ref, bi, True)
                for bi in range(nb)})
    chains = [(d, bi, gi) for gi in range(ng) for bi in range(nb) for d in range(2)]
    n = range(len(chains))
    cut = lambda name: [ops[d, bi][name][:, gi * gw:(gi + 1) * gw] for d, bi, gi in chains]
    at, rt, rt_b, bt, kt, be, ke, v, tot = (cut(k) for k in ("at", "rt", "rt_b", "bt", "kt", "be", "ke", "v", "tot"))
    mm = lambda a, b: jnp.dot(a, b, preferred_element_type=F32)

    r64 = lax.broadcasted_iota(jnp.int32, (c, gw), 0)
    c64 = lax.broadcasted_iota(jnp.int32, (c, gw), 1) & (HEAD - 1)
    eye_f = (r64 == c64).astype(F32)
    causal = {0: (c64 < r64, c64 <= r64), 1: (c64 > r64, c64 >= r64)}
    strict = [causal[d][0] for d, _, _ in chains]
    incl = [causal[d][1] for d, _, _ in chains]
    same_head = (lax.broadcasted_iota(jnp.int32, (gw, gw), 0) // HEAD
                 == lax.broadcasted_iota(jnp.int32, (gw, gw), 1) // HEAD)

    def bd(xp):
        return jnp.where(same_head, jnp.concatenate([xp] * (gw // c), axis=0), jnp.zeros((), xp.dtype))

    def diag(o):
        head = lax.broadcasted_iota(jnp.int32, (c, gw), 1) // HEAD
        out = o[:c]
        for h in range(1, gw // c):
            out = jnp.where(head == h, o[h * c:(h + 1) * c], out)
        return out

    rows = lambda a, b: jnp.concatenate([a, b], axis=0)
    cols = lambda a, b: jnp.concatenate([a, b], axis=1)
    g = [_dot_nt(rows(at[i], rt_b[i]), rows(bd(bt[i]), bd(kt[i]))) for i in n]
    x = [jnp.where(strict[i], g[i][:c, :gw], 0.0) for i in n]
    a_ak = [jnp.where(strict[i], g[i][:c, gw:], 0.0).astype(BF16) for i in n]
    m_rb = [jnp.where(incl[i], g[i][c:, :gw], 0.0).astype(BF16) for i in n]
    m_rk = [jnp.where(incl[i], g[i][c:, gw:], 0.0).astype(BF16) for i in n]
    t = [eye_f + x[i] for i in n]
    p = [x[i].astype(BF16) for i in n]
    p = [mm(p[i], bd(p[i])).astype(BF16) for i in n]
    for _ in range((c - 1).bit_length() - 2):
        o = [mm(rows(p[i], t[i].astype(BF16)), bd(p[i])) for i in n]
        p = [o[i][:c].astype(BF16) for i in n]
        t = [t[i] + o[i][c:] for i in n]
    t = [t[i] + mm(t[i].astype(BF16), bd(p[i])) for i in n]
    ov = [mm(rows(a_ak[i], m_rk[i]), bd(v[i])) for i in n]
    za = [mm(t[i].astype(BF16), cols(bd(at[i]), bd(ov[i][:c].astype(BF16)))).astype(BF16) for i in n]
    w2 = [mm(m_rb[i], cols(bd(za[i][:, :gw]), bd(za[i][:, gw:]))) for i in n]
    yp = [w2[i][:, gw:] + ov[i][c:] for i in n]
    tn = [_dot_tn(rows(be[i], ke[i]), rows(za[i], cols(jnp.zeros((c, gw), BF16), v[i]))) for i in n]
    gd = [diag(tn[i][:, :gw]) for i in n]
    hv = [diag(tn[i][:, gw:]) for i in n]
    head_sum = same_head.astype(BF16)
    tdiag = [jnp.where(r64 == c64, tot[i], 0.0) for i in n]
    thi = [tdiag[i].astype(BF16) for i in n]
    tlo = [(tdiag[i] - thi[i].astype(F32)).astype(BF16) for i in n]
    pc = [jnp.exp(mm(thi[i], head_sum) + mm(tlo[i], head_sum)) for i in n]
    s = [s_scr[d, bi, gi] for d, bi, gi in chains]
    os_ = [mm(rows((rt[i] + w2[i][:, :gw]).astype(BF16), gd[i].astype(BF16)), bd(s[i].astype(BF16))) for i in n]
    y = [os_[i][:c] + yp[i] for i in n]
    for i, (d, bi, gi) in enumerate(chains):
        s_scr[d, bi, gi] = pc[i] * s[i] + os_[i][c:] + hv[i]
    for d, y_ref in enumerate((yf_ref, yb_ref)):
        for bi in range(nb):
            y_ref[bi] = jnp.concatenate([y[chains.index((d, bi, gi))] for gi in range(ng)],
                                        axis=1).astype(y_ref.dtype)

    @pl.when(ci == pl.num_programs(1) - 1)
    def _():
        sout_ref[...] = s_scr[...]


def _wkv(lw, kd, bb, aa, r, v, s0):
    assert lw.shape[1] % WKV_ROWS == 0 and lw.shape[2] % WKV_CHUNK == 0
    _, b, l, width = lw.shape
    c = WKV_CHUNK
    nc = l // c
    nb = WKV_ROWS
    dirs = lambda d: pl.BlockSpec((1, nb, c, width), lambda i, j: (d, i, (nc - 1 - j) if d else j, 0))
    both = lambda d: pl.BlockSpec((nb, c, width), lambda i, j: (i, (nc - 1 - j) if d else j, 0))
    stspec = pl.BlockSpec((2, nb) + s0.shape[2:], lambda i, j: (0, i, 0, 0, 0))
    return pl.pallas_call(
        _wkv_kernel,
        grid=(b // nb, nc),
        in_specs=[dirs(0), dirs(0), dirs(0), both(0), both(0), both(0),
                  dirs(1), dirs(1), dirs(1), both(1), both(1), both(1), stspec],
        out_specs=[both(0), both(1), stspec],
        out_shape=[jax.ShapeDtypeStruct((b, l, width), BF16), jax.ShapeDtypeStruct((b, l, width), BF16),
                   jax.ShapeDtypeStruct(s0.shape, F32)],
        scratch_shapes=[pltpu.VMEM((2, nb) + s0.shape[2:], F32)],
        compiler_params=_params("arbitrary", "arbitrary"),
        name="wkv",
    )(lw, kd, bb, aa, r, v, lw, kd, bb, aa, r, v, s0)


def _s5_kernel(*refs, nb, reverse, toeplitz):
    if toeplitz:
        (u_ref, lag_ref, win_ref, wout_ref, a_ref, h0_ref, y_ref, hout_ref,
         x_scr, hh_scr, hre_scr, him_scr, win_scr, wout_scr, wt_scr) = refs
    else:
        (u_ref, yprev_ref, win_ref, wout_ref, a_ref, h0_ref, y_ref, hout_ref,
         x_scr, hh_scr, hre_scr, him_scr, win_scr, wout_scr) = refs
    i = pl.program_id(1)
    ns = hre_scr.shape[1]
    t = win_ref.shape[1]

    @pl.when(i == 0)
    def _():
        hre_scr[...] = h0_ref[0, 0]
        him_scr[...] = h0_ref[0, 1]
        lane_group = (lax.broadcasted_iota(jnp.int32, (S5_GROUP, 2 * ns), 1) % ns) // S5_STATE
        for s in range(t):
            for g in range(LANES // S5_GROUP):
                r0 = s * LANES + g * S5_GROUP
                own = lane_group == g
                win_scr[r0:r0 + S5_GROUP, :] = jnp.where(own, win_ref[0, s], jnp.zeros((), BF16))
                wout_scr[r0:r0 + S5_GROUP, :] = jnp.where(own, wout_ref[0, s], jnp.zeros((), BF16))
        if toeplitz:
            for sig in range(t):
                for tau in range(t):
                    wt_scr[sig * LANES:(sig + 1) * LANES, tau * LANES:(tau + 1) * LANES] = lag_ref[0, tau - sig + t - 1]

    u = u_ref[0].astype(BF16)
    x_scr[...] = jnp.dot(u, win_scr[...], preferred_element_type=F32)
    ar, ai = a_ref[0, 0:1, :], a_ref[0, 1:2, :]
    nch = x_scr.shape[0] // nb

    hr, hi = hre_scr[...], him_scr[...]
    for k in range(nch):
        c = nch - 1 - k if reverse else k
        rows = slice(c * nb, (c + 1) * nb)
        hh_scr[rows, 0:ns] = hr
        hh_scr[rows, ns:2 * ns] = hi
        hr, hi = (hr * ar - hi * ai + x_scr[rows, 0:ns], hr * ai + hi * ar + x_scr[rows, ns:2 * ns])
    hre_scr[...] = hr
    him_scr[...] = hi
    y = _dot_nt(hh_scr[...].astype(BF16), wout_scr[...])
    if toeplitz:
        y = y + jnp.dot(u, wt_scr[...], preferred_element_type=F32)
    else:
        y = y + yprev_ref[0].astype(F32)
    y_ref[0] = y.astype(y_ref.dtype)

    @pl.when(i == pl.num_programs(1) - 1)
    def _():
        hout_ref[0, 0] = hr
        hout_ref[0, 1] = hi


def _s5_scan(u, first, win, wout, acoef, h0, nb, reverse):
    nblk, rows, kw = u.shape
    ns = win.shape[3] // 2
    rt = min(rows, S5_ROWS)
    assert rows % rt == 0 and rt % nb == 0
    nt = rows // rt
    tile = pl.BlockSpec((1, rt, kw), lambda g, i: (g, (nt - 1 - i) if reverse else i, 0))
    per_blk = lambda a: pl.BlockSpec((1,) + a.shape[1:], lambda g, i: (g,) + (0,) * (a.ndim - 1))
    weights = [win, wout, acoef, h0]
    toeplitz = not reverse
    return pl.pallas_call(
        functools.partial(_s5_kernel, nb=nb, reverse=reverse, toeplitz=toeplitz),
        grid=(nblk, nt),
        in_specs=[tile, per_blk(first) if toeplitz else tile] + [per_blk(a) for a in weights],
        out_specs=[tile, per_blk(h0)],
        out_shape=[jax.ShapeDtypeStruct((nblk, rows, kw), BF16), jax.ShapeDtypeStruct(h0.shape, F32)],
        scratch_shapes=[pltpu.VMEM((rt, 2 * ns), F32), pltpu.VMEM((rt, 2 * ns), F32),
                        pltpu.VMEM((nb, ns), F32), pltpu.VMEM((nb, ns), F32),
                        pltpu.VMEM((kw, 2 * ns), BF16), pltpu.VMEM((kw, 2 * ns), BF16)]
        + ([pltpu.VMEM((kw, kw), BF16)] if toeplitz else []),
        compiler_params=_params("arbitrary", "arbitrary"),
        name="s5_bwd" if reverse else "s5_fwd",
    )(u, first, *weights)


def _s5_matrices(lam_re, lam_im, log_dt, b_re, b_im, c_re, c_im):
    hp = lax.Precision.HIGHEST
    t = S5_CHUNK
    dt = jnp.exp(log_dt.astype(F32))[..., None]
    lr, li = lam_re.astype(F32), lam_im.astype(F32)
    j = jnp.arange(t + 1, dtype=F32)[:, None, None, None]
    mag = jnp.exp(j * dt * lr)
    pr, pi = mag * jnp.cos(j * dt * li), mag * jnp.sin(j * dt * li)
    ar, ai = pr[1], pi[1]
    den = lr * lr + li * li
    fr = ((ar - 1.0) * lr + ai * li) / den
    fi = (ai * lr - (ar - 1.0) * li) / den
    bbr = fr[..., None] * b_re - fi[..., None] * b_im
    bbi = fr[..., None] * b_im + fi[..., None] * b_re
    car = c_re[None, None] * pr[:, :, :, None, :] - c_im[None, None] * pi[:, :, :, None, :]
    cai = c_re[None, None] * pi[:, :, :, None, :] + c_im[None, None] * pr[:, :, :, None, :]
    g = lr.shape[1]
    ca = jnp.concatenate([car, -cai], axis=-1).transpose(1, 2, 0, 3, 4).reshape(2, g, (t + 1) * S5_GROUP, 2 * S5_STATE)
    kj = jnp.einsum('dgxq,dgqe->dgxe', ca, jnp.concatenate([bbr, bbi], axis=2), precision=hp)
    kj = kj.reshape(2, g, t + 1, S5_GROUP, S5_GROUP).transpose(2, 0, 1, 3, 4)
    gpb = LANES // S5_GROUP
    nblk = g // gpb
    ns = gpb * S5_STATE
    own_chan = (jnp.arange(gpb)[:, None, None] == (jnp.arange(LANES) // S5_GROUP)[None, None, :])

    kl = jnp.concatenate([kj[t - 1:0:-1, 1], (kj[0, 0] + kj[0, 1])[None], kj[1:t, 0]], axis=0)
    kl = kl.reshape(2 * t - 1, nblk, gpb, S5_GROUP, S5_GROUP).transpose(0, 1, 4, 2, 3)
    kl = kl.reshape(2 * t - 1, nblk, 1, S5_GROUP, LANES)
    lagblk = jnp.where(own_chan, kl, 0.0).transpose(1, 0, 2, 3, 4).reshape(nblk, 2 * t - 1, LANES, LANES)

    def state_in(d, pw):
        prs = pr[pw, d].reshape(t, nblk, 1, ns).transpose(1, 0, 2, 3)
        pis = pi[pw, d].reshape(t, nblk, 1, ns).transpose(1, 0, 2, 3)
        lay = lambda q: q.reshape(nblk, gpb, S5_STATE, S5_GROUP).transpose(0, 3, 1, 2).reshape(nblk, 1, S5_GROUP, ns)
        br, bi = lay(bbr[d]), lay(bbi[d])
        return jnp.concatenate([prs * br - pis * bi, prs * bi + pis * br], axis=-1).astype(BF16)

    def state_out(d, pw):
        lay = lambda q: q.reshape(t, nblk, gpb, S5_GROUP, S5_STATE).transpose(1, 0, 3, 2, 4).reshape(
            nblk, t, S5_GROUP, ns)
        return jnp.concatenate([lay(car[pw, d]), lay(-cai[pw, d])], axis=-1).astype(BF16)

    coef = lambda d: jnp.stack([pr[t, d].reshape(nblk, ns), pi[t, d].reshape(nblk, ns)], axis=1)
    fwd = (lagblk.astype(BF16), state_in(0, t - 1 - jnp.arange(t)), state_out(0, jnp.arange(t) + 1), coef(0))
    bwd = (state_in(1, jnp.arange(t)), state_out(1, t - jnp.arange(t)), coef(1))
    return fwd, bwd


def _s5_rows(u5):
    nblk, nc, b, t, w = u5.shape
    return u5.reshape(nblk, nc * b, t * w)


def _s5_unrows(y, b):
    nblk, rows, kw = y.shape
    return y.reshape(nblk, rows // b, b, S5_CHUNK, kw // S5_CHUNK)


def _out_kernel(x_ref, gate_ref, yf_ref, yb_ref, bonus_ref, gr_ref, u_ref, gs_ref, ys_ref, lg_ref, lb_ref, d_ref,
                wg_ref, wo_ref, fg_ref, e_ref, o_ref):
    dr = bonus_ref.shape[2]
    y = yf_ref[0].astype(F32) + yb_ref[0].astype(F32)
    mean = _segsum(y, e_ref) * (1.0 / HEAD)
    dev = y - mean
    var = _segsum(dev * dev, e_ref) * (1.0 / HEAD)
    yn = dev * lax.rsqrt(var + LN_X_EPS) * lg_ref[...] + lb_ref[...]
    gr = gr_ref[0].astype(F32)
    y_r = (yn + bonus_ref[0].astype(F32)) * (gr * _sigmoid(gr))

    tm = u_ref.shape[1]
    ys = jnp.concatenate([ys_ref[j, :, 0].reshape(tm, LANES) for j in range(ys_ref.shape[0])], axis=1)
    s = ys.astype(F32) + d_ref[...] * u_ref[0].astype(F32)
    s = 0.5 * s * (1.0 + jnp.tanh(math.sqrt(2.0 / math.pi) * (s + 0.044715 * (s * s * s))))
    gl = jnp.dot(s.astype(BF16), wg_ref[...], preferred_element_type=F32)
    ds = gl.shape[1] // 2
    gs = gs_ref[0].astype(F32)
    y_s = gl[:, :ds] * _sigmoid(gl[:, ds:]) * (gs * _sigmoid(gs))

    out = (jnp.dot(y_r.astype(BF16), wo_ref[0:dr, :], preferred_element_type=F32)
           + jnp.dot(y_s.astype(BF16), wo_ref[dr:, :], preferred_element_type=F32))
    xo = x_ref[0] + gate_ref[0] * out
    ms = jnp.mean(xo * xo, axis=-1, keepdims=True)
    o_ref[0] = (xo * lax.rsqrt(ms + EPS) * fg_ref[...]).astype(o_ref.dtype)


def _out_stage(x, gate, yf, yb, bonus, z2, ys, lnx_g, lnx_b, s5_d, w_glu, w_out, final_g, e):
    b, l, d = x.shape
    dr = bonus.shape[2]
    ds = ys.shape[0] * LANES
    assert dr == ds
    tm = TOKENS_OUT
    tok = lambda w, cb: pl.BlockSpec((1, tm, w), lambda i, j: (i, j, cb))
    const = lambda *shape: pl.BlockSpec(shape, lambda i, j: (0,) * len(shape))
    return pl.pallas_call(
        _out_kernel,
        grid=(b, l // tm),
        in_specs=[tok(d, 0), pl.BlockSpec((1, 1, d), lambda i, j: (i, 0, 0)),
                  tok(dr, 0), tok(dr, 0), tok(dr, 0), tok(dr, 0), tok(ds, 1), tok(ds, 2),
                  pl.BlockSpec((ys.shape[0], tm // S5_CHUNK, 1, S5_CHUNK, LANES), lambda i, j: (0, j, i, 0, 0)),
                  const(1, dr), const(1, dr), const(1, ds), const(ds, 2 * ds), const(dr + ds, d),
                  const(1, d), const(dr, dr)],
        out_specs=tok(d, 0),
        out_shape=jax.ShapeDtypeStruct((b, l, d), x.dtype),
        compiler_params=_params("arbitrary", "arbitrary"),
        name="out_stage",
    )(x, gate, yf, yb, bonus, z2, z2, z2, ys, lnx_g.reshape(1, dr), lnx_b.reshape(1, dr), s5_d.reshape(1, ds),
      w_glu.astype(BF16), w_out.astype(BF16), final_g.reshape(1, d), e)


def kernel(x, c, ctx, c_ctx, norm_g, w_ada, b_ada, w_in, mu_shift, rwkv_w0, rwkv_w2, rwkv_a0, rwkv_a2, rwkv_k_k, rwkv_k_a, rwkv_r_k, lnx_g, lnx_b, s5_lam_re, s5_lam_im, s5_log_dt, s5_b_re, s5_b_im, s5_c_re, s5_c_im, s5_d, s5_w_glu, w_out, final_g):
    assert norm_g.shape[0] == 1, "one layer"
    b, l, d = x.shape
    lc = ctx.shape[1]
    dr = rwkv_k_k.shape[1]
    ds = s5_d.shape[1]
    sw = mu_shift.shape[1]
    assert sw == 3 * dr + 4 * LORA and b % 8 == 0 and lc % WKV_CHUNK == 0
    assert l % TOKENS_PREP == 0 and TOKENS_PREP % GRID_W == 0 and l % TOKENS_OUT == 0 and l % TOKENS_IN == 0

    cond = jnp.zeros((2 * b, d), F32).at[:b].set(c).at[b].set(c_ctx)
    m = _modulation(cond, w_ada[0], b_ada[0])
    shift, scale, gate = m[:b, :d], m[:b, d:2 * d], m[:b, 2 * d:]
    cshift = jnp.broadcast_to(m[b, :d], (b, d))
    cscale = jnp.broadcast_to(m[b, d:2 * d], (b, d))

    w1 = w_in[0][:, :sw].astype(BF16)
    w2 = w_in[0][:, sw:].astype(BF16)
    z1c, _, u5c = _inproj(ctx, norm_g[0], cscale[:, None], cshift[:, None], w1, w2)
    z1, z2, u5 = _inproj(x, norm_g[0], scale[:, None], shift[:, None], w1, w2)

    hid = jnp.arange(dr) // HEAD
    e = (hid[:, None] == hid[None, :]).astype(BF16)
    pp = (mu_shift, rwkv_w0[0], rwkv_w2[0], rwkv_a0[0], rwkv_a2[0], rwkv_k_k, rwkv_k_a,
          rwkv_r_k[0].reshape(1, dr), e)
    rc, vc, aac, wc, kdc, bbc, _ = _prep(z1c, pp, grid2d=False)
    rx, vx, aax, wx, kdx, bbx, bonus = _prep(z1, pp, grid2d=True)

    s0 = jnp.zeros((2, b, dr // (WKV_PACK * HEAD), HEAD, WKV_PACK * HEAD), F32)
    _, _, s_ctx = _wkv(wc, kdc, bbc, aac, rc, vc, s0)
    yf, yb, _ = _wkv(wx, kdx, bbx, aax, rx, vx, s_ctx)

    s5f, s5b = _s5_matrices(s5_lam_re[0], s5_lam_im[0], s5_log_dt[0], s5_b_re[0], s5_b_im[0],
                            s5_c_re[0], s5_c_im[0])
    uc, ux = _s5_rows(u5c), _s5_rows(u5)
    h0 = jnp.zeros((uc.shape[0], 2, b, s5f[3].shape[2]), F32)
    ysc, hcf = _s5_scan(uc, *s5f, h0, b, False)
    _, hcb = _s5_scan(uc, ysc, *s5b, h0, b, True)
    ysf, _ = _s5_scan(ux, *s5f, hcf, b, False)
    ysfb, _ = _s5_scan(ux, ysf, *s5b, hcb, b, True)
    ys = _s5_unrows(ysfb, b)

    return _out_stage(x, gate[:, None], yf, yb, bonus, z2, ys, lnx_g[0], lnx_b[0], s5_d[0], s5_w_glu[0],
                      w_out[0], final_g, e)
```

```python
import functools
import math

import jax
import jax.numpy as jnp
from jax import lax
from jax.experimental import pallas as pl
from jax.experimental.pallas import tpu as pltpu

F32 = jnp.float32
BF16 = jnp.bfloat16

GRID_W = 64
HEAD = 64
LORA = 64
S5_GROUP = 16
S5_STATE = 64
S5_CHUNK = 16
S5_ROWS = 1024
TOKENS_IN = 512
TOKENS_PREP = 256
TOKENS_OUT = 1024
WKV_CHUNK = 64
WKV_ROWS = 4
WKV_PACK = 2
EPS = 1e-6
LN_X_EPS = 64e-5
EXP_M05 = math.exp(-0.5)

LANES = 128
VMEM_LIMIT = 56 * 1024 * 1024


def _params(*sem):
    return pltpu.CompilerParams(dimension_semantics=sem, vmem_limit_bytes=VMEM_LIMIT)


def _sigmoid(x):
    return 1.0 / (1.0 + jnp.exp(-x))


def _dot_bf16(a, b):
    return jnp.dot(a.astype(BF16), b.astype(BF16), preferred_element_type=F32)


def _dot_f32(a, b):
    return jnp.dot(a, b, preferred_element_type=F32, precision=lax.Precision.HIGHEST)


def _segsum(x, e_ref):
    return jnp.dot(x.astype(BF16), e_ref[...], preferred_element_type=F32)


def _mod_kernel(c_ref, w_ref, b_ref, o_ref):
    c = c_ref[...]
    o_ref[...] = _dot_f32(c * _sigmoid(c), w_ref[...]) + b_ref[...]


def _modulation(cond, w_ada, b_ada):
    rows, d = cond.shape
    n = w_ada.shape[1]
    tn = 512
    return pl.pallas_call(
        _mod_kernel,
        grid=(n // tn,),
        in_specs=[pl.BlockSpec((rows, d), lambda j: (0, 0)),
                  pl.BlockSpec((d, tn), lambda j: (0, j)),
                  pl.BlockSpec((1, tn), lambda j: (0, j))],
        out_specs=pl.BlockSpec((rows, tn), lambda j: (0, j)),
        out_shape=jax.ShapeDtypeStruct((rows, n), F32),
        compiler_params=_params("arbitrary"),
        name="modulation",
    )(cond, w_ada, b_ada.reshape(1, n))


def _inproj_kernel(x_ref, g_ref, sc_ref, sh_ref, w1_ref, w2_ref, z1_ref, z2_ref, u_ref):
    x = x_ref[0]
    ms = jnp.mean(x * x, axis=-1, keepdims=True)
    h = x * lax.rsqrt(ms + EPS) * g_ref[...]
    h = (h * (1.0 + sc_ref[0]) + sh_ref[0]).astype(BF16)
    z1_ref[0] = jnp.dot(h, w1_ref[...], preferred_element_type=F32).astype(z1_ref.dtype)
    z2 = jnp.dot(h, w2_ref[...], preferred_element_type=F32)
    z2_ref[0] = z2.astype(z2_ref.dtype)
    nblk, nch = u_ref.shape[0], u_ref.shape[1]
    w = z2.shape[1] // 3
    for j in range(nblk):
        u_ref[j, :, 0] = z2[:, w + LANES * j:w + LANES * (j + 1)].reshape(nch, S5_CHUNK, LANES).astype(u_ref.dtype)


def _inproj(x, norm_g, scale, shift, w1, w2):
    b, l, d = x.shape
    n1, n2 = w1.shape[1], w2.shape[1]
    tm = min(l, TOKENS_IN)
    nblk = n2 // 3 // LANES
    return pl.pallas_call(
        _inproj_kernel,
        grid=(b, l // tm),
        in_specs=[pl.BlockSpec((1, tm, d), lambda i, j: (i, j, 0)),
                  pl.BlockSpec((1, d), lambda i, j: (0, 0)),
                  pl.BlockSpec((1, 1, d), lambda i, j: (i, 0, 0)),
                  pl.BlockSpec((1, 1, d), lambda i, j: (i, 0, 0)),
                  pl.BlockSpec((d, n1), lambda i, j: (0, 0)),
                  pl.BlockSpec((d, n2), lambda i, j: (0, 0))],
        out_specs=[pl.BlockSpec((1, tm, n1), lambda i, j: (i, j, 0)),
                   pl.BlockSpec((1, tm, n2), lambda i, j: (i, j, 0)),
                   pl.BlockSpec((nblk, tm // S5_CHUNK, 1, S5_CHUNK, LANES), lambda i, j: (0, j, i, 0, 0))],
        out_shape=[jax.ShapeDtypeStruct((b, l, n1), BF16),
                   jax.ShapeDtypeStruct((b, l, n2), BF16),
                   jax.ShapeDtypeStruct((nblk, l // S5_CHUNK, b, S5_CHUNK, LANES), BF16)],
        compiler_params=_params("arbitrary", "arbitrary"),
        name="inproj",
    )(x, norm_g.reshape(1, d), scale, shift, w1, w2)


def _prep_body(zb, up, down, mu_ref, w0_ref, w2_ref, a0_ref, a2_ref, kk_ref, ka_ref, rk_ref, e_ref, sh_ref, outs,
               grid2d):
    t, sw = zb.shape
    dr = (sw - 4 * LORA) // 3
    z = zb.astype(F32)
    slot = lax.broadcasted_iota(jnp.int32, (1, sw), 1) & 3
    pn = jnp.dot(sh_ref[...], zb, preferred_element_type=F32)
    prev, nxt = pn[:t], pn[t:]
    if grid2d:
        sh = jnp.where(slot == 0, prev, jnp.where(slot == 1, nxt, jnp.where(slot == 2, up, down)))
    else:
        sh = jnp.where((slot & 1) == 0, prev, nxt)
    zs = z + mu_ref[...] * (sh - z)

    r = zs[:, 0:dr]
    k = zs[:, dr:2 * dr]
    v = zs[:, 2 * dr:3 * dr]
    kk = k * kk_ref[...]
    ss = _segsum(kk * kk, e_ref)
    kk = kk * lax.rsqrt(jnp.maximum(ss, 1e-24))
    r_o, v_o, aa_o, w_o, kd_o, bb_o, bonus_o = outs
    r_o[0] = r.astype(r_o.dtype)
    v_o[0] = v.astype(v_o.dtype)
    aa_o[0] = (-kk).astype(aa_o.dtype)
    ksum = None
    for d in range(2):
        zw = zs[:, 3 * dr + LORA * d:3 * dr + LORA * (d + 1)]
        za = zs[:, 3 * dr + 2 * LORA + LORA * d:3 * dr + 2 * LORA + LORA * (d + 1)]
        wl = w0_ref[d:d + 1, :] + _dot_bf16(jnp.tanh(zw), w2_ref[d])
        w_o[d, 0] = -EXP_M05 * _sigmoid(wl)
        asig = _sigmoid(a0_ref[d:d + 1, :] + _dot_bf16(za, a2_ref[d]))
        kd = k * (1.0 + (asig - 1.0) * ka_ref[...])
        kd_o[d, 0] = kd.astype(kd_o.dtype)
        bb_o[d, 0] = (kk * asig).astype(bb_o.dtype)
        ksum = kd if ksum is None else ksum + kd
    bonus_o[0] = (_segsum(r * (0.5 * ksum) * rk_ref[...], e_ref) * v).astype(bonus_o.dtype)


def _prep2d_kernel(zc_ref, zu_ref, zd_ref, *rest):
    params, outs = rest[:10], rest[10:]
    j = pl.program_id(1)
    nj = pl.num_programs(1)
    zb = zc_ref[0]
    z = zb.astype(F32)
    t = z.shape[0]
    row = lax.broadcasted_iota(jnp.int32, (t, 1), 0)
    up = jnp.concatenate([zu_ref[0].astype(F32), z[:t - GRID_W]], axis=0)
    up = jnp.where(jnp.logical_and(j == 0, row < GRID_W), 0.0, up)
    down = jnp.concatenate([z[GRID_W:], zd_ref[0].astype(F32)], axis=0)
    down = jnp.where(jnp.logical_and(j == nj - 1, row >= t - GRID_W), 0.0, down)
    _prep_body(zb, up, down, *params, outs, grid2d=True)


def _prep1d_kernel(zc_ref, *rest):
    params, outs = rest[:10], rest[10:]
    _prep_body(zc_ref[0], None, None, *params, outs, grid2d=False)


def _prep(z1, pp, grid2d):
    b, l, sw = z1.shape
    dr = (sw - 4 * LORA) // 3
    tt = TOKENS_PREP if grid2d else l
    nj = l // tt
    rb = tt // GRID_W
    const = lambda *shape: pl.BlockSpec(shape, lambda i, j: (0,) * len(shape))
    p_specs = [const(1, sw), const(2, dr), const(2, LORA, dr), const(2, dr), const(2, LORA, dr),
               const(1, dr), const(1, dr), const(1, dr), const(dr, dr), const(2 * tt, tt)]
    edge = GRID_W if grid2d else tt
    tok = jnp.arange(tt)
    nbr = jnp.concatenate([jnp.where(tok % edge != 0, tok - 1, -1), jnp.where(tok % edge != edge - 1, tok + 1, -1)])
    shift = (nbr[:, None] == tok[None, :]).astype(BF16)
    cur = pl.BlockSpec((1, tt, sw), lambda i, j: (i, j, 0))
    if grid2d:
        nrow = l // GRID_W
        in_specs = [cur,
                    pl.BlockSpec((1, GRID_W, sw), lambda i, j: (i, jnp.maximum(j * rb - 1, 0), 0)),
                    pl.BlockSpec((1, GRID_W, sw), lambda i, j: (i, jnp.minimum((j + 1) * rb, nrow - 1), 0))]
        args = (z1, z1, z1)
        body = _prep2d_kernel
    else:
        in_specs = [cur]
        args = (z1,)
        body = _prep1d_kernel
    o1 = pl.BlockSpec((1, tt, dr), lambda i, j: (i, j, 0))
    o2 = pl.BlockSpec((2, 1, tt, dr), lambda i, j: (0, i, j, 0))
    s1 = jax.ShapeDtypeStruct((b, l, dr), BF16)
    s2 = jax.ShapeDtypeStruct((2, b, l, dr), BF16)
    s2w = jax.ShapeDtypeStruct((2, b, l, dr), F32)
    return pl.pallas_call(
        body,
        grid=(b, nj),
        in_specs=in_specs + p_specs,
        out_specs=[o1, o1, o1, o2, o2, o2, o1],
        out_shape=[s1, s1, s1, s2w, s2, s2, s1],
        compiler_params=_params("arbitrary", "arbitrary"),
        name="prep2d" if grid2d else "prep1d",
    )(*args, *pp, shift)


def _dot_nt(a, b):
    return lax.dot_general(a, b, (((1,), (1,)), ((), ())), preferred_element_type=F32)


def _dot_tn(a, b):
    return lax.dot_general(a, b, (((0,), (0,)), ((), ())), preferred_element_type=F32)


def _wkv_operands(lw_ref, kd_ref, bb_ref, aa_ref, r_ref, v_ref, bi, backward):
    c = aa_ref.shape[1]
    row = lax.broadcasted_iota(jnp.int32, (c, c), 0)
    col = lax.broadcasted_iota(jnp.int32, (c, c), 1)
    lw = lw_ref[0, bi]
    lw_hi = lw.astype(BF16)
    lw_lo = (lw - lw_hi.astype(F32)).astype(BF16)
    lc = (col >= row if backward else col <= row).astype(BF16)
    cum = jnp.dot(lc, lw_hi, preferred_element_type=F32) + jnp.dot(lc, lw_lo, preferred_element_type=F32)
    tot = cum[0:1, :] if backward else cum[c - 1:c, :]
    pinv = jnp.exp(-cum)
    pend = jnp.exp(tot - cum)
    bb, kd = bb_ref[0, bi].astype(F32), kd_ref[0, bi].astype(F32)
    rt = r_ref[bi].astype(F32) * jnp.exp(cum)
    return dict(
        tot=tot, at=(aa_ref[bi].astype(F32) * jnp.exp(cum - lw)).astype(BF16),
        rt=rt, rt_b=rt.astype(BF16),
        bt=(bb * pinv).astype(BF16), kt=(kd * pinv).astype(BF16),
        be=(bb * pend).astype(BF16), ke=(kd * pend).astype(BF16), v=v_ref[bi].astype(BF16))


def _wkv_kernel(lwf_ref, kdf_ref, bbf_ref, aaf_ref, rf_ref, vf_ref, lwb_ref, kdb_ref, bbb_ref, aab_ref, rb_ref,
                vb_ref, s0_ref, yf_ref, yb_ref, sout_ref, s_scr):
    ci = pl.program_id(1)
    nb, c, width = aaf_ref.shape
    gw = WKV_PACK * HEAD
    ng = width // gw
    assert c == HEAD and width % gw == 0

    @pl.when(ci == 0)
    def _():
        s_scr[...] = s0_ref[...]

    ops = {(0, bi): _wkv_operands(lwf_ref, kdf_ref, bbf_ref, aaf_ref, rf_ref, vf_ref, bi, False) for bi in range(nb)}
    ops.update({(1, bi): _wkv_operands(lwb_ref, kdb_ref, bbb_ref, aab_ref, rb_ref, vb_ref, bi, True)
                for bi in range(nb)})
    chains = [(d, bi, gi) for gi in range(ng) for bi in range(nb) for d in range(2)]
    n = range(len(chains))
    cut = lambda name: [ops[d, bi][name][:, gi * gw:(gi + 1) * gw] for d, bi, gi in chains]
    at, rt, rt_b, bt, kt, be, ke, v, tot = (cut(k) for k in ("at", "rt", "rt_b", "bt", "kt", "be", "ke", "v", "tot"))
    mm = lambda a, b: jnp.dot(a, b, preferred_element_type=F32)

    r64 = lax.broadcasted_iota(jnp.int32, (c, gw), 0)
    c64 = lax.broadcasted_iota(jnp.int32, (c, gw), 1) & (HEAD - 1)
    eye_f = (r64 == c64).astype(F32)
    causal = {0: (c64 < r64, c64 <= r64), 1: (c64 > r64, c64 >= r64)}
    strict = [causal[d][0] for d, _, _ in chains]
    incl = [causal[d][1] for d, _, _ in chains]
    same_head = (lax.broadcasted_iota(jnp.int32, (gw, gw), 0) // HEAD
                 == lax.broadcasted_iota(jnp.int32, (gw, gw), 1) // HEAD)

    def bd(xp):
        return jnp.where(same_head, jnp.concatenate([xp] * (gw // c), axis=0), jnp.zeros((), xp.dtype))

    def diag(o):
        head = lax.broadcasted_iota(jnp.int32, (c, gw), 1) // HEAD
        out = o[:c]
        for h in range(1, gw // c):
            out = jnp.where(head == h, o[h * c:(h + 1) * c], out)
        return out

    rows = lambda a, b: jnp.concatenate([a, b], axis=0)
    cols = lambda a, b: jnp.concatenate([a, b], axis=1)
    g = [_dot_nt(rows(at[i], rt_b[i]), rows(bd(bt[i]), bd(kt[i]))) for i in n]
    x = [jnp.where(strict[i], g[i][:c, :gw], 0.0) for i in n]
    a_ak = [jnp.where(strict[i], g[i][:c, gw:], 0.0).astype(BF16) for i in n]
    m_rb = [jnp.where(incl[i], g[i][c:, :gw], 0.0).astype(BF16) for i in n]
    m_rk = [jnp.where(incl[i], g[i][c:, gw:], 0.0).astype(BF16) for i in n]
    t = [eye_f + x[i] for i in n]
    p = [x[i].astype(BF16) for i in n]
    p = [mm(p[i], bd(p[i])).astype(BF16) for i in n]
    for _ in range((c - 1).bit_length() - 2):
        o = [mm(rows(p[i], t[i].astype(BF16)), bd(p[i])) for i in n]
        p = [o[i][:c].astype(BF16) for i in n]
        t = [t[i] + o[i][c:] for i in n]
    t = [t[i] + mm(t[i].astype(BF16), bd(p[i])) for i in n]
    ov = [mm(rows(a_ak[i], m_rk[i]), bd(v[i])) for i in n]
    za = [mm(t[i].astype(BF16), cols(bd(at[i]), bd(ov[i][:c].astype(BF16)))).astype(BF16) for i in n]
    w2 = [mm(m_rb[i], cols(bd(za[i][:, :gw]), bd(za[i][:, gw:]))) for i in n]
    yp = [w2[i][:, gw:] + ov[i][c:] for i in n]
    tn = [_dot_tn(rows(be[i], ke[i]), rows(za[i], cols(jnp.zeros((c, gw), BF16), v[i]))) for i in n]
    gd = [diag(tn[i][:, :gw]) for i in n]
    hv = [diag(tn[i][:, gw:]) for i in n]
    head_sum = same_head.astype(BF16)
    tdiag = [jnp.where(r64 == c64, tot[i], 0.0) for i in n]
    thi = [tdiag[i].astype(BF16) for i in n]
    tlo = [(tdiag[i] - thi[i].astype(F32)).astype(BF16) for i in n]
    pc = [jnp.exp(mm(thi[i], head_sum) + mm(tlo[i], head_sum)) for i in n]
    s = [s_scr[d, bi, gi] for d, bi, gi in chains]
    os_ = [mm(rows((rt[i] + w2[i][:, :gw]).astype(BF16), gd[i].astype(BF16)), bd(s[i].astype(BF16))) for i in n]
    y = [os_[i][:c] + yp[i] for i in n]
    for i, (d, bi, gi) in enumerate(chains):
        s_scr[d, bi, gi] = pc[i] * s[i] + os_[i][c:] + hv[i]
    for d, y_ref in enumerate((yf_ref, yb_ref)):
        for bi in range(nb):
            y_ref[bi] = jnp.concatenate([y[chains.index((d, bi, gi))] for gi in range(ng)],
                                        axis=1).astype(y_ref.dtype)

    @pl.when(ci == pl.num_programs(1) - 1)
    def _():
        sout_ref[...] = s_scr[...]


def _wkv(lw, kd, bb, aa, r, v, s0):
    assert lw.shape[1] % WKV_ROWS == 0 and lw.shape[2] % WKV_CHUNK == 0
    _, b, l, width = lw.shape
    c = WKV_CHUNK
    nc = l // c
    nb = WKV_ROWS
    dirs = lambda d: pl.BlockSpec((1, nb, c, width), lambda i, j: (d, i, (nc - 1 - j) if d else j, 0))
    both = lambda d: pl.BlockSpec((nb, c, width), lambda i, j: (i, (nc - 1 - j) if d else j, 0))
    stspec = pl.BlockSpec((2, nb) + s0.shape[2:], lambda i, j: (0, i, 0, 0, 0))
    return pl.pallas_call(
        _wkv_kernel,
        grid=(b // nb, nc),
        in_specs=[dirs(0), dirs(0), dirs(0), both(0), both(0), both(0),
                  dirs(1), dirs(1), dirs(1), both(1), both(1), both(1), stspec],
        out_specs=[both(0), both(1), stspec],
        out_shape=[jax.ShapeDtypeStruct((b, l, width), BF16), jax.ShapeDtypeStruct((b, l, width), BF16),
                   jax.ShapeDtypeStruct(s0.shape, F32)],
        scratch_shapes=[pltpu.VMEM((2, nb) + s0.shape[2:], F32)],
        compiler_params=_params("arbitrary", "arbitrary"),
        name="wkv",
    )(lw, kd, bb, aa, r, v, lw, kd, bb, aa, r, v, s0)


def _s5_kernel(*refs, nb, reverse, toeplitz):
    if toeplitz:
        (u_ref, lag_ref, win_ref, wout_ref, a_ref, h0_ref, y_ref, hout_ref,
         x_scr, hh_scr, hre_scr, him_scr, win_scr, wout_scr, wt_scr) = refs
    else:
        (u_ref, yprev_ref, win_ref, wout_ref, a_ref, h0_ref, y_ref, hout_ref,
         x_scr, hh_scr, hre_scr, him_scr, win_scr, wout_scr) = refs
    i = pl.program_id(1)
    ns = hre_scr.shape[1]
    t = win_ref.shape[1]

    @pl.when(i == 0)
    def _():
        hre_scr[...] = h0_ref[0, 0]
        him_scr[...] = h0_ref[0, 1]
        lane_group = (lax.broadcasted_iota(jnp.int32, (S5_GROUP, 2 * ns), 1) % ns) // S5_STATE
        for s in range(t):
            for g in range(LANES // S5_GROUP):
                r0 = s * LANES + g * S5_GROUP
                own = lane_group == g
                win_scr[r0:r0 + S5_GROUP, :] = jnp.where(own, win_ref[0, s], jnp.zeros((), BF16))
                wout_scr[r0:r0 + S5_GROUP, :] = jnp.where(own, wout_ref[0, s], jnp.zeros((), BF16))
        if toeplitz:
            for sig in range(t):
                for tau in range(t):
                    wt_scr[sig * LANES:(sig + 1) * LANES, tau * LANES:(tau + 1) * LANES] = lag_ref[0, tau - sig + t - 1]

    u = u_ref[0].astype(BF16)
    x_scr[...] = jnp.dot(u, win_scr[...], preferred_element_type=F32)
    ar, ai = a_ref[0, 0:1, :], a_ref[0, 1:2, :]
    nch = x_scr.shape[0] // nb

    hr, hi = hre_scr[...], him_scr[...]
    for k in range(nch):
        c = nch - 1 - k if reverse else k
        rows = slice(c * nb, (c + 1) * nb)
        hh_scr[rows, 0:ns] = hr
        hh_scr[rows, ns:2 * ns] = hi
        hr, hi = (hr * ar - hi * ai + x_scr[rows, 0:ns], hr * ai + hi * ar + x_scr[rows, ns:2 * ns])
    hre_scr[...] = hr
    him_scr[...] = hi
    y = _dot_nt(hh_scr[...].astype(BF16), wout_scr[...])
    if toeplitz:
        y = y + jnp.dot(u, wt_scr[...], preferred_element_type=F32)
    else:
        y = y + yprev_ref[0].astype(F32)
    y_ref[0] = y.astype(y_ref.dtype)

    @pl.when(i == pl.num_programs(1) - 1)
    def _():
        hout_ref[0, 0] = hr
        hout_ref[0, 1] = hi


def _s5_scan(u, first, win, wout, acoef, h0, nb, reverse):
    nblk, rows, kw = u.shape
    ns = win.shape[3] // 2
    rt = min(rows, S5_ROWS)
    assert rows % rt == 0 and rt % nb == 0
    nt = rows // rt
    tile = pl.BlockSpec((1, rt, kw), lambda g, i: (g, (nt - 1 - i) if reverse else i, 0))
    per_blk = lambda a: pl.BlockSpec((1,) + a.shape[1:], lambda g, i: (g,) + (0,) * (a.ndim - 1))
    weights = [win, wout, acoef, h0]
    toeplitz = not reverse
    return pl.pallas_call(
        functools.partial(_s5_kernel, nb=nb, reverse=reverse, toeplitz=toeplitz),
        grid=(nblk, nt),
        in_specs=[tile, per_blk(first) if toeplitz else tile] + [per_blk(a) for a in weights],
        out_specs=[tile, per_blk(h0)],
        out_shape=[jax.ShapeDtypeStruct((nblk, rows, kw), BF16), jax.ShapeDtypeStruct(h0.shape, F32)],
        scratch_shapes=[pltpu.VMEM((rt, 2 * ns), F32), pltpu.VMEM((rt, 2 * ns), F32),
                        pltpu.VMEM((nb, ns), F32), pltpu.VMEM((nb, ns), F32),
                        pltpu.VMEM((kw, 2 * ns), BF16), pltpu.VMEM((kw, 2 * ns), BF16)]
        + ([pltpu.VMEM((kw, kw), BF16)] if toeplitz else []),
        compiler_params=_params("arbitrary", "arbitrary"),
        name="s5_bwd" if reverse else "s5_fwd",
    )(u, first, *weights)


def _s5_matrices(lam_re, lam_im, log_dt, b_re, b_im, c_re, c_im):
    hp = lax.Precision.HIGHEST
    t = S5_CHUNK
    dt = jnp.exp(log_dt.astype(F32))[..., None]
    lr, li = lam_re.astype(F32), lam_im.astype(F32)
    j = jnp.arange(t + 1, dtype=F32)[:, None, None, None]
    mag = jnp.exp(j * dt * lr)
    pr, pi = mag * jnp.cos(j * dt * li), mag * jnp.sin(j * dt * li)
    ar, ai = pr[1], pi[1]
    den = lr * lr + li * li
    fr = ((ar - 1.0) * lr + ai * li) / den
    fi = (ai * lr - (ar - 1.0) * li) / den
    bbr = fr[..., None] * b_re - fi[..., None] * b_im
    bbi = fr[..., None] * b_im + fi[..., None] * b_re
    car = c_re[None, None] * pr[:, :, :, None, :] - c_im[None, None] * pi[:, :, :, None, :]
    cai = c_re[None, None] * pi[:, :, :, None, :] + c_im[None, None] * pr[:, :, :, None, :]
    g = lr.shape[1]
    ca = jnp.concatenate([car, -cai], axis=-1).transpose(1, 2, 0, 3, 4).reshape(2, g, (t + 1) * S5_GROUP, 2 * S5_STATE)
    kj = jnp.einsum('dgxq,dgqe->dgxe', ca, jnp.concatenate([bbr, bbi], axis=2), precision=hp)
    kj = kj.reshape(2, g, t + 1, S5_GROUP, S5_GROUP).transpose(2, 0, 1, 3, 4)
    gpb = LANES // S5_GROUP
    nblk = g // gpb
    ns = gpb * S5_STATE
    own_chan = (jnp.arange(gpb)[:, None, None] == (jnp.arange(LANES) // S5_GROUP)[None, None, :])

    kl = jnp.concatenate([kj[t - 1:0:-1, 1], (kj[0, 0] + kj[0, 1])[None], kj[1:t, 0]], axis=0)
    kl = kl.reshape(2 * t - 1, nblk, gpb, S5_GROUP, S5_GROUP).transpose(0, 1, 4, 2, 3)
    kl = kl.reshape(2 * t - 1, nblk, 1, S5_GROUP, LANES)
    lagblk = jnp.where(own_chan, kl, 0.0).transpose(1, 0, 2, 3, 4).reshape(nblk, 2 * t - 1, LANES, LANES)

    def state_in(d, pw):
        prs = pr[pw, d].reshape(t, nblk, 1, ns).transpose(1, 0, 2, 3)
        pis = pi[pw, d].reshape(t, nblk, 1, ns).transpose(1, 0, 2, 3)
        lay = lambda q: q.reshape(nblk, gpb, S5_STATE, S5_GROUP).transpose(0, 3, 1, 2).reshape(nblk, 1, S5_GROUP, ns)
        br, bi = lay(bbr[d]), lay(bbi[d])
        return jnp.concatenate([prs * br - pis * bi, prs * bi + pis * br], axis=-1).astype(BF16)

    def state_out(d, pw):
        lay = lambda q: q.reshape(t, nblk, gpb, S5_GROUP, S5_STATE).transpose(1, 0, 3, 2, 4).reshape(
            nblk, t, S5_GROUP, ns)
        return jnp.concatenate([lay(car[pw, d]), lay(-cai[pw, d])], axis=-1).astype(BF16)

    coef = lambda d: jnp.stack([pr[t, d].reshape(nblk, ns), pi[t, d].reshape(nblk, ns)], axis=1)
    fwd = (lagblk.astype(BF16), state_in(0, t - 1 - jnp.arange(t)), state_out(0, jnp.arange(t) + 1), coef(0))
    bwd = (state_in(1, jnp.arange(t)), state_out(1, t - jnp.arange(t)), coef(1))
    return fwd, bwd


def _s5_rows(u5):
    nblk, nc, b, t, w = u5.shape
    return u5.reshape(nblk, nc * b, t * w)


def _s5_unrows(y, b):
    nblk, rows, kw = y.shape
    return y.reshape(nblk, rows // b, b, S5_CHUNK, kw // S5_CHUNK)


def _out_kernel(x_ref, gate_ref, yf_ref, yb_ref, bonus_ref, gr_ref, u_ref, gs_ref, ys_ref, lg_ref, lb_ref, d_ref,
                wg_ref, wo_ref, fg_ref, e_ref, o_ref):
    dr = bonus_ref.shape[2]
    y = yf_ref[0].astype(F32) + yb_ref[0].astype(F32)
    mean = _segsum(y, e_ref) * (1.0 / HEAD)
    dev = y - mean
    var = _segsum(dev * dev, e_ref) * (1.0 / HEAD)
    yn = dev * lax.rsqrt(var + LN_X_EPS) * lg_ref[...] + lb_ref[...]
    gr = gr_ref[0].astype(F32)
    y_r = (yn + bonus_ref[0].astype(F32)) * (gr * _sigmoid(gr))

    tm = u_ref.shape[1]
    ys = jnp.concatenate([ys_ref[j, :, 0].reshape(tm, LANES) for j in range(ys_ref.shape[0])], axis=1)
    s = ys.astype(F32) + d_ref[...] * u_ref[0].astype(F32)
    s = 0.5 * s * (1.0 + jnp.tanh(math.sqrt(2.0 / math.pi) * (s + 0.044715 * (s * s * s))))
    gl = jnp.dot(s.astype(BF16), wg_ref[...], preferred_element_type=F32)
    ds = gl.shape[1] // 2
    gs = gs_ref[0].astype(F32)
    y_s = gl[:, :ds] * _sigmoid(gl[:, ds:]) * (gs * _sigmoid(gs))

    out = (jnp.dot(y_r.astype(BF16), wo_ref[0:dr, :], preferred_element_type=F32)
           + jnp.dot(y_s.astype(BF16), wo_ref[dr:, :], preferred_element_type=F32))
    xo = x_ref[0] + gate_ref[0] * out
    ms = jnp.mean(xo * xo, axis=-1, keepdims=True)
    o_ref[0] = (xo * lax.rsqrt(ms + EPS) * fg_ref[...]).astype(o_ref.dtype)


def _out_stage(x, gate, yf, yb, bonus, z2, ys, lnx_g, lnx_b, s5_d, w_glu, w_out, final_g, e):
    b, l, d = x.shape
    dr = bonus.shape[2]
    ds = ys.shape[0] * LANES
    assert dr == ds
    tm = TOKENS_OUT
    tok = lambda w, cb: pl.BlockSpec((1, tm, w), lambda i, j: (i, j, cb))
    const = lambda *shape: pl.BlockSpec(shape, lambda i, j: (0,) * len(shape))
    return pl.pallas_call(
        _out_kernel,
        grid=(b, l // tm),
        in_specs=[tok(d, 0), pl.BlockSpec((1, 1, d), lambda i, j: (i, 0, 0)),
                  tok(dr, 0), tok(dr, 0), tok(dr, 0), tok(dr, 0), tok(ds, 1), tok(ds, 2),
                  pl.BlockSpec((ys.shape[0], tm // S5_CHUNK, 1, S5_CHUNK, LANES), lambda i, j: (0, j, i, 0, 0)),
                  const(1, dr), const(1, dr), const(1, ds), const(ds, 2 * ds), const(dr + ds, d),
                  const(1, d), const(dr, dr)],
        out_specs=tok(d, 0),
        out_shape=jax.ShapeDtypeStruct((b, l, d), x.dtype),
        compiler_params=_params("arbitrary", "arbitrary"),
        name="out_stage",
    )(x, gate, yf, yb, bonus, z2, z2, z2, ys, lnx_g.reshape(1, dr), lnx_b.reshape(1, dr), s5_d.reshape(1, ds),
      w_glu.astype(BF16), w_out.astype(BF16), final_g.reshape(1, d), e)


def kernel(x, c, ctx, c_ctx, norm_g, w_ada, b_ada, w_in, mu_shift, rwkv_w0, rwkv_w2, rwkv_a0, rwkv_a2, rwkv_k_k, rwkv_k_a, rwkv_r_k, lnx_g, lnx_b, s5_lam_re, s5_lam_im, s5_log_dt, s5_b_re, s5_b_im, s5_c_re, s5_c_im, s5_d, s5_w_glu, w_out, final_g):
    assert norm_g.shape[0] == 1, "one layer"
    b, l, d = x.shape
    lc = ctx.shape[1]
    dr = rwkv_k_k.shape[1]
    ds = s5_d.shape[1]
    sw = mu_shift.shape[1]
    assert sw == 3 * dr + 4 * LORA and b % 8 == 0 and lc % WKV_CHUNK == 0
    assert l % TOKENS_PREP == 0 and TOKENS_PREP % GRID_W == 0 and l % TOKENS_OUT == 0 and l % TOKENS_IN == 0

    cond = jnp.zeros((2 * b, d), F32).at[:b].set(c).at[b].set(c_ctx)
    m = _modulation(cond, w_ada[0], b_ada[0])
    shift, scale, gate = m[:b, :d], m[:b, d:2 * d], m[:b, 2 * d:]
    cshift = jnp.broadcast_to(m[b, :d], (b, d))
    cscale = jnp.broadcast_to(m[b, d:2 * d], (b, d))

    w1 = w_in[0][:, :sw].astype(BF16)
    w2 = w_in[0][:, sw:].astype(BF16)
    z1c, _, u5c = _inproj(ctx, norm_g[0], cscale[:, None], cshift[:, None], w1, w2)
    z1, z2, u5 = _inproj(x, norm_g[0], scale[:, None], shift[:, None], w1, w2)

    hid = jnp.arange(dr) // HEAD
    e = (hid[:, None] == hid[None, :]).astype(BF16)
    pp = (mu_shift, rwkv_w0[0], rwkv_w2[0], rwkv_a0[0], rwkv_a2[0], rwkv_k_k, rwkv_k_a,
          rwkv_r_k[0].reshape(1, dr), e)
    rc, vc, aac, wc, kdc, bbc, _ = _prep(z1c, pp, grid2d=False)
    rx, vx, aax, wx, kdx, bbx, bonus = _prep(z1, pp, grid2d=True)

    s0 = jnp.zeros((2, b, dr // (WKV_PACK * HEAD), HEAD, WKV_PACK * HEAD), F32)
    _, _, s_ctx = _wkv(wc, kdc, bbc, aac, rc, vc, s0)
    yf, yb, _ = _wkv(wx, kdx, bbx, aax, rx, vx, s_ctx)

    s5f, s5b = _s5_matrices(s5_lam_re[0], s5_lam_im[0], s5_log_dt[0], s5_b_re[0], s5_b_im[0],
                            s5_c_re[0], s5_c_im[0])
    uc, ux = _s5_rows(u5c), _s5_rows(u5)
    h0 = jnp.zeros((uc.shape[0], 2, b, s5f[3].shape[2]), F32)
    ysc, hcf = _s5_scan(uc, *s5f, h0, b, False)
    _, hcb = _s5_scan(uc, ysc, *s5b, h0, b, True)
    ysf, _ = _s5_scan(ux, *s5f, hcf, b, False)
    ysfb, _ = _s5_scan(ux, ysf, *s5b, hcb, b, True)
    ys = _s5_unrows(ysfb, b)

    return _out_stage(x, gate[:, None], yf, yb, bonus, z2, ys, lnx_g[0], lnx_b[0], s5_d[0], s5_w_glu[0],
                      w_out[0], final_g, e)
```

```python
import functools
import math

import jax
import jax.numpy as jnp
from jax import lax
from jax.experimental import pallas as pl
from jax.experimental.pallas import tpu as pltpu

F32 = jnp.float32
BF16 = jnp.bfloat16

GRID_W = 64
HEAD = 64
LORA = 64
S5_GROUP = 16
S5_STATE = 64
S5_CHUNK = 16
S5_ROWS = 1024
TOKENS_IN = 512
TOKENS_PREP = 256
TOKENS_OUT = 1024
WKV_CHUNK = 64
WKV_ROWS = 8
WKV_PACK = 2
EPS = 1e-6
LN_X_EPS = 64e-5
EXP_M05 = math.exp(-0.5)

LANES = 128
VMEM_LIMIT = 56 * 1024 * 1024


def _params(*sem):
    return pltpu.CompilerParams(dimension_semantics=sem, vmem_limit_bytes=VMEM_LIMIT)


def _sigmoid(x):
    return 1.0 / (1.0 + jnp.exp(-x))


def _dot_bf16(a, b):
    return jnp.dot(a.astype(BF16), b.astype(BF16), preferred_element_type=F32)


def _dot_f32(a, b):
    return jnp.dot(a, b, preferred_element_type=F32, precision=lax.Precision.HIGHEST)


def _segsum(x, e_ref):
    return jnp.dot(x.astype(BF16), e_ref[...], preferred_element_type=F32)


def _mod_kernel(c_ref, w_ref, b_ref, o_ref):
    c = c_ref[...]
    o_ref[...] = _dot_f32(c * _sigmoid(c), w_ref[...]) + b_ref[...]


def _modulation(cond, w_ada, b_ada):
    rows, d = cond.shape
    n = w_ada.shape[1]
    tn = 512
    return pl.pallas_call(
        _mod_kernel,
        grid=(n // tn,),
        in_specs=[pl.BlockSpec((rows, d), lambda j: (0, 0)),
                  pl.BlockSpec((d, tn), lambda j: (0, j)),
                  pl.BlockSpec((1, tn), lambda j: (0, j))],
        out_specs=pl.BlockSpec((rows, tn), lambda j: (0, j)),
        out_shape=jax.ShapeDtypeStruct((rows, n), F32),
        compiler_params=_params("arbitrary"),
        name="modulation",
    )(cond, w_ada, b_ada.reshape(1, n))


def _inproj_kernel(x_ref, g_ref, sc_ref, sh_ref, w1_ref, w2_ref, z1_ref, z2_ref, u_ref):
    x = x_ref[0]
    ms = jnp.mean(x * x, axis=-1, keepdims=True)
    h = x * lax.rsqrt(ms + EPS) * g_ref[...]
    h = (h * (1.0 + sc_ref[0]) + sh_ref[0]).astype(BF16)
    z1_ref[0] = jnp.dot(h, w1_ref[...], preferred_element_type=F32).astype(z1_ref.dtype)
    z2 = jnp.dot(h, w2_ref[...], preferred_element_type=F32)
    z2_ref[0] = z2.astype(z2_ref.dtype)
    nblk, nch = u_ref.shape[0], u_ref.shape[1]
    w = z2.shape[1] // 3
    for j in range(nblk):
        u_ref[j, :, 0] = z2[:, w + LANES * j:w + LANES * (j + 1)].reshape(nch, S5_CHUNK, LANES).astype(u_ref.dtype)


def _inproj(x, norm_g, scale, shift, w1, w2):
    b, l, d = x.shape
    n1, n2 = w1.shape[1], w2.shape[1]
    tm = min(l, TOKENS_IN)
    nblk = n2 // 3 // LANES
    return pl.pallas_call(
        _inproj_kernel,
        grid=(b, l // tm),
        in_specs=[pl.BlockSpec((1, tm, d), lambda i, j: (i, j, 0)),
                  pl.BlockSpec((1, d), lambda i, j: (0, 0)),
                  pl.BlockSpec((1, 1, d), lambda i, j: (i, 0, 0)),
                  pl.BlockSpec((1, 1, d), lambda i, j: (i, 0, 0)),
                  pl.BlockSpec((d, n1), lambda i, j: (0, 0)),
                  pl.BlockSpec((d, n2), lambda i, j: (0, 0))],
        out_specs=[pl.BlockSpec((1, tm, n1), lambda i, j: (i, j, 0)),
                   pl.BlockSpec((1, tm, n2), lambda i, j: (i, j, 0)),
                   pl.BlockSpec((nblk, tm // S5_CHUNK, 1, S5_CHUNK, LANES), lambda i, j: (0, j, i, 0, 0))],
        out_shape=[jax.ShapeDtypeStruct((b, l, n1), BF16),
                   jax.ShapeDtypeStruct((b, l, n2), BF16),
                   jax.ShapeDtypeStruct((nblk, l // S5_CHUNK, b, S5_CHUNK, LANES), BF16)],
        compiler_params=_params("arbitrary", "arbitrary"),
        name="inproj",
    )(x, norm_g.reshape(1, d), scale, shift, w1, w2)


def _prep_body(zb, up, down, mu_ref, w0_ref, w2_ref, a0_ref, a2_ref, kk_ref, ka_ref, rk_ref, e_ref, sh_ref, outs,
               grid2d):
    t, sw = zb.shape
    dr = (sw - 4 * LORA) // 3
    z = zb.astype(F32)
    slot = lax.broadcasted_iota(jnp.int32, (1, sw), 1) & 3
    pn = jnp.dot(sh_ref[...], zb, preferred_element_type=F32)
    prev, nxt = pn[:t], pn[t:]
    if grid2d:
        sh = jnp.where(slot == 0, prev, jnp.where(slot == 1, nxt, jnp.where(slot == 2, up, down)))
    else:
        sh = jnp.where((slot & 1) == 0, prev, nxt)
    zs = z + mu_ref[...] * (sh - z)

    r = zs[:, 0:dr]
    k = zs[:, dr:2 * dr]
    v = zs[:, 2 * dr:3 * dr]
    kk = k * kk_ref[...]
    ss = _segsum(kk * kk, e_ref)
    kk = kk * lax.rsqrt(jnp.maximum(ss, 1e-24))
    r_o, v_o, aa_o, w_o, kd_o, bb_o, bonus_o = outs
    r_o[0] = r.astype(r_o.dtype)
    v_o[0] = v.astype(v_o.dtype)
    aa_o[0] = (-kk).astype(aa_o.dtype)
    ksum = None
    for d in range(2):
        zw = zs[:, 3 * dr + LORA * d:3 * dr + LORA * (d + 1)]
        za = zs[:, 3 * dr + 2 * LORA + LORA * d:3 * dr + 2 * LORA + LORA * (d + 1)]
        wl = w0_ref[d:d + 1, :] + _dot_bf16(jnp.tanh(zw), w2_ref[d])
        w_o[d, 0] = -EXP_M05 * _sigmoid(wl)
        asig = _sigmoid(a0_ref[d:d + 1, :] + _dot_bf16(za, a2_ref[d]))
        kd = k * (1.0 + (asig - 1.0) * ka_ref[...])
        kd_o[d, 0] = kd.astype(kd_o.dtype)
        bb_o[d, 0] = (kk * asig).astype(bb_o.dtype)
        ksum = kd if ksum is None else ksum + kd
    bonus_o[0] = (_segsum(r * (0.5 * ksum) * rk_ref[...], e_ref) * v).astype(bonus_o.dtype)


def _prep2d_kernel(zc_ref, zu_ref, zd_ref, *rest):
    params, outs = rest[:10], rest[10:]
    j = pl.program_id(1)
    nj = pl.num_programs(1)
    zb = zc_ref[0]
    z = zb.astype(F32)
    t = z.shape[0]
    row = lax.broadcasted_iota(jnp.int32, (t, 1), 0)
    up = jnp.concatenate([zu_ref[0].astype(F32), z[:t - GRID_W]], axis=0)
    up = jnp.where(jnp.logical_and(j == 0, row < GRID_W), 0.0, up)
    down = jnp.concatenate([z[GRID_W:], zd_ref[0].astype(F32)], axis=0)
    down = jnp.where(jnp.logical_and(j == nj - 1, row >= t - GRID_W), 0.0, down)
    _prep_body(zb, up, down, *params, outs, grid2d=True)


def _prep1d_kernel(zc_ref, *rest):
    params, outs = rest[:10], rest[10:]
    _prep_body(zc_ref[0], None, None, *params, outs, grid2d=False)


def _prep(z1, pp, grid2d):
    b, l, sw = z1.shape
    dr = (sw - 4 * LORA) // 3
    tt = TOKENS_PREP if grid2d else l
    nj = l // tt
    rb = tt // GRID_W
    const = lambda *shape: pl.BlockSpec(shape, lambda i, j: (0,) * len(shape))
    p_specs = [const(1, sw), const(2, dr), const(2, LORA, dr), const(2, dr), const(2, LORA, dr),
               const(1, dr), const(1, dr), const(1, dr), const(dr, dr), const(2 * tt, tt)]
    edge = GRID_W if grid2d else tt
    tok = jnp.arange(tt)
    nbr = jnp.concatenate([jnp.where(tok % edge != 0, tok - 1, -1), jnp.where(tok % edge != edge - 1, tok + 1, -1)])
    shift = (nbr[:, None] == tok[None, :]).astype(BF16)
    cur = pl.BlockSpec((1, tt, sw), lambda i, j: (i, j, 0))
    if grid2d:
        nrow = l // GRID_W
        in_specs = [cur,
                    pl.BlockSpec((1, GRID_W, sw), lambda i, j: (i, jnp.maximum(j * rb - 1, 0), 0)),
                    pl.BlockSpec((1, GRID_W, sw), lambda i, j: (i, jnp.minimum((j + 1) * rb, nrow - 1), 0))]
        args = (z1, z1, z1)
        body = _prep2d_kernel
    else:
        in_specs = [cur]
        args = (z1,)
        body = _prep1d_kernel
    o1 = pl.BlockSpec((1, tt, dr), lambda i, j: (i, j, 0))
    o2 = pl.BlockSpec((2, 1, tt, dr), lambda i, j: (0, i, j, 0))
    s1 = jax.ShapeDtypeStruct((b, l, dr), BF16)
    s2 = jax.ShapeDtypeStruct((2, b, l, dr), BF16)
    s2w = jax.ShapeDtypeStruct((2, b, l, dr), F32)
    return pl.pallas_call(
        body,
        grid=(b, nj),
        in_specs=in_specs + p_specs,
        out_specs=[o1, o1, o1, o2, o2, o2, o1],
        out_shape=[s1, s1, s1, s2w, s2, s2, s1],
        compiler_params=_params("arbitrary", "arbitrary"),
        name="prep2d" if grid2d else "prep1d",
    )(*args, *pp, shift)


def _dot_nt(a, b):
    return lax.dot_general(a, b, (((1,), (1,)), ((), ())), preferred_element_type=F32)


def _dot_tn(a, b):
    return lax.dot_general(a, b, (((0,), (0,)), ((), ())), preferred_element_type=F32)


def _wkv_operands(lw_ref, kd_ref, bb_ref, aa_ref, r_ref, v_ref, bi, backward):
    c = aa_ref.shape[1]
    row = lax.broadcasted_iota(jnp.int32, (c, c), 0)
    col = lax.broadcasted_iota(jnp.int32, (c, c), 1)
    lw = lw_ref[0, bi]
    lw_hi = lw.astype(BF16)
    lw_lo = (lw - lw_hi.astype(F32)).astype(BF16)
    lc = (col >= row if backward else col <= row).astype(BF16)
    cum = jnp.dot(lc, lw_hi, preferred_element_type=F32) + jnp.dot(lc, lw_lo, preferred_element_type=F32)
    tot = cum[0:1, :] if backward else cum[c - 1:c, :]
    pinv = jnp.exp(-cum)
    pend = jnp.exp(tot - cum)
    bb, kd = bb_ref[0, bi].astype(F32), kd_ref[0, bi].astype(F32)
    rt = r_ref[bi].astype(F32) * jnp.exp(cum)
    return dict(
        tot=tot, at=(aa_ref[bi].astype(F32) * jnp.exp(cum - lw)).astype(BF16),
        rt=rt, rt_b=rt.astype(BF16),
        bt=(bb * pinv).astype(BF16), kt=(kd * pinv).astype(BF16),
        be=(bb * pend).astype(BF16), ke=(kd * pend).astype(BF16), v=v_ref[bi].astype(BF16))


def _wkv_kernel(lwf_ref, kdf_ref, bbf_ref, aaf_ref, rf_ref, vf_ref, lwb_ref, kdb_ref, bbb_ref, aab_ref, rb_ref,
                vb_ref, s0_ref, yf_ref, yb_ref, sout_ref, s_scr):
    ci = pl.program_id(1)
    nb, c, width = aaf_ref.shape
    gw = WKV_PACK * HEAD
    ng = width // gw
    assert c == HEAD and width % gw == 0

    @pl.when(ci == 0)
    def _():
        s_scr[...] = s0_ref[...]

    ops = {(0, bi): _wkv_operands(lwf_ref, kdf_ref, bbf_ref, aaf_ref, rf_ref, vf_ref, bi, False) for bi in range(nb)}
    ops.update({(1, bi): _wkv_operands(lwb_ref, kdb_ref, bbb_ref, aab_ref, rb_ref, vb_ref, bi, True)
                for bi in range(nb)})
    chains = [(d, bi, gi) for gi in range(ng) for bi in range(nb) for d in range(2)]
    n = range(len(chains))
    cut = lambda name: [ops[d, bi][name][:, gi * gw:(gi + 1) * gw] for d, bi, gi in chains]
    at, rt, rt_b, bt, kt, be, ke, v, tot = (cut(k) for k in ("at", "rt", "rt_b", "bt", "kt", "be", "ke", "v", "tot"))
    mm = lambda a, b: jnp.dot(a, b, preferred_element_type=F32)

    r64 = lax.broadcasted_iota(jnp.int32, (c, gw), 0)
    c64 = lax.broadcasted_iota(jnp.int32, (c, gw), 1) & (HEAD - 1)
    eye_f = (r64 == c64).astype(F32)
    causal = {0: (c64 < r64, c64 <= r64), 1: (c64 > r64, c64 >= r64)}
    strict = [causal[d][0] for d, _, _ in chains]
    incl = [causal[d][1] for d, _, _ in chains]
    same_head = (lax.broadcasted_iota(jnp.int32, (gw, gw), 0) // HEAD
                 == lax.broadcasted_iota(jnp.int32, (gw, gw), 1) // HEAD)

    def bd(xp):
        return jnp.where(same_head, jnp.concatenate([xp] * (gw // c), axis=0), jnp.zeros((), xp.dtype))

    def diag(o):
        head = lax.broadcasted_iota(jnp.int32, (c, gw), 1) // HEAD
        out = o[:c]
        for h in range(1, gw // c):
            out = jnp.where(head == h, o[h * c:(h + 1) * c], out)
        return out

    rows = lambda a, b: jnp.concatenate([a, b], axis=0)
    cols = lambda a, b: jnp.concatenate([a, b], axis=1)
    g = [_dot_nt(rows(at[i], rt_b[i]), rows(bd(bt[i]), bd(kt[i]))) for i in n]
    x = [jnp.where(strict[i], g[i][:c, :gw], 0.0) for i in n]
    a_ak = [jnp.where(strict[i], g[i][:c, gw:], 0.0).astype(BF16) for i in n]
    m_rb = [jnp.where(incl[i], g[i][c:, :gw], 0.0).astype(BF16) for i in n]
    m_rk = [jnp.where(incl[i], g[i][c:, gw:], 0.0).astype(BF16) for i in n]
    t = [eye_f + x[i] for i in n]
    p = [x[i].astype(BF16) for i in n]
    p = [mm(p[i], bd(p[i])).astype(BF16) for i in n]
    for _ in range((c - 1).bit_length() - 2):
        o = [mm(rows(p[i], t[i].astype(BF16)), bd(p[i])) for i in n]
        p = [o[i][:c].astype(BF16) for i in n]
        t = [t[i] + o[i][c:] for i in n]
    t = [t[i] + mm(t[i].astype(BF16), bd(p[i])) for i in n]
    ov = [mm(rows(a_ak[i], m_rk[i]), bd(v[i])) for i in n]
    za = [mm(t[i].astype(BF16), cols(bd(at[i]), bd(ov[i][:c].astype(BF16)))).astype(BF16) for i in n]
    w2 = [mm(m_rb[i], cols(bd(za[i][:, :gw]), bd(za[i][:, gw:]))) for i in n]
    yp = [w2[i][:, gw:] + ov[i][c:] for i in n]
    tn = [_dot_tn(rows(be[i], ke[i]), rows(za[i], cols(jnp.zeros((c, gw), BF16), v[i]))) for i in n]
    gd = [diag(tn[i][:, :gw]) for i in n]
    hv = [diag(tn[i][:, gw:]) for i in n]
    head_sum = same_head.astype(BF16)
    tdiag = [jnp.where(r64 == c64, tot[i], 0.0) for i in n]
    thi = [tdiag[i].astype(BF16) for i in n]
    tlo = [(tdiag[i] - thi[i].astype(F32)).astype(BF16) for i in n]
    pc = [jnp.exp(mm(thi[i], head_sum) + mm(tlo[i], head_sum)) for i in n]
    s = [s_scr[d, bi, gi] for d, bi, gi in chains]
    os_ = [mm(rows((rt[i] + w2[i][:, :gw]).astype(BF16), gd[i].astype(BF16)), bd(s[i].astype(BF16))) for i in n]
    y = [os_[i][:c] + yp[i] for i in n]
    for i, (d, bi, gi) in enumerate(chains):
        s_scr[d, bi, gi] = pc[i] * s[i] + os_[i][c:] + hv[i]
    for d, y_ref in enumerate((yf_ref, yb_ref)):
        for bi in range(nb):
            y_ref[bi] = jnp.concatenate([y[chains.index((d, bi, gi))] for gi in range(ng)],
                                        axis=1).astype(y_ref.dtype)

    @pl.when(ci == pl.num_programs(1) - 1)
    def _():
        sout_ref[...] = s_scr[...]


def _wkv(lw, kd, bb, aa, r, v, s0):
    assert lw.shape[1] % WKV_ROWS == 0 and lw.shape[2] % WKV_CHUNK == 0
    _, b, l, width = lw.shape
    c = WKV_CHUNK
    nc = l // c
    nb = WKV_ROWS
    dirs = lambda d: pl.BlockSpec((1, nb, c, width), lambda i, j: (d, i, (nc - 1 - j) if d else j, 0))
    both = lambda d: pl.BlockSpec((nb, c, width), lambda i, j: (i, (nc - 1 - j) if d else j, 0))
    stspec = pl.BlockSpec((2, nb) + s0.shape[2:], lambda i, j: (0, i, 0, 0, 0))
    return pl.pallas_call(
        _wkv_kernel,
        grid=(b // nb, nc),
        in_specs=[dirs(0), dirs(0), dirs(0), both(0), both(0), both(0),
                  dirs(1), dirs(1), dirs(1), both(1), both(1), both(1), stspec],
        out_specs=[both(0), both(1), stspec],
        out_shape=[jax.ShapeDtypeStruct((b, l, width), BF16), jax.ShapeDtypeStruct((b, l, width), BF16),
                   jax.ShapeDtypeStruct(s0.shape, F32)],
        scratch_shapes=[pltpu.VMEM((2, nb) + s0.shape[2:], F32)],
        compiler_params=_params("arbitrary", "arbitrary"),
        name="wkv",
    )(lw, kd, bb, aa, r, v, lw, kd, bb, aa, r, v, s0)


def _s5_kernel(*refs, nb, reverse, toeplitz):
    if toeplitz:
        (u_ref, lag_ref, win_ref, wout_ref, a_ref, h0_ref, y_ref, hout_ref,
         x_scr, hh_scr, hre_scr, him_scr, win_scr, wout_scr, wt_scr) = refs
    else:
        (u_ref, yprev_ref, win_ref, wout_ref, a_ref, h0_ref, y_ref, hout_ref,
         x_scr, hh_scr, hre_scr, him_scr, win_scr, wout_scr) = refs
    i = pl.program_id(1)
    ns = hre_scr.shape[1]
    t = win_ref.shape[1]

    @pl.when(i == 0)
    def _():
        hre_scr[...] = h0_ref[0, 0]
        him_scr[...] = h0_ref[0, 1]
        lane_group = (lax.broadcasted_iota(jnp.int32, (S5_GROUP, 2 * ns), 1) % ns) // S5_STATE
        for s in range(t):
            for g in range(LANES // S5_GROUP):
                r0 = s * LANES + g * S5_GROUP
                own = lane_group == g
                win_scr[r0:r0 + S5_GROUP, :] = jnp.where(own, win_ref[0, s], jnp.zeros((), BF16))
                wout_scr[r0:r0 + S5_GROUP, :] = jnp.where(own, wout_ref[0, s], jnp.zeros((), BF16))
        if toeplitz:
            for sig in range(t):
                for tau in range(t):
                    wt_scr[sig * LANES:(sig + 1) * LANES, tau * LANES:(tau + 1) * LANES] = lag_ref[0, tau - sig + t - 1]

    u = u_ref[0].astype(BF16)
    x_scr[...] = jnp.dot(u, win_scr[...], preferred_element_type=F32)
    ar, ai = a_ref[0, 0:1, :], a_ref[0, 1:2, :]
    nch = x_scr.shape[0] // nb

    hr, hi = hre_scr[...], him_scr[...]
    for k in range(nch):
        c = nch - 1 - k if reverse else k
        rows = slice(c * nb, (c + 1) * nb)
        hh_scr[rows, 0:ns] = hr
        hh_scr[rows, ns:2 * ns] = hi
        hr, hi = (hr * ar - hi * ai + x_scr[rows, 0:ns], hr * ai + hi * ar + x_scr[rows, ns:2 * ns])
    hre_scr[...] = hr
    him_scr[...] = hi
    y = _dot_nt(hh_scr[...].astype(BF16), wout_scr[...])
    if toeplitz:
        y = y + jnp.dot(u, wt_scr[...], preferred_element_type=F32)
    else:
        y = y + yprev_ref[0].astype(F32)
    y_ref[0] = y.astype(y_ref.dtype)

    @pl.when(i == pl.num_programs(1) - 1)
    def _():
        hout_ref[0, 0] = hr
        hout_ref[0, 1] = hi


def _s5_scan(u, first, win, wout, acoef, h0, nb, reverse):
    nblk, rows, kw = u.shape
    ns = win.shape[3] // 2
    rt = min(rows, S5_ROWS)
    assert rows % rt == 0 and rt % nb == 0
    nt = rows // rt
    tile = pl.BlockSpec((1, rt, kw), lambda g, i: (g, (nt - 1 - i) if reverse else i, 0))
    per_blk = lambda a: pl.BlockSpec((1,) + a.shape[1:], lambda g, i: (g,) + (0,) * (a.ndim - 1))
    weights = [win, wout, acoef, h0]
    toeplitz = not reverse
    return pl.pallas_call(
        functools.partial(_s5_kernel, nb=nb, reverse=reverse, toeplitz=toeplitz),
        grid=(nblk, nt),
        in_specs=[tile, per_blk(first) if toeplitz else tile] + [per_blk(a) for a in weights],
        out_specs=[tile, per_blk(h0)],
        out_shape=[jax.ShapeDtypeStruct((nblk, rows, kw), BF16), jax.ShapeDtypeStruct(h0.shape, F32)],
        scratch_shapes=[pltpu.VMEM((rt, 2 * ns), F32), pltpu.VMEM((rt, 2 * ns), F32),
                        pltpu.VMEM((nb, ns), F32), pltpu.VMEM((nb, ns), F32),
                        pltpu.VMEM((kw, 2 * ns), BF16), pltpu.VMEM((kw, 2 * ns), BF16)]
        + ([pltpu.VMEM((kw, kw), BF16)] if toeplitz else []),
        compiler_params=_params("arbitrary", "arbitrary"),
        name="s5_bwd" if reverse else "s5_fwd",
    )(u, first, *weights)


def _s5_matrices(lam_re, lam_im, log_dt, b_re, b_im, c_re, c_im):
    hp = lax.Precision.HIGHEST
    t = S5_CHUNK
    dt = jnp.exp(log_dt.astype(F32))[..., None]
    lr, li = lam_re.astype(F32), lam_im.astype(F32)
    j = jnp.arange(t + 1, dtype=F32)[:, None, None, None]
    mag = jnp.exp(j * dt * lr)
    pr, pi = mag * jnp.cos(j * dt * li), mag * jnp.sin(j * dt * li)
    ar, ai = pr[1], pi[1]
    den = lr * lr + li * li
    fr = ((ar - 1.0) * lr + ai * li) / den
    fi = (ai * lr - (ar - 1.0) * li) / den
    bbr = fr[..., None] * b_re - fi[..., None] * b_im
    bbi = fr[..., None] * b_im + fi[..., None] * b_re
    car = c_re[None, None] * pr[:, :, :, None, :] - c_im[None, None] * pi[:, :, :, None, :]
    cai = c_re[None, None] * pi[:, :, :, None, :] + c_im[None, None] * pr[:, :, :, None, :]
    g = lr.shape[1]
    ca = jnp.concatenate([car, -cai], axis=-1).transpose(1, 2, 0, 3, 4).reshape(2, g, (t + 1) * S5_GROUP, 2 * S5_STATE)
    kj = jnp.einsum('dgxq,dgqe->dgxe', ca, jnp.concatenate([bbr, bbi], axis=2), precision=hp)
    kj = kj.reshape(2, g, t + 1, S5_GROUP, S5_GROUP).transpose(2, 0, 1, 3, 4)
    gpb = LANES // S5_GROUP
    nblk = g // gpb
    ns = gpb * S5_STATE
    own_chan = (jnp.arange(gpb)[:, None, None] == (jnp.arange(LANES) // S5_GROUP)[None, None, :])

    kl = jnp.concatenate([kj[t - 1:0:-1, 1], (kj[0, 0] + kj[0, 1])[None], kj[1:t, 0]], axis=0)
    kl = kl.reshape(2 * t - 1, nblk, gpb, S5_GROUP, S5_GROUP).transpose(0, 1, 4, 2, 3)
    kl = kl.reshape(2 * t - 1, nblk, 1, S5_GROUP, LANES)
    lagblk = jnp.where(own_chan, kl, 0.0).transpose(1, 0, 2, 3, 4).reshape(nblk, 2 * t - 1, LANES, LANES)

    def state_in(d, pw):
        prs = pr[pw, d].reshape(t, nblk, 1, ns).transpose(1, 0, 2, 3)
        pis = pi[pw, d].reshape(t, nblk, 1, ns).transpose(1, 0, 2, 3)
        lay = lambda q: q.reshape(nblk, gpb, S5_STATE, S5_GROUP).transpose(0, 3, 1, 2).reshape(nblk, 1, S5_GROUP, ns)
        br, bi = lay(bbr[d]), lay(bbi[d])
        return jnp.concatenate([prs * br - pis * bi, prs * bi + pis * br], axis=-1).astype(BF16)

    def state_out(d, pw):
        lay = lambda q: q.reshape(t, nblk, gpb, S5_GROUP, S5_STATE).transpose(1, 0, 3, 2, 4).reshape(
            nblk, t, S5_GROUP, ns)
        return jnp.concatenate([lay(car[pw, d]), lay(-cai[pw, d])], axis=-1).astype(BF16)

    coef = lambda d: jnp.stack([pr[t, d].reshape(nblk, ns), pi[t, d].reshape(nblk, ns)], axis=1)
    fwd = (lagblk.astype(BF16), state_in(0, t - 1 - jnp.arange(t)), state_out(0, jnp.arange(t) + 1), coef(0))
    bwd = (state_in(1, jnp.arange(t)), state_out(1, t - jnp.arange(t)), coef(1))
    return fwd, bwd


def _s5_rows(u5):
    nblk, nc, b, t, w = u5.shape
    return u5.reshape(nblk, nc * b, t * w)


def _s5_unrows(y, b):
    nblk, rows, kw = y.shape
    return y.reshape(nblk, rows // b, b, S5_CHUNK, kw // S5_CHUNK)


def _out_kernel(x_ref, gate_ref, yf_ref, yb_ref, bonus_ref, gr_ref, u_ref, gs_ref, ys_ref, lg_ref, lb_ref, d_ref,
                wg_ref, wo_ref, fg_ref, e_ref, o_ref):
    dr = bonus_ref.shape[2]
    y = yf_ref[0].astype(F32) + yb_ref[0].astype(F32)
    mean = _segsum(y, e_ref) * (1.0 / HEAD)
    dev = y - mean
    var = _segsum(dev * dev, e_ref) * (1.0 / HEAD)
    yn = dev * lax.rsqrt(var + LN_X_EPS) * lg_ref[...] + lb_ref[...]
    gr = gr_ref[0].astype(F32)
    y_r = (yn + bonus_ref[0].astype(F32)) * (gr * _sigmoid(gr))

    tm = u_ref.shape[1]
    ys = jnp.concatenate([ys_ref[j, :, 0].reshape(tm, LANES) for j in range(ys_ref.shape[0])], axis=1)
    s = ys.astype(F32) + d_ref[...] * u_ref[0].astype(F32)
    s = 0.5 * s * (1.0 + jnp.tanh(math.sqrt(2.0 / math.pi) * (s + 0.044715 * (s * s * s))))
    gl = jnp.dot(s.astype(BF16), wg_ref[...], preferred_element_type=F32)
    ds = gl.shape[1] // 2
    gs = gs_ref[0].astype(F32)
    y_s = gl[:, :ds] * _sigmoid(gl[:, ds:]) * (gs * _sigmoid(gs))

    out = (jnp.dot(y_r.astype(BF16), wo_ref[0:dr, :], preferred_element_type=F32)
           + jnp.dot(y_s.astype(BF16), wo_ref[dr:, :], preferred_element_type=F32))
    xo = x_ref[0] + gate_ref[0] * out
    ms = jnp.mean(xo * xo, axis=-1, keepdims=True)
    o_ref[0] = (xo * lax.rsqrt(ms + EPS) * fg_ref[...]).astype(o_ref.dtype)


def _out_stage(x, gate, yf, yb, bonus, z2, ys, lnx_g, lnx_b, s5_d, w_glu, w_out, final_g, e):
    b, l, d = x.shape
    dr = bonus.shape[2]
    ds = ys.shape[0] * LANES
    assert dr == ds
    tm = TOKENS_OUT
    tok = lambda w, cb: pl.BlockSpec((1, tm, w), lambda i, j: (i, j, cb))
    const = lambda *shape: pl.BlockSpec(shape, lambda i, j: (0,) * len(shape))
    return pl.pallas_call(
        _out_kernel,
        grid=(b, l // tm),
        in_specs=[tok(d, 0), pl.BlockSpec((1, 1, d), lambda i, j: (i, 0, 0)),
                  tok(dr, 0), tok(dr, 0), tok(dr, 0), tok(dr, 0), tok(ds, 1), tok(ds, 2),
                  pl.BlockSpec((ys.shape[0], tm // S5_CHUNK, 1, S5_CHUNK, LANES), lambda i, j: (0, j, i, 0, 0)),
                  const(1, dr), const(1, dr), const(1, ds), const(ds, 2 * ds), const(dr + ds, d),
                  const(1, d), const(dr, dr)],
        out_specs=tok(d, 0),
        out_shape=jax.ShapeDtypeStruct((b, l, d), x.dtype),
        compiler_params=_params("arbitrary", "arbitrary"),
        name="out_stage",
    )(x, gate, yf, yb, bonus, z2, z2, z2, ys, lnx_g.reshape(1, dr), lnx_b.reshape(1, dr), s5_d.reshape(1, ds),
      w_glu.astype(BF16), w_out.astype(BF16), final_g.reshape(1, d), e)


def kernel(x, c, ctx, c_ctx, norm_g, w_ada, b_ada, w_in, mu_shift, rwkv_w0, rwkv_w2, rwkv_a0, rwkv_a2, rwkv_k_k, rwkv_k_a, rwkv_r_k, lnx_g, lnx_b, s5_lam_re, s5_lam_im, s5_log_dt, s5_b_re, s5_b_im, s5_c_re, s5_c_im, s5_d, s5_w_glu, w_out, final_g):
    assert norm_g.shape[0] == 1, "one layer"
    b, l, d = x.shape
    lc = ctx.shape[1]
    dr = rwkv_k_k.shape[1]
    ds = s5_d.shape[1]
    sw = mu_shift.shape[1]
    assert sw == 3 * dr + 4 * LORA and b % 8 == 0 and lc % WKV_CHUNK == 0
    assert l % TOKENS_PREP == 0 and TOKENS_PREP % GRID_W == 0 and l % TOKENS_OUT == 0 and l % TOKENS_IN == 0

    cond = jnp.zeros((2 * b, d), F32).at[:b].set(c).at[b].set(c_ctx)
    m = _modulation(cond, w_ada[0], b_ada[0])
    shift, scale, gate = m[:b, :d], m[:b, d:2 * d], m[:b, 2 * d:]
    cshift = jnp.broadcast_to(m[b, :d], (b, d))
    cscale = jnp.broadcast_to(m[b, d:2 * d], (b, d))

    w1 = w_in[0][:, :sw].astype(BF16)
    w2 = w_in[0][:, sw:].astype(BF16)
    z1c, _, u5c = _inproj(ctx, norm_g[0], cscale[:, None], cshift[:, None], w1, w2)
    z1, z2, u5 = _inproj(x, norm_g[0], scale[:, None], shift[:, None], w1, w2)

    hid = jnp.arange(dr) // HEAD
    e = (hid[:, None] == hid[None, :]).astype(BF16)
    pp = (mu_shift, rwkv_w0[0], rwkv_w2[0], rwkv_a0[0], rwkv_a2[0], rwkv_k_k, rwkv_k_a,
          rwkv_r_k[0].reshape(1, dr), e)
    rc, vc, aac, wc, kdc, bbc, _ = _prep(z1c, pp, grid2d=False)
    rx, vx, aax, wx, kdx, bbx, bonus = _prep(z1, pp, grid2d=True)

    s0 = jnp.zeros((2, b, dr // (WKV_PACK * HEAD), HEAD, WKV_PACK * HEAD), F32)
    _, _, s_ctx = _wkv(wc, kdc, bbc, aac, rc, vc, s0)
    yf, yb, _ = _wkv(wx, kdx, bbx, aax, rx, vx, s_ctx)

    s5f, s5b = _s5_matrices(s5_lam_re[0], s5_lam_im[0], s5_log_dt[0], s5_b_re[0], s5_b_im[0],
                            s5_c_re[0], s5_c_im[0])
    uc, ux = _s5_rows(u5c), _s5_rows(u5)
    h0 = jnp.zeros((uc.shape[0], 2, b, s5f[3].shape[2]), F32)
    ysc, hcf = _s5_scan(uc, *s5f, h0, b, False)
    _, hcb = _s5_scan(uc, ysc, *s5b, h0, b, True)
    ysf, _ = _s5_scan(ux, *s5f, hcf, b, False)
    ysfb, _ = _s5_scan(ux, ysf, *s5b, hcb, b, True)
    ys = _s5_unrows(ysfb, b)

    return _out_stage(x, gate[:, None], yf, yb, bonus, z2, ys, lnx_g[0], lnx_b[0], s5_d[0], s5_w_glu[0],
                      w_out[0], final_g, e)
```
